```python
import math
import jax
import jax.numpy as jnp
from jax import lax
import numpy as np

D_MODEL = 2048
BATCH = 2
SEQ = 4096
DEPTH = 4
DEC_BATCH = 8
DEC_SEQ = 8
PAST_LEN = 16384
PAGE_SIZE = 128

N_BRANCH = 4
W_BR = D_MODEL // 2
CONV_K = 4
LRU_W = W_BR
LRU_BLOCKS = 8
LRU_BS = LRU_W // LRU_BLOCKS
LRU_C = 8.0
NSA_HD = 128
NSA_H = W_BR // NSA_HD
NSA_KVH = 2
NSA_GQ = NSA_H // NSA_KVH
NSA_W = NSA_H * NSA_HD
CMP_BLOCK = 32
CMP_STRIDE = 16
SEL_BLOCK = 64
CMP_PER_SEL = SEL_BLOCK // CMP_STRIDE
TOP_N = 16
WINDOW = 512
Q_BLOCK = 128
GDN_HD = 128
GDN_H = W_BR // GDN_HD
GDN_W = GDN_H * GDN_HD
GDN_CHUNK = 64
S5_W = W_BR
S5_GS = 16
S5_G = S5_W // S5_GS
S5_P = 64
DN_ALPHA = (2.0 * DEPTH) ** 0.25
DN_BETA = (8.0 * DEPTH) ** -0.25
NEG = -1e30
FORCE = 1e9
IN_SIZES = (LRU_W, LRU_W, NSA_W, 6 * NSA_KVH * NSA_HD, 3 * NSA_H, NSA_W,
            3 * GDN_W, GDN_H, GDN_H, GDN_W, S5_W, S5_W, N_BRANCH * D_MODEL)
N_IN = sum(IN_SIZES)

kernel_name = 'hybrid_lru_nsa_gdn_s5_step'

F32 = jnp.float32


def layer_norm(x, g, b, eps=1e-5):
    x = x.astype(F32)
    mu = jnp.mean(x, axis=-1, keepdims=True)
    var = jnp.mean(jnp.square(x - mu), axis=-1, keepdims=True)
    return (x - mu) * lax.rsqrt(var + eps) * g + b


def rms_norm(x, g, eps=1e-6):
    return x * lax.rsqrt(jnp.mean(jnp.square(x), axis=-1, keepdims=True) + eps) * g


def l2norm(x, eps=1e-6):
    return x * lax.rsqrt(jnp.sum(jnp.square(x), axis=-1, keepdims=True) + eps)


def masked_softmax(s, mask, axis):
    p = jax.nn.softmax(jnp.where(mask, s, NEG), axis=axis)
    return jnp.where(mask, p, 0.0)


def split_cols(z):
    idx = np.cumsum(IN_SIZES)[:-1].tolist()
    return jnp.split(z, idx, axis=-1)


def causal_conv(x, buf, w):
    xp = jnp.concatenate([buf.astype(x.dtype), x], axis=1)
    t = x.shape[1]
    y = sum(xp[:, j:j + t] * w[j] for j in range(CONV_K))
    return y, xp[:, -(CONV_K - 1):]


def linear_recurrence(a, b, h0):
    b = b.at[:, 0].add(a[:, 0] * h0)

    def comb(e1, e2):
        a1, b1 = e1
        a2, b2 = e2
        return a1 * a2, a2 * b1 + b2

    _, h = lax.associative_scan(comb, (a, b), axis=1)
    return h


def complex_combine(e1, e2):
    ar1, ai1, br1, bi1 = e1
    ar2, ai2, br2, bi2 = e2
    return (ar2 * ar1 - ai2 * ai1, ar2 * ai1 + ai2 * ar1,
            ar2 * br1 - ai2 * bi1 + br2, ar2 * bi1 + ai2 * br1 + bi2)


def rglru_mixer(u, buf, h0, w):
    b_, t, _ = u.shape
    xc, buf_new = causal_conv(u, buf, w['lru_conv_w'])
    xc = (xc + w['lru_conv_b']).astype(F32)
    xb = xc.reshape(b_, t, LRU_BLOCKS, LRU_BS)
    r = jax.nn.sigmoid(jnp.einsum('btnc,ncd->btnd', xb, w['lru_wa']).reshape(b_, t, LRU_W) + w['lru_ba'])
    i = jax.nn.sigmoid(jnp.einsum('btnc,ncd->btnd', xb, w['lru_wx']).reshape(b_, t, LRU_W) + w['lru_bx'])
    log_a = -LRU_C * r * jax.nn.softplus(-w['lru_lambda'])
    a = jnp.exp(log_a)
    b = jnp.sqrt(-jnp.expm1(2.0 * log_a)) * (i * xc)
    h = linear_recurrence(a, b, h0.astype(F32))
    return h, buf_new, h[:, -1]


def compress_blocks(k, v, w):
    b_, tc = k.shape[:2]
    nh = -(-tc // CMP_STRIDE)
    kv = jnp.stack([k, v], axis=2)
    kv = jnp.pad(kv, ((0, 0), (0, nh * CMP_STRIDE - tc), (0, 0), (0, 0), (0, 0)))
    halves = kv.reshape(b_, nh, CMP_STRIDE, 2, NSA_KVH, NSA_HD).transpose(0, 1, 3, 4, 2, 5)
    pe = w['nsa_cmp_pos'].reshape(2, CMP_STRIDE, NSA_HD)
    w1 = w['nsa_cmp_w1'].reshape(2, 2, CMP_STRIDE, NSA_HD, NSA_HD)
    h_first = jnp.einsum('bncgsd,csde->bncge', halves + pe[0], w1[:, 0])
    h_second = jnp.einsum('bncgsd,csde->bncge', halves + pe[1], w1[:, 1])
    hid = jax.nn.silu(h_first[:, :-1] + h_second[:, 1:])
    out = jnp.einsum('bncge,ced->bncgd', hid, w['nsa_cmp_w2'])
    c_end = jnp.arange(nh - 1, dtype=jnp.int32) * CMP_STRIDE + (CMP_BLOCK - 1)
    return out[:, :, 0], out[:, :, 1], c_end


def nsa_block_attend(q, gates, q_pos, kc, vc, c_end, ks_t, vs_t, kw, vw, w_pos, slopes):
    b_, qb = q.shape[:2]
    ns = ks_t.shape[2]
    nc = kc.shape[1]
    qg = q.reshape(b_, qb, NSA_KVH, NSA_GQ, NSA_HD)
    sl = slopes.reshape(NSA_KVH, NSA_GQ)
    dist_c = q_pos[:, None] - c_end[None, :]
    s_c = jnp.einsum('bqgjd,bngd->bgjqn', qg, kc) - sl[:, :, None, None] * dist_c.astype(F32)
    p_c = masked_softmax(s_c, dist_c >= 0, -1)
    o_c = jnp.einsum('bgjqn,bngd->bqgjd', p_c, vc)
    imp = jnp.pad(p_c.sum(axis=2), ((0, 0), (0, 0), (0, 0), (0, ns * CMP_PER_SEL - nc)))
    imp = imp.reshape(b_, NSA_KVH, qb, ns, CMP_PER_SEL).sum(-1)
    blk = jnp.arange(ns, dtype=jnp.int32)
    forced = (blk[None, :] == (q_pos // SEL_BLOCK)[:, None]) | (blk[None, :] == 0)
    future = blk[None, :] * SEL_BLOCK > q_pos[:, None]
    imp = jnp.where(forced, FORCE, jnp.where(future, -FORCE, imp))
    _, sel = lax.top_k(imp, min(TOP_N, ns))
    gather = jax.vmap(jax.vmap(lambda rows, idx: rows[idx]))
    k_s = gather(ks_t, sel)
    v_s = gather(vs_t, sel)
    key_pos = sel[..., None] * SEL_BLOCK + jnp.arange(SEL_BLOCK, dtype=jnp.int32)
    dist_s = (q_pos[:, None, None] - key_pos)[:, :, None]
    s_s = jnp.einsum('bqgjd,bgqksd->bgjqks', qg, k_s) - sl[None, :, :, None, None, None] * dist_s.astype(F32)
    p_s = masked_softmax(s_s, dist_s >= 0, (-2, -1))
    o_s = jnp.einsum('bgjqks,bgqksd->bqgjd', p_s, v_s)
    dist_w = q_pos[:, None] - w_pos[None, :]
    m_w = (dist_w >= 0) & (dist_w < WINDOW) & (w_pos[None, :] >= 0)
    s_w = jnp.einsum('bqgjd,blgd->bgjql', qg, kw) - sl[:, :, None, None] * dist_w.astype(F32)
    p_w = masked_softmax(s_w, m_w, -1)
    o_w = jnp.einsum('bgjql,blgd->bqgjd', p_w, vw)
    gb = gates.reshape(b_, qb, NSA_KVH, NSA_GQ, 3)
    o = gb[..., 0:1] * o_c + gb[..., 1:2] * o_s + gb[..., 2:3] * o_w
    return o.reshape(b_, qb, NSA_W)


def nsa_mixer(q, kv, gl, kv_past, win_past, w, slopes):
    b_, t, _ = q.shape
    q = q.reshape(b_, t, NSA_H, NSA_HD).astype(F32) * NSA_HD ** -0.5
    gates = jax.nn.sigmoid(gl.reshape(b_, t, NSA_H, 3).astype(F32))
    kv = kv.reshape(b_, t, 6, NSA_KVH, NSA_HD).astype(F32)
    kv_rows = kv[:, :, :4]
    ctx = jnp.concatenate([kv_past.astype(F32), kv_rows], axis=1)
    tc = ctx.shape[1]
    pos0 = tc - t
    kc, vc, c_end = compress_blocks(ctx[:, :, 0], ctx[:, :, 1], w)
    ns = -(-tc // SEL_BLOCK)
    sel_kv = jnp.pad(ctx[:, :, 2:4], ((0, 0), (0, ns * SEL_BLOCK - tc), (0, 0), (0, 0), (0, 0)))
    sel_kv = sel_kv.reshape(b_, ns, SEL_BLOCK, 2, NSA_KVH, NSA_HD).transpose(3, 0, 4, 1, 2, 5)
    win = jnp.concatenate([win_past.astype(F32), kv[:, :, 4:6]], axis=1)
    lw = win_past.shape[1]
    win_pad = jnp.pad(win, ((0, 0), (WINDOW, 0), (0, 0), (0, 0), (0, 0)))
    qb = min(Q_BLOCK, t)
    n_blk = t // qb

    def body(i):
        q0 = i * qb
        q_b = lax.dynamic_slice_in_dim(q, q0, qb, axis=1)
        g_b = lax.dynamic_slice_in_dim(gates, q0, qb, axis=1)
        w_b = lax.dynamic_slice_in_dim(win_pad, q0 + lw, WINDOW + qb, axis=1)
        q_pos = pos0 + q0 + jnp.arange(qb, dtype=jnp.int32)
        w_pos = pos0 + q0 - WINDOW + jnp.arange(WINDOW + qb, dtype=jnp.int32)
        return nsa_block_attend(q_b, g_b, q_pos, kc, vc, c_end, sel_kv[0], sel_kv[1],
                                w_b[:, :, 0], w_b[:, :, 1], w_pos, slopes)

    o = lax.map(body, jnp.arange(n_blk, dtype=jnp.int32))
    o = o.transpose(1, 0, 2, 3).reshape(b_, t, NSA_W)
    win_new = win[:, -min(WINDOW, tc):]
    return o, kv_rows, win_new


def gated_delta_chunked(q, k, v, g, beta, s0):
    b_, t, h, dk = q.shape
    dv = v.shape[-1]
    c = GDN_CHUNK
    n = -(-t // c)
    pad = n * c - t

    def prep(x):
        x = jnp.pad(x, [(0, 0), (0, pad)] + [(0, 0)] * (x.ndim - 2))
        x = x.reshape((b_, n, c) + x.shape[2:])
        return jnp.moveaxis(jnp.moveaxis(x, 3, 2), 1, 0)

    q = prep(q * dk ** -0.5)
    k = prep(k)
    v = prep(v)
    g = prep(g)
    beta = prep(beta)
    gc = jnp.cumsum(g, axis=-1)
    diff = gc[..., :, None] - gc[..., None, :]
    incl = jnp.tril(jnp.ones((c, c), bool))
    strict = jnp.tril(jnp.ones((c, c), bool), -1)
    decay = jnp.where(incl, jnp.exp(jnp.where(incl, diff, 0.0)), 0.0)
    kb = k * beta[..., None]
    m = jnp.where(strict, jnp.einsum('nbhcd,nbhsd->nbhcs', kb, k) * decay, 0.0)
    a_mat = m + jnp.eye(c, dtype=F32)
    rhs = jnp.concatenate([v * beta[..., None], kb * jnp.exp(gc)[..., None]], axis=-1)
    sol = lax.linalg.triangular_solve(a_mat, rhs, left_side=True, lower=True, unit_diagonal=True)
    u, wk = sol[..., :dv], sol[..., dv:]
    qk = jnp.einsum('nbhcd,nbhsd->nbhcs', q, k) * decay

    def step(s, xs):
        q_i, k_i, u_i, w_i, qk_i, gc_i = xs
        v_new = u_i - jnp.einsum('bhcd,bhde->bhce', w_i, s)
        o = jnp.einsum('bhcd,bhde->bhce', q_i * jnp.exp(gc_i)[..., None], s) + jnp.einsum('bhcs,bhse->bhce', qk_i, v_new)
        g_last = gc_i[..., -1]
        s = s * jnp.exp(g_last)[..., None, None] + jnp.einsum(
            'bhcd,bhce->bhde', k_i * jnp.exp(g_last[..., None] - gc_i)[..., None], v_new)
        return s, o

    s_fin, o = lax.scan(step, s0.astype(F32), (q, k, u, wk, qk, gc))
    o = o.transpose(1, 0, 3, 2, 4).reshape(b_, n * c, h, dv)[:, :t]
    return o, s_fin


def gdn_mixer(qkv, a_in, b_in, buf, s0, w):
    b_, t, _ = qkv.shape
    cv, buf_new = causal_conv(qkv, buf, w['gdn_conv_w'])
    cv = jax.nn.silu(cv.astype(F32))
    q, k, v = jnp.split(cv, 3, axis=-1)
    q = l2norm(q.reshape(b_, t, GDN_H, GDN_HD))
    k = l2norm(k.reshape(b_, t, GDN_H, GDN_HD))
    v = v.reshape(b_, t, GDN_H, GDN_HD)
    g = -jnp.exp(w['gdn_a_log']) * jax.nn.softplus(a_in.astype(F32) + w['gdn_dt_bias'])
    beta = jax.nn.sigmoid(b_in.astype(F32))
    o, s_new = gated_delta_chunked(q, k, v, g, beta, s0)
    o = rms_norm(o, w['gdn_norm_w'])
    return o.reshape(b_, t, GDN_W), buf_new, s_new


def s5_mixer(u, h0_re, h0_im, w):
    b_, t, _ = u.shape
    uf = u.astype(F32)
    dt = jnp.exp(w['s5_log_dt'])[:, None]
    lr, li = w['s5_lam_re'], w['s5_lam_im']
    mag = jnp.exp(lr * dt)
    a_re = mag * jnp.cos(li * dt)
    a_im = mag * jnp.sin(li * dt)
    den = lr * lr + li * li
    f_re = ((a_re - 1.0) * lr + a_im * li) / den
    f_im = (a_im * lr - (a_re - 1.0) * li) / den
    bb_re = f_re[..., None] * w['s5_b_re'] - f_im[..., None] * w['s5_b_im']
    bb_im = f_re[..., None] * w['s5_b_im'] + f_im[..., None] * w['s5_b_re']
    ug = uf.reshape(b_, t, S5_G, S5_GS)
    bu_re = jnp.einsum('btgc,gpc->tbgp', ug, bb_re)
    bu_im = jnp.einsum('btgc,gpc->tbgp', ug, bb_im)
    h0_re = h0_re.astype(F32)
    h0_im = h0_im.astype(F32)
    bu_re = bu_re.at[0].add(a_re * h0_re - a_im * h0_im)
    bu_im = bu_im.at[0].add(a_re * h0_im + a_im * h0_re)
    a_re_t = jnp.broadcast_to(a_re, (t, 1, S5_G, S5_P))
    a_im_t = jnp.broadcast_to(a_im, (t, 1, S5_G, S5_P))
    _, _, h_re, h_im = lax.associative_scan(complex_combine, (a_re_t, a_im_t, bu_re, bu_im), axis=0)
    y = jnp.einsum('tbgp,gcp->btgc', h_re, w['s5_c_re']) - jnp.einsum('tbgp,gcp->btgc', h_im, w['s5_c_im'])
    y = y.reshape(b_, t, S5_W) + w['s5_d'] * uf
    y = jax.nn.gelu(y)
    gl = jnp.einsum('btw,we->bte', y, w['s5_glu_w'])
    y = gl[..., :S5_W] * jax.nn.sigmoid(gl[..., S5_W:])
    return y, h_re[-1], h_im[-1]


def mixer_layer(x, lru_buf, lru_h0, kv_past, win_past, gdn_buf, gdn_s0, s5_re0, s5_im0, w, slopes):
    b_, t, _ = x.shape
    z = jnp.einsum('btd,de->bte', x, w['w_in'])
    (u_lru, g_lru, q_nsa, kv_nsa, gl_nsa, g_nsa, qkv_gdn, a_gdn, b_gdn, g_gdn,
     u_s5, g_s5, merge) = split_cols(z)
    y_lru, lru_buf_new, lru_h = rglru_mixer(u_lru, lru_buf, lru_h0, w)
    y_nsa, kv_rows, win_new = nsa_mixer(q_nsa, kv_nsa, gl_nsa, kv_past, win_past, w, slopes)
    y_gdn, gdn_buf_new, gdn_s = gdn_mixer(qkv_gdn, a_gdn, b_gdn, gdn_buf, gdn_s0, w)
    y_s5, s5_re, s5_im = s5_mixer(u_s5, s5_re0, s5_im0, w)
    ybr = jnp.stack([y_lru * jax.nn.silu(g_lru), y_nsa * jax.nn.silu(g_nsa),
                     y_gdn * jax.nn.silu(g_gdn), y_s5 * jax.nn.silu(g_s5)], axis=2)
    proj = jnp.einsum('btmw,mwd->btmd', ybr, w['w_branch'])
    gates = jax.nn.sigmoid(merge.reshape(b_, t, N_BRANCH, D_MODEL).astype(F32))
    out = jnp.einsum('btd,de->bte', jnp.sum(gates * proj, axis=2), w['w_out'])
    x_new = layer_norm(DN_ALPHA * x.astype(F32) + out, w['ln_g'], w['ln_b']).astype(x.dtype)
    return x_new, (lru_buf_new, lru_h, kv_rows, win_new, gdn_buf_new, gdn_s, s5_re, s5_im)


def setup_inputs(seed: int = 0) -> dict:
    key = jax.random.key(seed)
    ks = iter(jax.random.split(key, 48))

    def nrm(shape, scale):
        return scale * jax.random.normal(next(ks), shape, F32)

    def uni(shape, lo, hi):
        return jax.random.uniform(next(ks), shape, F32, lo, hi)

    n_pages = PAST_LEN // PAGE_SIZE
    n_pool = (DEC_BATCH * n_pages * 5) // 4
    w_buf = min(WINDOW, PAST_LEN)
    x_prompt = nrm((BATCH, SEQ, D_MODEL), 1.0)
    x_sample = nrm((DEC_BATCH, DEC_SEQ, D_MODEL), 1.0)
    state_lru_h = nrm((DEPTH, DEC_BATCH, LRU_W), 0.5)
    state_lru_conv = nrm((DEPTH, DEC_BATCH, CONV_K - 1, LRU_W), 1.0)
    cache_nsa_kv = nrm((DEPTH, n_pool, PAGE_SIZE, 4, NSA_KVH, NSA_HD), 1.0)
    cache_win_kv = nrm((DEPTH, DEC_BATCH, w_buf, 2, NSA_KVH, NSA_HD), 1.0)
    state_gdn_s = nrm((DEPTH, DEC_BATCH, GDN_H, GDN_HD, GDN_HD), GDN_HD ** -0.5)
    state_gdn_conv = nrm((DEPTH, DEC_BATCH, CONV_K - 1, 3 * GDN_W), 1.0)
    state_s5_re = nrm((DEPTH, DEC_BATCH, S5_G, S5_P), 0.1)
    state_s5_im = nrm((DEPTH, DEC_BATCH, S5_G, S5_P), 0.1)
    page_table = jax.random.permutation(next(ks), n_pool)[:DEC_BATCH * n_pages].reshape(
        DEC_BATCH, n_pages).astype(jnp.int32)
    w_in = nrm((DEPTH, D_MODEL, N_IN), D_MODEL ** -0.5)
    lru_conv_w = nrm((DEPTH, CONV_K, LRU_W), 0.5)
    lru_conv_b = nrm((DEPTH, LRU_W), 0.02)
    lru_wa = nrm((DEPTH, LRU_BLOCKS, LRU_BS, LRU_BS), LRU_BS ** -0.5)
    lru_ba = nrm((DEPTH, LRU_W), 0.1)
    lru_wx = nrm((DEPTH, LRU_BLOCKS, LRU_BS, LRU_BS), LRU_BS ** -0.5)
    lru_bx = nrm((DEPTH, LRU_W), 0.1)
    lru_a = uni((DEPTH, LRU_W), 0.9, 0.999)
    lru_lambda = jnp.log(lru_a) - jnp.log1p(-lru_a)
    nsa_cmp_pos = nrm((DEPTH, CMP_BLOCK, NSA_HD), 0.02)
    nsa_cmp_w1 = nrm((DEPTH, 2, CMP_BLOCK * NSA_HD, NSA_HD), (CMP_BLOCK * NSA_HD) ** -0.5)
    nsa_cmp_w2 = nrm((DEPTH, 2, NSA_HD, NSA_HD), NSA_HD ** -0.5)
    gdn_conv_w = nrm((DEPTH, CONV_K, 3 * GDN_W), 0.5)
    gdn_a_log = jnp.log(uni((DEPTH, GDN_H), 1.0, 16.0))
    gdn_dt = jnp.exp(uni((DEPTH, GDN_H), math.log(1e-3), math.log(1e-1)))
    gdn_dt_bias = gdn_dt + jnp.log(-jnp.expm1(-gdn_dt))
    gdn_norm_w = 1.0 + nrm((DEPTH, GDN_HD), 0.02)
    s5_lam_re = -0.5 + nrm((DEPTH, S5_G, S5_P), 0.01)
    s5_lam_im = math.pi * jnp.arange(S5_P, dtype=F32) + nrm((DEPTH, S5_G, S5_P), 0.01)
    s5_log_dt = uni((DEPTH, S5_G), math.log(1e-3), math.log(1e-1))
    s5_b_re = nrm((DEPTH, S5_G, S5_P, S5_GS), (2.0 * S5_GS) ** -0.5)
    s5_b_im = nrm((DEPTH, S5_G, S5_P, S5_GS), (2.0 * S5_GS) ** -0.5)
    s5_c_re = nrm((DEPTH, S5_G, S5_GS, S5_P), (2.0 * S5_P) ** -0.5)
    s5_c_im = nrm((DEPTH, S5_G, S5_GS, S5_P), (2.0 * S5_P) ** -0.5)
    s5_d = nrm((DEPTH, S5_W), 1.0)
    s5_glu_w = nrm((DEPTH, S5_W, 2 * S5_W), S5_W ** -0.5)
    w_branch = nrm((DEPTH, N_BRANCH, W_BR, D_MODEL), W_BR ** -0.5 * DN_BETA)
    w_out = nrm((DEPTH, D_MODEL, D_MODEL), D_MODEL ** -0.5 * DN_BETA)
    ln_g = 1.0 + nrm((DEPTH, D_MODEL), 0.02)
    ln_b = nrm((DEPTH, D_MODEL), 0.02)
    return {'x_prompt': x_prompt, 'x_sample': x_sample,
            'state_lru_h': state_lru_h, 'state_lru_conv': state_lru_conv,
            'cache_nsa_kv': cache_nsa_kv, 'cache_win_kv': cache_win_kv,
            'state_gdn_s': state_gdn_s, 'state_gdn_conv': state_gdn_conv,
            'state_s5_re': state_s5_re, 'state_s5_im': state_s5_im, 'page_table': page_table,
            'w_in': w_in, 'lru_conv_w': lru_conv_w, 'lru_conv_b': lru_conv_b, 'lru_wa': lru_wa,
            'lru_ba': lru_ba, 'lru_wx': lru_wx, 'lru_bx': lru_bx, 'lru_lambda': lru_lambda,
            'nsa_cmp_pos': nsa_cmp_pos, 'nsa_cmp_w1': nsa_cmp_w1, 'nsa_cmp_w2': nsa_cmp_w2,
            'gdn_conv_w': gdn_conv_w, 'gdn_a_log': gdn_a_log, 'gdn_dt_bias': gdn_dt_bias,
            'gdn_norm_w': gdn_norm_w, 's5_lam_re': s5_lam_re, 's5_lam_im': s5_lam_im,
            's5_log_dt': s5_log_dt, 's5_b_re': s5_b_re, 's5_b_im': s5_b_im, 's5_c_re': s5_c_re,
            's5_c_im': s5_c_im, 's5_d': s5_d, 's5_glu_w': s5_glu_w, 'w_branch': w_branch,
            'w_out': w_out, 'ln_g': ln_g, 'ln_b': ln_b}


def reference(x_prompt, x_sample, state_lru_h, state_lru_conv, cache_nsa_kv, cache_win_kv,
              state_gdn_s, state_gdn_conv, state_s5_re, state_s5_im, page_table,
              w_in, lru_conv_w, lru_conv_b, lru_wa, lru_ba, lru_wx, lru_bx, lru_lambda,
              nsa_cmp_pos, nsa_cmp_w1, nsa_cmp_w2, gdn_conv_w, gdn_a_log, gdn_dt_bias, gdn_norm_w,
              s5_lam_re, s5_lam_im, s5_log_dt, s5_b_re, s5_b_im, s5_c_re, s5_c_im, s5_d, s5_glu_w,
              w_branch, w_out, ln_g, ln_b):
    slopes = jnp.exp2(-8.0 * jnp.arange(1, NSA_H + 1, dtype=F32) / NSA_H)
    bp = x_prompt.shape[0]
    db, n_pages = page_table.shape
    past_len = n_pages * PAGE_SIZE
    zero_state = (jnp.zeros((bp, CONV_K - 1, LRU_W), F32), jnp.zeros((bp, LRU_W), F32),
                  jnp.zeros((bp, 0, 4, NSA_KVH, NSA_HD), F32), jnp.zeros((bp, 0, 2, NSA_KVH, NSA_HD), F32),
                  jnp.zeros((bp, CONV_K - 1, 3 * GDN_W), F32), jnp.zeros((bp, GDN_H, GDN_HD, GDN_HD), F32),
                  jnp.zeros((bp, S5_G, S5_P), F32), jnp.zeros((bp, S5_G, S5_P), F32))
    xp, xs = x_prompt, x_sample
    p_new, s_new = [], []
    for l in range(DEPTH):
        w = dict(w_in=w_in[l], lru_conv_w=lru_conv_w[l], lru_conv_b=lru_conv_b[l], lru_wa=lru_wa[l],
                 lru_ba=lru_ba[l], lru_wx=lru_wx[l], lru_bx=lru_bx[l], lru_lambda=lru_lambda[l],
                 nsa_cmp_pos=nsa_cmp_pos[l], nsa_cmp_w1=nsa_cmp_w1[l], nsa_cmp_w2=nsa_cmp_w2[l],
                 gdn_conv_w=gdn_conv_w[l], gdn_a_log=gdn_a_log[l], gdn_dt_bias=gdn_dt_bias[l],
                 gdn_norm_w=gdn_norm_w[l], s5_lam_re=s5_lam_re[l], s5_lam_im=s5_lam_im[l],
                 s5_log_dt=s5_log_dt[l], s5_b_re=s5_b_re[l], s5_b_im=s5_b_im[l], s5_c_re=s5_c_re[l],
                 s5_c_im=s5_c_im[l], s5_d=s5_d[l], s5_glu_w=s5_glu_w[l], w_branch=w_branch[l],
                 w_out=w_out[l], ln_g=ln_g[l], ln_b=ln_b[l])
        kv_past = cache_nsa_kv[l][page_table].reshape(db, past_len, 4, NSA_KVH, NSA_HD)
        xp, st_p = mixer_layer(xp, *zero_state, w, slopes)
        xs, st_s = mixer_layer(xs, state_lru_conv[l], state_lru_h[l], kv_past, cache_win_kv[l],
                               state_gdn_conv[l], state_gdn_s[l], state_s5_re[l], state_s5_im[l], w, slopes)
        p_new.append(st_p)
        s_new.append(st_s)
    p_lru_conv, p_lru_h, p_nsa_kv, p_win_kv, p_gdn_conv, p_gdn_s, p_s5_re, p_s5_im = [
        jnp.stack(z) for z in zip(*p_new)]
    s_lru_conv, s_lru_h, s_nsa_kv, s_win_kv, s_gdn_conv, s_gdn_s, s_s5_re, s_s5_im = [
        jnp.stack(z) for z in zip(*s_new)]
    return (xp, xs, p_lru_h, p_lru_conv, p_nsa_kv, p_win_kv, p_gdn_s, p_gdn_conv, p_s5_re, p_s5_im,
            s_lru_h, s_lru_conv, s_nsa_kv, s_win_kv, s_gdn_s, s_gdn_conv, s_s5_re, s_s5_im)
```

```python
import functools

import jax
import jax.numpy as jnp
from jax import lax
from jax.experimental import pallas as pl
from jax.experimental.pallas import tpu as pltpu

F32 = jnp.float32
BF16 = jnp.bfloat16
HI = lax.Precision.HIGHEST

D_MODEL = 2048
W_BR = D_MODEL // 2
N_BRANCH = 4
CONV_K = 4
LRU_BLOCKS = 8
LRU_BS = W_BR // LRU_BLOCKS
LRU_C = 8.0
NSA_HD = 128
NSA_H = 8
NSA_KVH = 2
NSA_GQ = NSA_H // NSA_KVH
CMP_STRIDE = 16
CMP_BLOCK = 32
SEL_BLOCK = 64
CMP_PER_SEL = SEL_BLOCK // CMP_STRIDE
TOP_N = 16
WINDOW = 512
Q_BLOCK = 128
PAGE_SIZE = 128
GDN_HD = 128
GDN_H = 8
GDN_CHUNK = 64
S5_GS = 16
S5_G = W_BR // S5_GS
S5_P = 64
S5_L = 8
NEG = -1e30
FORCE = 1e9

LANES = 128
SUBLANES = 8
VMEM_LIMIT = 56 * 1024 * 1024

C_ULRU = 0
C_GLRU = 1024
C_QNSA = 2048
C_GNSA = 3072
C_GGDN = 4096
C_US5 = 5120
C_GS5 = 6144
C_QKV = 7168
C_KV = 10240
C_SMALL = 11776
C_MERGE = 12288
NP = C_MERGE + N_BRANCH * D_MODEL
SM_GL = 0
SM_A = 24
SM_B = 32


def _cparams(sem):
    return pltpu.CompilerParams(dimension_semantics=sem, vmem_limit_bytes=VMEM_LIMIT)


def _sigmoid(x):
    return 1.0 / (1.0 + jnp.exp(-x))


def _silu(x):
    return x * _sigmoid(x)


def _softplus(x):
    return jnp.maximum(x, 0.0) + jnp.log1p(jnp.exp(-jnp.abs(x)))


def _dot(a, b):
    return jnp.dot(a, b, preferred_element_type=F32)


def _dot_hi(a, b):
    return jnp.dot(a, b, preferred_element_type=F32, precision=HI)


def _dot_nt(a, b):
    return lax.dot_general(a, b, (((1,), (1,)), ((), ())), preferred_element_type=F32)


def _dot_tn(a, b):
    return lax.dot_general(a, b, (((0,), (0,)), ((), ())), preferred_element_type=F32)


def _iota(shape, axis):
    return lax.broadcasted_iota(jnp.int32, shape, axis)


def _pack_w_in(w_in):
    o = 0
    seg = {}
    for name, size in (("u_lru", W_BR), ("g_lru", W_BR), ("q_nsa", W_BR), ("kv", 6 * NSA_KVH * NSA_HD),
                       ("gl", 3 * NSA_H), ("g_nsa", W_BR), ("qkv", 3 * W_BR), ("a", GDN_H), ("b", GDN_H),
                       ("g_gdn", W_BR), ("u_s5", W_BR), ("g_s5", W_BR), ("merge", N_BRANCH * D_MODEL)):
        seg[name] = w_in[..., o:o + size]
        o += size
    pad = jnp.zeros(w_in.shape[:-1] + (C_MERGE - C_SMALL - 40,), w_in.dtype)
    packed = jnp.concatenate(
        [seg["u_lru"], seg["g_lru"], seg["q_nsa"], seg["g_nsa"], seg["g_gdn"], seg["u_s5"], seg["g_s5"],
         seg["qkv"], seg["kv"], seg["gl"], seg["a"], seg["b"], pad, seg["merge"]], axis=-1)
    return packed.astype(BF16)


def _inproj_kernel(l_ref, x_ref, w_ref, o_ref, xb_ref):
    @pl.when(pl.program_id(1) == 0)
    def _():
        xb_ref[...] = x_ref[...].astype(BF16)

    o_ref[...] = _dot(xb_ref[...], w_ref[...])


def _inproj(lidx, x2d, wp):
    n = x2d.shape[0]
    tm = min(n, 1024)
    tn = 512
    return pl.pallas_call(
        _inproj_kernel,
        grid_spec=pltpu.PrefetchScalarGridSpec(
            num_scalar_prefetch=1, grid=(n // tm, NP // tn),
            in_specs=[pl.BlockSpec((tm, D_MODEL), lambda i, j, l: (i, 0)),
                      pl.BlockSpec((None, D_MODEL, tn), lambda i, j, l: (l[0], 0, j))],
            out_specs=pl.BlockSpec((tm, tn), lambda i, j, l: (i, j)),
            scratch_shapes=[pltpu.VMEM((tm, D_MODEL), BF16)]),
        out_shape=jax.ShapeDtypeStruct((n, NP), F32),
        compiler_params=_cparams(("parallel", "arbitrary")),
        name="inproj",
    )(lidx, x2d, wp)


def _lru_kernel(l_ref, u_ref, g_ref, buf_ref, h0_ref, cw_ref, cb_ref, wa_ref, ba_ref, wx_ref, bx_ref, lam_ref,
                y_ref, bufo_ref, ho_ref, xp_scr, a_scr, b_scr, h_scr, *, tt, nt):
    ti = pl.program_id(1)

    @pl.when(ti == 0)
    def _():
        xp_scr[5:8, :] = buf_ref[...]
        h_scr[...] = h0_ref[...]

    u = u_ref[...]
    xp_scr[8:8 + tt, :] = u
    cw = cw_ref[...]
    xc = (cb_ref[...] + cw[3:4] * u + cw[2:3] * xp_scr[7:7 + tt, :]
          + cw[1:2] * xp_scr[6:6 + tt, :] + cw[0:1] * xp_scr[5:5 + tt, :])
    tail = u[tt - 3:tt, :]
    xp_scr[5:8, :] = tail
    sp = _softplus(-lam_ref[...])
    for n in range(LRU_BLOCKS):
        sl = slice(n * LRU_BS, (n + 1) * LRU_BS)
        xn = xc[:, sl]
        xb = xn.astype(BF16)
        r = _sigmoid(_dot(xb, wa_ref[n]) + ba_ref[:, sl])
        i = _sigmoid(_dot(xb, wx_ref[n]) + bx_ref[:, sl])
        a = jnp.exp(-LRU_C * r * sp[:, sl])
        a_scr[:, sl] = a
        b_scr[:, sl] = jnp.sqrt(1.0 - a * a) * (i * xn)

    def body(i, h):
        for k in range(SUBLANES):
            t = i * SUBLANES + k
            h = a_scr[pl.ds(t, 1), :] * h + b_scr[pl.ds(t, 1), :]
            b_scr[pl.ds(t, 1), :] = h
        return h

    h = lax.fori_loop(0, tt // SUBLANES, body, h_scr[...])
    h_scr[...] = h
    y_ref[...] = (b_scr[...] * _silu(g_ref[...])).astype(y_ref.dtype)

    @pl.when(ti == nt - 1)
    def _():
        bufo_ref[...] = tail
        ho_ref[...] = h


def _lru(lidx, z3, buf, h0, p, ydtype):
    s, t, _ = z3.shape
    tt = min(t, 512)
    nt = t // tt
    wspec = lambda shape: pl.BlockSpec((None,) + shape, lambda b, i, l: (l[0],) + (0,) * len(shape))
    return pl.pallas_call(
        functools.partial(_lru_kernel, tt=tt, nt=nt),
        grid_spec=pltpu.PrefetchScalarGridSpec(
            num_scalar_prefetch=1, grid=(s, nt),
            in_specs=[pl.BlockSpec((None, tt, W_BR), lambda b, i, l: (b, i, C_ULRU // W_BR)),
                      pl.BlockSpec((None, tt, W_BR), lambda b, i, l: (b, i, C_GLRU // W_BR)),
                      pl.BlockSpec((None, 3, W_BR), lambda b, i, l: (b, 0, 0)),
                      pl.BlockSpec((None, 1, W_BR), lambda b, i, l: (b, 0, 0)),
                      wspec((CONV_K, W_BR)), wspec((1, W_BR)),
                      wspec((LRU_BLOCKS, LRU_BS, LRU_BS)), wspec((1, W_BR)),
                      wspec((LRU_BLOCKS, LRU_BS, LRU_BS)), wspec((1, W_BR)), wspec((1, W_BR))],
            out_specs=[pl.BlockSpec((None, tt, W_BR), lambda b, i, l: (b, i, 0)),
                       pl.BlockSpec((None, 3, W_BR), lambda b, i, l: (b, 0, 0)),
                       pl.BlockSpec((None, 1, W_BR), lambda b, i, l: (b, 0, 0))],
            scratch_shapes=[pltpu.VMEM((tt + 8, W_BR), F32), pltpu.VMEM((tt, W_BR), F32),
                            pltpu.VMEM((tt, W_BR), F32), pltpu.VMEM((1, W_BR), F32)]),
        out_shape=[jax.ShapeDtypeStruct((s, t, W_BR), ydtype),
                   jax.ShapeDtypeStruct((s, 3, W_BR), F32),
                   jax.ShapeDtypeStruct((s, 1, W_BR), F32)],
        compiler_params=_cparams(("parallel", "arbitrary")),
        name="rglru",
    )(lidx, z3, z3, buf, h0, p["lru_conv_w"], p["lru_conv_b"], p["lru_wa"], p["lru_ba"],
      p["lru_wx"], p["lru_bx"], p["lru_lambda"])


S5_CB = LANES // S5_GS
S5_SW = S5_CB * S5_P


def _gelu_tanh(x):
    return 0.5 * x * (1.0 + jnp.tanh(0.7978845608028654 * (x + 0.044715 * (x * x * x))))


def _s5_kernel(l_ref, u_ref, h0re_ref, h0im_ref, wst_ref, vout_ref, kt_ref, alre_ref, alim_ref, d_ref,
               y_ref, hre_ref, him_ref, hin_scr, s_scr, *, n, sb):
    rows = sb * n
    us = [u_ref[pl.ds(j, rows, stride=S5_L), :] for j in range(S5_L)]
    ub = jnp.concatenate(us, axis=1).astype(BF16)
    s = _dot(ub, wst_ref[...])
    alre = alre_ref[...]
    alim = alim_ref[...]
    h0re = h0re_ref[...]
    h0im = h0im_ref[...]
    if n == 1:
        hin_scr[:, :S5_SW] = h0re
        hin_scr[:, S5_SW:] = h0im
        hre = alre * h0re - alim * h0im + s[:, :S5_SW]
        him = alre * h0im + alim * h0re + s[:, S5_SW:]
    else:
        s_scr[...] = s

        def body(c, carry):
            hre, him = carry
            hin_scr[pl.ds(c, 1), :S5_SW] = hre
            hin_scr[pl.ds(c, 1), S5_SW:] = him
            srow = s_scr[pl.ds(c, 1), :]
            return (alre * hre - alim * him + srow[:, :S5_SW],
                    alre * him + alim * hre + srow[:, S5_SW:])

        hre, him = lax.fori_loop(0, n, body, (h0re, h0im))
    hre_ref[...] = hre
    him_ref[...] = him
    ycat = _dot(hin_scr[...].astype(BF16), vout_ref[...]) + _dot(ub, kt_ref[...])
    d = d_ref[...]
    for j in range(S5_L):
        yj = ycat[:, j * LANES:(j + 1) * LANES] + d * us[j]
        y_ref[pl.ds(j, rows, stride=S5_L), :] = _gelu_tanh(yj)


def _s5_scan(lidx, z3, h0re, h0im, p, sb):
    s, t, _ = z3.shape
    n = t // S5_L
    assert sb == 1 or n == 1
    sg = s // sb
    zr = z3.reshape(sg, sb * t, NP)
    ncb = W_BR // LANES
    wspec = lambda shape: pl.BlockSpec((None, None) + shape, lambda b, c, l: (l[0], c) + (0,) * len(shape))
    hspec = pl.BlockSpec((None, sb, S5_SW), lambda b, c, l: (b, 0, c))
    y, hre, him = pl.pallas_call(
        functools.partial(_s5_kernel, n=n, sb=sb),
        grid_spec=pltpu.PrefetchScalarGridSpec(
            num_scalar_prefetch=1, grid=(sg, ncb),
            in_specs=[pl.BlockSpec((None, sb * t, LANES), lambda b, c, l: (b, 0, C_US5 // LANES + c)),
                      hspec, hspec,
                      wspec((S5_L * LANES, 2 * S5_SW)), wspec((2 * S5_SW, S5_L * LANES)),
                      wspec((S5_L * LANES, S5_L * LANES)), wspec((1, S5_SW)), wspec((1, S5_SW)),
                      wspec((1, LANES))],
            out_specs=[pl.BlockSpec((None, sb * t, LANES), lambda b, c, l: (b, 0, c)), hspec, hspec],
            scratch_shapes=[pltpu.VMEM((sb * n, 2 * S5_SW), F32), pltpu.VMEM((sb * n, 2 * S5_SW), F32)]),
        out_shape=[jax.ShapeDtypeStruct((sg, sb * t, W_BR), F32),
                   jax.ShapeDtypeStruct((sg, sb, S5_G * S5_P), F32),
                   jax.ShapeDtypeStruct((sg, sb, S5_G * S5_P), F32)],
        compiler_params=_cparams(("parallel", "arbitrary")),
        name="s5_scan",
    )(lidx, zr, h0re, h0im, p["s5_wst"], p["s5_vout"], p["s5_kt"], p["s5_alre"], p["s5_alim"], p["s5_d"])
    return y.reshape(s * t, W_BR), hre, him


def _s5_glu_kernel(l_ref, y_ref, g_ref, w_ref, o_ref):
    gl = _dot(y_ref[...].astype(BF16), w_ref[...])
    o_ref[...] = (gl[:, :W_BR] * _sigmoid(gl[:, W_BR:]) * _silu(g_ref[...])).astype(o_ref.dtype)


def _s5_glu(lidx, y2d, z2d, p, ydtype):
    n = y2d.shape[0]
    tm = min(n, 512)
    return pl.pallas_call(
        _s5_glu_kernel,
        grid_spec=pltpu.PrefetchScalarGridSpec(
            num_scalar_prefetch=1, grid=(n // tm,),
            in_specs=[pl.BlockSpec((tm, W_BR), lambda i, l: (i, 0)),
                      pl.BlockSpec((tm, W_BR), lambda i, l: (i, C_GS5 // W_BR)),
                      pl.BlockSpec((None, W_BR, 2 * W_BR), lambda i, l: (l[0], 0, 0))],
            out_specs=pl.BlockSpec((tm, W_BR), lambda i, l: (i, 0))),
        out_shape=jax.ShapeDtypeStruct((n, W_BR), ydtype),
        compiler_params=_cparams(("parallel",)),
        name="s5_glu",
    )(lidx, y2d, z2d, p["s5_glu_w"])


def _s5_weights(w):
    dt = jnp.exp(w["s5_log_dt"])[..., None]
    lr, li = w["s5_lam_re"], w["s5_lam_im"]
    mag = jnp.exp(lr * dt)
    a_re = mag * jnp.cos(li * dt)
    a_im = mag * jnp.sin(li * dt)
    den = lr * lr + li * li
    f_re = ((a_re - 1.0) * lr + a_im * li) / den
    f_im = (a_im * lr - (a_re - 1.0) * li) / den
    bb_re = f_re[..., None] * w["s5_b_re"] - f_im[..., None] * w["s5_b_im"]
    bb_im = f_re[..., None] * w["s5_b_im"] + f_im[..., None] * w["s5_b_re"]
    pw_re = [jnp.ones_like(a_re)]
    pw_im = [jnp.zeros_like(a_im)]
    for _ in range(S5_L):
        pr, pi = pw_re[-1], pw_im[-1]
        pw_re.append(pr * a_re - pi * a_im)
        pw_im.append(pr * a_im + pi * a_re)
    pw_re = jnp.stack(pw_re, axis=1)
    pw_im = jnp.stack(pw_im, axis=1)
    dd = lr.shape[0]
    eye = jnp.eye(S5_CB, dtype=F32)
    grp = lambda a: a.reshape(a.shape[0], a.shape[1], S5_G // S5_CB, S5_CB, *a.shape[3:])
    rev_re = jnp.stack([pw_re[:, S5_L - 1 - j] for j in range(S5_L)], axis=1)
    rev_im = jnp.stack([pw_im[:, S5_L - 1 - j] for j in range(S5_L)], axis=1)
    st_re = rev_re[..., None] * bb_re[:, None] - rev_im[..., None] * bb_im[:, None]
    st_im = rev_re[..., None] * bb_im[:, None] + rev_im[..., None] * bb_re[:, None]

    def state_w(a):
        a = grp(a).transpose(0, 2, 1, 3, 5, 4)
        a = a[:, :, :, :, :, None, :] * eye[:, None, :, None]
        return a.reshape(dd, S5_G // S5_CB, S5_L * LANES, S5_SW)

    wst = jnp.concatenate([state_w(st_re), state_w(st_im)], axis=-1).astype(BF16)
    c_re, c_im = w["s5_c_re"], w["s5_c_im"]
    nx_re, nx_im = pw_re[:, 1:], pw_im[:, 1:]
    ca_re = c_re[:, None] * nx_re[:, :, :, None] - c_im[:, None] * nx_im[:, :, :, None]
    ca_im = c_re[:, None] * nx_im[:, :, :, None] + c_im[:, None] * nx_re[:, :, :, None]

    def out_w(a):
        a = grp(a).transpose(0, 2, 3, 5, 1, 4)
        a = a[:, :, :, :, :, None, :] * eye[:, None, None, :, None]
        return a.reshape(dd, S5_G // S5_CB, S5_SW, S5_L * LANES)

    vout = jnp.concatenate([out_w(ca_re), out_w(-ca_im)], axis=2).astype(BF16)
    cat_re = c_re[:, None] * pw_re[:, :S5_L, :, None] - c_im[:, None] * pw_im[:, :S5_L, :, None]
    cat_im = c_re[:, None] * pw_im[:, :S5_L, :, None] + c_im[:, None] * pw_re[:, :S5_L, :, None]
    kk = (jnp.einsum("dlgcp,dgpe->dlgce", cat_re, bb_re, precision=HI)
          - jnp.einsum("dlgcp,dgpe->dlgce", cat_im, bb_im, precision=HI))
    zero = jnp.zeros_like(kk[:, 0])
    kt = jnp.stack([jnp.stack([kk[:, j - i] if j >= i else zero for j in range(S5_L)], axis=1)
                    for i in range(S5_L)], axis=1)
    kt = kt.reshape(dd, S5_L, S5_L, S5_G // S5_CB, S5_CB, S5_GS, S5_GS)
    kt = kt.transpose(0, 3, 1, 4, 6, 2, 5)
    kt = kt[:, :, :, :, :, :, None, :] * eye[:, None, None, :, None]
    kt = kt.reshape(dd, S5_G // S5_CB, S5_L * LANES, S5_L * LANES).astype(BF16)
    sw = lambda a: a.reshape(dd, S5_G // S5_CB, 1, S5_SW)
    return dict(s5_wst=wst, s5_vout=vout, s5_kt=kt, s5_alre=sw(pw_re[:, S5_L]), s5_alim=sw(pw_im[:, S5_L]),
                s5_d=w["s5_d"].reshape(dd, W_BR // LANES, 1, LANES), s5_glu_w=w["s5_glu_w"].astype(BF16))


def _unit_lower_inverse(m, c):
    x = jnp.where(_iota((c, c), 0) == _iota((c, c), 1), 1.0, 0.0).astype(F32)
    for j in range(c - 1):
        x = x - m[:, j:j + 1] * x[j:j + 1, :]
    return x


def _gdn_kernel(l_ref, q_ref, k_ref, v_ref, sm_ref, gg_ref, bq_ref, bk_ref, bv_ref, cwq_ref, cwk_ref, cwv_ref,
                alog_ref, dtb_ref, nw_ref, s0_ref, y_ref, bufo_ref, so_ref,
                xp_scr, qkv_scr, gx_scr, bx_scr, gc_scr, s_scr, *, tt, nt, c):
    ti = pl.program_id(1)
    log2c = c.bit_length() - 1

    @pl.when(ti == 0)
    def _():
        for i, b_ref in enumerate((bq_ref, bk_ref, bv_ref)):
            xp_scr[i, 5:8, :] = b_ref[...]
        s_scr[...] = s0_ref[...]

    for i, (x_ref, cw_ref) in enumerate(((q_ref, cwq_ref), (k_ref, cwk_ref), (v_ref, cwv_ref))):
        x = x_ref[...]
        xp_scr[i, 8:8 + tt, :] = x
        cw = cw_ref[...]
        cv = (cw[3:4] * x + cw[2:3] * xp_scr[i, 7:7 + tt, :]
              + cw[1:2] * xp_scr[i, 6:6 + tt, :] + cw[0:1] * xp_scr[i, 5:5 + tt, :])
        tail = x[tt - 3:tt, :]
        xp_scr[i, 5:8, :] = tail
        bufo_ref[:, i * W_BR:(i + 1) * W_BR] = tail
        cv = _silu(cv)
        if i < 2:
            scale = GDN_HD ** -0.5 if i == 0 else 1.0
            for h in range(GDN_H):
                sl = slice(h * GDN_HD, (h + 1) * GDN_HD)
                xh = cv[:, sl]
                qkv_scr[i, :, sl] = xh * (lax.rsqrt(jnp.sum(xh * xh, axis=-1, keepdims=True) + 1e-6) * scale)
        else:
            qkv_scr[i] = cv

    sm = sm_ref[...]
    gsm = -jnp.exp(alog_ref[...]) * _softplus(sm + dtb_ref[...])
    bsm = _sigmoid(sm)
    src = _iota((LANES, W_BR), 0)
    head = _iota((LANES, W_BR), 1) >> 7
    gx_scr[...] = _dot_hi(gsm, jnp.where(src - SM_A == head, 1.0, 0.0))
    bx_scr[...] = _dot_hi(bsm, jnp.where(src - SM_B == head, 1.0, 0.0))
    headc = _iota((LANES, GDN_H * c), 1) >> log2c
    gc_scr[...] = _dot_hi(gsm, jnp.where(_iota((LANES, GDN_H * c), 0) - SM_A == headc, 1.0, 0.0))

    rowi = _iota((c, c), 0)
    coli = _iota((c, c), 1)
    incl = coli <= rowi
    strict = coli < rowi
    ltri = jnp.where(incl, 1.0, 0.0)
    upper = jnp.where(_iota((c, GDN_H * c), 0) > (_iota((c, GDN_H * c), 1) & (c - 1)), 1.0, 0.0)
    nw = nw_ref[...]

    def chunk(ci, carry):
        r0 = pl.multiple_of(ci * c, c)
        gcb_all = _dot_hi(ltri, gx_scr[pl.ds(r0, c), :])
        diffs = _dot_hi(ltri, gc_scr[pl.ds(r0, c), :] * upper)
        beta_all = bx_scr[pl.ds(r0, c), :]
        for h in range(GDN_H):
            sl = slice(h * GDN_HD, (h + 1) * GDN_HD)
            q = qkv_scr[0, pl.ds(r0, c), sl]
            k = qkv_scr[1, pl.ds(r0, c), sl]
            v = qkv_scr[2, pl.ds(r0, c), sl]
            beta = beta_all[:, sl]
            gcb = gcb_all[:, sl]
            decay = jnp.where(incl, jnp.exp(diffs[:, h * c:(h + 1) * c]), 0.0)
            kb = k * beta
            kbf = k.astype(BF16)
            m = jnp.where(strict, _dot_nt(kb.astype(BF16), kbf) * decay, 0.0)
            tinv = _unit_lower_inverse(m, c)
            rhs = jnp.concatenate([v * beta, kb * jnp.exp(gcb)], axis=1).astype(BF16)
            sol = _dot(tinv.astype(BF16), rhs)
            u = sol[:, :GDN_HD]
            w = sol[:, GDN_HD:]
            qk = _dot_nt(q.astype(BF16), kbf) * decay
            s = s_scr[h]
            sb = s.astype(BF16)
            vnew = u - _dot(w.astype(BF16), sb)
            vnb = vnew.astype(BF16)
            o = _dot((q * jnp.exp(gcb)).astype(BF16), sb) + _dot(qk.astype(BF16), vnb)
            glast = gcb[c - 1:c, :]
            s_scr[h] = s * jnp.exp(glast) + _dot_tn((k * jnp.exp(glast - gcb)).astype(BF16), vnb)
            o = o * lax.rsqrt(jnp.mean(o * o, axis=-1, keepdims=True) + 1e-6) * nw
            y_ref[pl.ds(r0, c), sl] = (o * _silu(gg_ref[pl.ds(r0, c), sl])).astype(y_ref.dtype)
        return carry

    lax.fori_loop(0, tt // c, chunk, 0)

    @pl.when(ti == nt - 1)
    def _():
        so_ref[...] = s_scr[...]


def _gdn(lidx, z3, buf, s0, p, ydtype, chunk):
    s, t, _ = z3.shape
    tt = min(t, 256)
    nt = t // tt
    c = min(chunk, tt)
    zspec = lambda col: pl.BlockSpec((None, tt, W_BR), lambda b, i, l: (b, i, col // W_BR))
    bspec = lambda j: pl.BlockSpec((None, 3, W_BR), lambda b, i, l: (b, 0, j))
    cspec = lambda j: pl.BlockSpec((None, CONV_K, W_BR), lambda b, i, l: (l[0], 0, j))
    rspec = pl.BlockSpec((None, 1, LANES), lambda b, i, l: (l[0], 0, 0))
    sspec = pl.BlockSpec((None, GDN_H, GDN_HD, GDN_HD), lambda b, i, l: (b, 0, 0, 0))
    return pl.pallas_call(
        functools.partial(_gdn_kernel, tt=tt, nt=nt, c=c),
        grid_spec=pltpu.PrefetchScalarGridSpec(
            num_scalar_prefetch=1, grid=(s, nt),
            in_specs=[zspec(C_QKV), zspec(C_QKV + W_BR), zspec(C_QKV + 2 * W_BR),
                      pl.BlockSpec((None, tt, LANES), lambda b, i, l: (b, i, C_SMALL // LANES)),
                      zspec(C_GGDN), bspec(0), bspec(1), bspec(2), cspec(0), cspec(1), cspec(2),
                      rspec, rspec, rspec, sspec],
            out_specs=[pl.BlockSpec((None, tt, W_BR), lambda b, i, l: (b, i, 0)),
                       pl.BlockSpec((None, 3, 3 * W_BR), lambda b, i, l: (b, 0, 0)),
                       sspec],
            scratch_shapes=[pltpu.VMEM((3, tt + 8, W_BR), F32), pltpu.VMEM((3, tt, W_BR), F32),
                            pltpu.VMEM((tt, W_BR), F32), pltpu.VMEM((tt, W_BR), F32),
                            pltpu.VMEM((tt, GDN_H * c), F32), pltpu.VMEM((GDN_H, GDN_HD, GDN_HD), F32)]),
        out_shape=[jax.ShapeDtypeStruct((s, t, W_BR), ydtype),
                   jax.ShapeDtypeStruct((s, 3, 3 * W_BR), F32),
                   jax.ShapeDtypeStruct((s, GDN_H, GDN_HD, GDN_HD), F32)],
        compiler_params=_cparams(("parallel", "arbitrary")),
        name="gdn",
    )(lidx, z3, z3, z3, z3, z3, buf, buf, buf, p["gdn_conv_w"], p["gdn_conv_w"], p["gdn_conv_w"],
      p["gdn_a_log"], p["gdn_dt_bias"], p["gdn_norm_w"], s0)


PAGES_PER_STEP = 4


def _gather_kernel(l_ref, pt_ref, p0_ref, p1_ref, p2_ref, p3_ref, zn_ref, cmp_ref, sel_ref, *, n_past):
    i = pl.program_id(1)
    half = 2 * NSA_KVH * NSA_HD

    @pl.when(i < n_past)
    def _():
        for k, p_ref in enumerate((p0_ref, p1_ref, p2_ref, p3_ref)):
            x = p_ref[...]
            cmp_ref[k * PAGE_SIZE:(k + 1) * PAGE_SIZE, :] = x[:, :half]
            sel_ref[k * PAGE_SIZE:(k + 1) * PAGE_SIZE, :] = x[:, half:].astype(BF16)

    @pl.when(i == n_past)
    def _():
        sel_ref[...] = jnp.zeros(sel_ref.shape, BF16)
        new = zn_ref[:, half:2 * half]
        sel_ref[0:2 * SUBLANES, :] = jnp.concatenate([new, jnp.zeros_like(new)], axis=0).astype(BF16)


def _gather_ctx(lidx, pt_flat, cache3, z3, n_pool):
    b, t_new, _ = z3.shape
    assert t_new == SUBLANES
    n_pages = pt_flat.shape[0] // b
    n_past = n_pages // PAGES_PER_STEP
    rows = PAGES_PER_STEP * PAGE_SIZE
    width = 4 * NSA_KVH * NSA_HD
    half = width // 2

    def pspec(k):
        def imap(bi, i, l, pt):
            page = jnp.minimum(i * PAGES_PER_STEP + k, n_pages - 1)
            return (l[0] * n_pool + pt[bi * n_pages + page], 0, 0)
        return pl.BlockSpec((None, PAGE_SIZE, width), imap)

    return pl.pallas_call(
        functools.partial(_gather_kernel, n_past=n_past),
        grid_spec=pltpu.PrefetchScalarGridSpec(
            num_scalar_prefetch=2, grid=(b, n_past + 1),
            in_specs=[pspec(k) for k in range(PAGES_PER_STEP)]
            + [pl.BlockSpec((None, t_new, width), lambda bi, i, l, pt: (bi, 0, C_KV // width))],
            out_specs=[pl.BlockSpec((None, rows, half), lambda bi, i, l, pt: (bi, jnp.minimum(i, n_past - 1), 0)),
                       pl.BlockSpec((None, rows, half), lambda bi, i, l, pt: (bi, i, 0))]),
        out_shape=[jax.ShapeDtypeStruct((b, n_past * rows, half), F32),
                   jax.ShapeDtypeStruct((b, (n_past + 1) * rows, half), BF16)],
        compiler_params=_cparams(("parallel", "arbitrary")),
        name="nsa_gather",
    )(lidx, pt_flat, cache3, cache3, cache3, cache3, z3)


def _cmp_kernel(l_ref, x0_ref, x1_ref, w1_ref, pe_ref, w2_ref, o_ref, carry_scr, *, nh):
    @pl.when(pl.program_id(2) == 0)
    def _():
        carry_scr[...] = jnp.zeros(carry_scr.shape, F32)

    w1 = w1_ref[...]
    bias = _dot_hi(pe_ref[...], w1.astype(F32))
    last = _iota((nh, NSA_HD), 0) == nh - 1
    for g, x_ref in enumerate((x0_ref, x1_ref)):
        ucat = jnp.concatenate([x_ref[pl.ds(s, nh, stride=CMP_STRIDE), :] for s in range(CMP_STRIDE)], axis=1)
        hh = _dot(ucat.astype(BF16), w1)
        hf = hh[:, :NSA_HD] + bias[0:1, :NSA_HD]
        hs = hh[:, NSA_HD:] + bias[1:2, NSA_HD:]
        hs_next = jnp.where(last, carry_scr[g], pltpu.roll(hs, nh - 1, axis=0))
        carry_scr[g] = hs[0:1, :]
        hid = _silu(hf + hs_next)
        o_ref[g] = _dot(hid.astype(BF16), w2_ref[...]).astype(BF16)


def _compress(lidx, src3, col0, n_rows, p):
    b = src3.shape[0]
    tr = min(n_rows, 4096)
    nt = n_rows // tr
    nh = tr // CMP_STRIDE
    xspec = lambda g: pl.BlockSpec((None, tr, NSA_HD), lambda bi, c, i, l: (bi, nt - 1 - i, col0 + NSA_KVH * c + g))
    return pl.pallas_call(
        functools.partial(_cmp_kernel, nh=nh),
        grid_spec=pltpu.PrefetchScalarGridSpec(
            num_scalar_prefetch=1, grid=(b, 2, nt),
            in_specs=[xspec(0), xspec(1),
                      pl.BlockSpec((None, None, CMP_STRIDE * NSA_HD, 2 * NSA_HD), lambda bi, c, i, l: (l[0], c, 0, 0)),
                      pl.BlockSpec((None, SUBLANES, CMP_STRIDE * NSA_HD), lambda bi, c, i, l: (l[0], 0, 0)),
                      pl.BlockSpec((None, None, NSA_HD, NSA_HD), lambda bi, c, i, l: (l[0], c, 0, 0))],
            out_specs=pl.BlockSpec((None, None, NSA_KVH, nh, NSA_HD), lambda bi, c, i, l: (bi, c, 0, nt - 1 - i, 0)),
            scratch_shapes=[pltpu.VMEM((NSA_KVH, 1, NSA_HD), F32)]),
        out_shape=jax.ShapeDtypeStruct((b, 2, NSA_KVH, n_rows // CMP_STRIDE, NSA_HD), BF16),
        compiler_params=_cparams(("parallel", "parallel", "arbitrary")),
        name="nsa_compress",
    )(lidx, src3, src3, p["nsa_w1"], p["nsa_pe"], p["nsa_w2"])


def _masked_softmax(s, valid):
    s = jnp.where(valid, s, NEG)
    m = jnp.max(s, axis=-1, keepdims=True)
    e = jnp.where(valid, jnp.exp(s - m), 0.0)
    den = jnp.sum(e, axis=-1, keepdims=True)
    return e / jnp.where(den > 0.0, den, 1.0)


def _attn_kernel(l_ref, q_ref, sm_ref, gn_ref, kc_ref, vc_ref, ks_ref, vs_ref, kw_ref, vw_ref, y_ref, *,
                 qb, pos0, ncp, ns, nsp, kt, wn, wpos0, tw):
    g = pl.program_id(1)
    q0 = pl.program_id(2) * qb
    qpos0 = pos0 + q0
    hd = NSA_HD
    q = q_ref[...] * (hd ** -0.5)
    qr = jnp.concatenate([q[:, j * hd:(j + 1) * hd] for j in range(NSA_GQ)], axis=0).astype(BF16)
    slope_g = jnp.where(g == 0, 1.0, 2.0 ** -NSA_GQ)
    slopes = [slope_g * 2.0 ** -(j + 1) for j in range(NSA_GQ)]
    heads = lambda a: [a[j * qb:(j + 1) * qb] for j in range(NSA_GQ)]

    dist = (qpos0 + _iota((qb, ncp), 0)) - (_iota((qb, ncp), 1) * CMP_STRIDE + (CMP_BLOCK - 1))
    valid = dist >= 0
    distf = dist.astype(F32)
    sc = heads(_dot_nt(qr, kc_ref[...]))
    pc = [_masked_softmax(sc[j] - slopes[j] * distf, valid) for j in range(NSA_GQ)]
    o_c = heads(_dot(jnp.concatenate(pc, axis=0).astype(BF16), vc_ref[...]))
    imp = pc[0] + pc[1] + pc[2] + pc[3]
    pool = jnp.where((_iota((ncp, nsp), 0) >> 2) == _iota((ncp, nsp), 1), 1.0, 0.0)
    imp = _dot_hi(imp, pool)

    blk = _iota((qb, nsp), 1)
    qp = qpos0 + _iota((qb, nsp), 0)
    val = jnp.where(blk * SEL_BLOCK > qp, -FORCE, imp)
    val = jnp.where(blk == (qp >> 6), FORCE, jnp.where(blk == 0, FORCE, val))
    val = jnp.where(blk >= ns, -3.0 * FORCE, val)
    rank = jnp.zeros((qb, nsp), F32)
    for bidx in range(ns):
        col = val[:, bidx:bidx + 1]
        rank = rank + jnp.where(col > val, 1.0, jnp.where(col == val, jnp.where(blk > bidx, 1.0, 0.0), 0.0))
    selb = jnp.where(rank < float(min(TOP_N, ns)), 1.0, 0.0).astype(BF16)

    n_tiles = (qpos0 + qb + kt - 1) // kt
    rows = NSA_GQ * qb

    def tile(t, carry):
        m, lsum, acc = carry
        k0 = pl.multiple_of(t * kt, kt)
        s = heads(_dot_nt(qr, ks_ref[pl.ds(k0, kt), :].astype(BF16)))
        d = (qpos0 + _iota((qb, kt), 0)) - (k0 + _iota((qb, kt), 1))
        expand = jnp.where(_iota((nsp, kt), 0) == ((k0 + _iota((nsp, kt), 1)) >> 6), 1.0, 0.0).astype(BF16)
        ok = jnp.where(d >= 0, _dot(selb, expand), 0.0) > 0.5
        df = d.astype(F32)
        sm = jnp.concatenate([jnp.where(ok, s[j] - slopes[j] * df, NEG) for j in range(NSA_GQ)], axis=0)
        okf = jnp.where(ok, 1.0, 0.0)
        okf = jnp.concatenate([okf] * NSA_GQ, axis=0)
        m_new = jnp.maximum(m, jnp.max(sm, axis=-1, keepdims=True))
        pr = jnp.exp(sm - m_new) * okf
        alpha = jnp.exp(m - m_new)
        lsum = alpha * lsum + jnp.sum(pr, axis=-1, keepdims=True)
        acc = alpha * acc + _dot(pr.astype(BF16), vs_ref[pl.ds(k0, kt), :].astype(BF16))
        return m_new, lsum, acc

    _, lsum, acc = lax.fori_loop(0, n_tiles, tile, (jnp.full((rows, 1), NEG, F32), jnp.zeros((rows, 1), F32),
                                                   jnp.zeros((rows, hd), F32)))
    o_s = heads(acc / lsum)

    k0w = pl.multiple_of(jnp.clip(q0 - WINDOW, 0, tw - wn), SUBLANES)
    dw = (qpos0 + _iota((qb, wn), 0)) - (wpos0 + k0w + _iota((qb, wn), 1))
    okw = jnp.abs(2 * dw - (WINDOW - 1)) < WINDOW
    dwf = dw.astype(F32)
    sw = heads(_dot_nt(qr, kw_ref[pl.ds(k0w, wn), :].astype(BF16)))
    pw = [_masked_softmax(sw[j] - slopes[j] * dwf, okw) for j in range(NSA_GQ)]
    o_w = heads(_dot(jnp.concatenate(pw, axis=0).astype(BF16), vw_ref[pl.ds(k0w, wn), :].astype(BF16)))

    gate = _sigmoid(sm_ref[...])
    lane = _iota((qb, LANES), 1)
    pick = lambda idx: jnp.sum(jnp.where(lane == idx, gate, 0.0), axis=-1, keepdims=True)
    for j in range(NSA_GQ):
        base = SM_GL + 3 * (g * NSA_GQ + j)
        o = pick(base) * o_c[j] + pick(base + 1) * o_s[j] + pick(base + 2) * o_w[j]
        y_ref[:, j * hd:(j + 1) * hd] = (o * _silu(gn_ref[:, j * hd:(j + 1) * hd])).astype(y_ref.dtype)


def _attention(lidx, z3, cmp_kv, ks_src, ks_col, vs_col, kw_src, kw_col, vw_col, pos0, wpos0, ydtype):
    b, t, _ = z3.shape
    qb = min(Q_BLOCK, t)
    ncp = cmp_kv.shape[3]
    tk = ks_src.shape[1]
    tw = kw_src.shape[1]
    kt = 512
    ns = -(-(pos0 + t) // SEL_BLOCK)
    nsp = -(-ns // LANES) * LANES
    wn = WINDOW + Q_BLOCK
    assert tk % kt == 0 and tk >= pos0 + t and tw >= wn and ncp // CMP_PER_SEL <= nsp
    hw = NSA_GQ * NSA_HD
    kern = functools.partial(_attn_kernel, qb=qb, pos0=pos0, ncp=ncp, ns=ns, nsp=nsp, kt=kt, wn=wn, wpos0=wpos0, tw=tw)
    kvspec = lambda rows, col: pl.BlockSpec((None, rows, NSA_HD), lambda bi, g, i, l: (bi, 0, col + g))
    cspec = lambda c: pl.BlockSpec((None, None, None, ncp, NSA_HD), lambda bi, g, i, l: (bi, c, g, 0, 0))
    return pl.pallas_call(
        kern,
        grid_spec=pltpu.PrefetchScalarGridSpec(
            num_scalar_prefetch=1, grid=(b, NSA_KVH, t // qb),
            in_specs=[pl.BlockSpec((None, qb, hw), lambda bi, g, i, l: (bi, i, C_QNSA // hw + g)),
                      pl.BlockSpec((None, qb, LANES), lambda bi, g, i, l: (bi, i, C_SMALL // LANES)),
                      pl.BlockSpec((None, qb, hw), lambda bi, g, i, l: (bi, i, C_GNSA // hw + g)),
                      cspec(0), cspec(1),
                      kvspec(tk, ks_col), kvspec(tk, vs_col), kvspec(tw, kw_col), kvspec(tw, vw_col)],
            out_specs=pl.BlockSpec((None, qb, hw), lambda bi, g, i, l: (bi, i, g))),
        out_shape=jax.ShapeDtypeStruct((b, t, W_BR), ydtype),
        compiler_params=_cparams(("parallel", "parallel", "arbitrary")),
        name="nsa_attention",
    )(lidx, z3, z3, z3, cmp_kv, cmp_kv, ks_src, ks_src, kw_src, kw_src)


def _merge_kernel(l_ref, y0_ref, y1_ref, y2_ref, y3_ref, g0_ref, g1_ref, g2_ref, g3_ref, w_ref, o_ref):
    acc = None
    for m, (y_ref, g_ref) in enumerate(((y0_ref, g0_ref), (y1_ref, g1_ref), (y2_ref, g2_ref), (y3_ref, g3_ref))):
        term = _sigmoid(g_ref[...]) * _dot(y_ref[...].astype(BF16), w_ref[m])
        acc = term if acc is None else acc + term
    o_ref[...] = acc.astype(o_ref.dtype)


def _merge(lidx, ys, z2d, p):
    n = z2d.shape[0]
    tm = min(n, 512)
    tn = 512
    yspec = pl.BlockSpec((tm, W_BR), lambda i, j, l: (i, 0))
    gspec = lambda m: pl.BlockSpec((tm, tn), lambda i, j, l: (i, (C_MERGE + m * D_MODEL) // tn + j))
    return pl.pallas_call(
        _merge_kernel,
        grid_spec=pltpu.PrefetchScalarGridSpec(
            num_scalar_prefetch=1, grid=(n // tm, D_MODEL // tn),
            in_specs=[yspec] * N_BRANCH + [gspec(m) for m in range(N_BRANCH)]
            + [pl.BlockSpec((None, N_BRANCH, W_BR, tn), lambda i, j, l: (l[0], 0, 0, j))],
            out_specs=pl.BlockSpec((tm, tn), lambda i, j, l: (i, j))),
        out_shape=jax.ShapeDtypeStruct((n, D_MODEL), BF16),
        compiler_params=_cparams(("parallel", "arbitrary")),
        name="merge",
    )(lidx, *ys, z2d, z2d, z2d, z2d, p["w_branch"])


def _outproj_kernel(l_ref, m_ref, x_ref, w_ref, g_ref, b_ref, o_ref, *, alpha):
    v = alpha * x_ref[...] + _dot(m_ref[...], w_ref[...])
    mu = jnp.mean(v, axis=-1, keepdims=True)
    c = v - mu
    var = jnp.mean(c * c, axis=-1, keepdims=True)
    o_ref[...] = c * lax.rsqrt(var + 1e-5) * g_ref[...] + b_ref[...]


def _outproj(lidx, merged, x2d, p, alpha):
    n = x2d.shape[0]
    tm = min(n, 256)
    return pl.pallas_call(
        functools.partial(_outproj_kernel, alpha=alpha),
        grid_spec=pltpu.PrefetchScalarGridSpec(
            num_scalar_prefetch=1, grid=(n // tm,),
            in_specs=[pl.BlockSpec((tm, D_MODEL), lambda i, l: (i, 0)),
                      pl.BlockSpec((tm, D_MODEL), lambda i, l: (i, 0)),
                      pl.BlockSpec((None, D_MODEL, D_MODEL), lambda i, l: (l[0], 0, 0)),
                      pl.BlockSpec((None, 1, D_MODEL), lambda i, l: (l[0], 0, 0)),
                      pl.BlockSpec((None, 1, D_MODEL), lambda i, l: (l[0], 0, 0))],
            out_specs=pl.BlockSpec((tm, D_MODEL), lambda i, l: (i, 0))),
        out_shape=jax.ShapeDtypeStruct((n, D_MODEL), F32),
        compiler_params=_cparams(("parallel",)),
        name="outproj_ln",
    )(lidx, merged, x2d, p["w_out"], p["ln_g"], p["ln_b"])


def _prep_params(w):
    p = {}
    row = lambda a: a[:, None, :]
    p["lru_conv_w"] = w["lru_conv_w"]
    p["lru_conv_b"] = row(w["lru_conv_b"])
    p["lru_wa"] = w["lru_wa"].astype(BF16)
    p["lru_ba"] = row(w["lru_ba"])
    p["lru_wx"] = w["lru_wx"].astype(BF16)
    p["lru_bx"] = row(w["lru_bx"])
    p["lru_lambda"] = row(w["lru_lambda"])
    p.update(_s5_weights(w))
    p["w_branch"] = w["w_branch"].astype(BF16)
    p["w_out"] = w["w_out"].astype(BF16)
    p["ln_g"] = row(w["ln_g"])
    p["ln_b"] = row(w["ln_b"])
    dd = w["gdn_a_log"].shape[0]
    lane_row = lambda a, off: jnp.zeros((dd, 1, LANES), F32).at[:, 0, off:off + a.shape[-1]].set(a)
    p["gdn_conv_w"] = w["gdn_conv_w"]
    p["gdn_a_log"] = lane_row(w["gdn_a_log"], SM_A)
    p["gdn_dt_bias"] = lane_row(w["gdn_dt_bias"], SM_A)
    p["gdn_norm_w"] = row(w["gdn_norm_w"])
    flat = CMP_STRIDE * NSA_HD
    w1 = w["nsa_cmp_w1"].reshape(dd, 2, 2, flat, NSA_HD).transpose(0, 1, 3, 2, 4)
    p["nsa_w1"] = w1.reshape(dd, 2, flat, 2 * NSA_HD).astype(BF16)
    pe = w["nsa_cmp_pos"].reshape(dd, 2, flat)
    p["nsa_pe"] = jnp.concatenate([pe, jnp.zeros((dd, SUBLANES - 2, flat), F32)], axis=1)
    p["nsa_w2"] = w["nsa_cmp_w2"].astype(BF16)
    return p


def _mixer_layer(lidx, x2d, s, t, state, nsa_branch, p, wp, alpha, ydtype):
    lru_buf, lru_h, gdn_buf, gdn_s, s5_re, s5_im = state
    z = _inproj(lidx, x2d, wp)
    z3 = z.reshape(s, t, NP)
    y_lru, lru_buf, lru_h = _lru(lidx, z3, lru_buf, lru_h, p, ydtype)
    y_nsa = nsa_branch(z3)
    y_gdn, gdn_buf, gdn_s = _gdn(lidx, z3, gdn_buf, gdn_s, p, ydtype, GDN_CHUNK)
    y_s5, s5_re, s5_im = _s5_scan(lidx, z3, s5_re, s5_im, p, s5_re.shape[1])
    y_s5 = _s5_glu(lidx, y_s5, z, p, ydtype)
    flat = lambda y: y.reshape(s * t, W_BR)
    merged = _merge(lidx, (flat(y_lru), flat(y_nsa), flat(y_gdn), y_s5), z, p)
    x_new = _outproj(lidx, merged, x2d, p, alpha)
    return x_new, z3, (lru_buf, lru_h, gdn_buf, gdn_s, s5_re, s5_im)


def kernel(x_prompt, x_sample, state_lru_h, state_lru_conv, cache_nsa_kv, cache_win_kv, state_gdn_s, state_gdn_conv, state_s5_re, state_s5_im, page_table, w_in, lru_conv_w, lru_conv_b, lru_wa, lru_ba, lru_wx, lru_bx, lru_lambda, nsa_cmp_pos, nsa_cmp_w1, nsa_cmp_w2, gdn_conv_w, gdn_a_log, gdn_dt_bias, gdn_norm_w, s5_lam_re, s5_lam_im, s5_log_dt, s5_b_re, s5_b_im, s5_c_re, s5_c_im, s5_d, s5_glu_w, w_branch, w_out, ln_g, ln_b):
    depth = w_in.shape[0]
    bp, tp, _ = x_prompt.shape
    db, ts, _ = x_sample.shape
    n_pages = page_table.shape[1]
    past = n_pages * PAGE_SIZE
    n_pool = cache_nsa_kv.shape[1]
    wbuf = cache_win_kv.shape[2]
    alpha = (2.0 * depth) ** 0.25
    kvw = 4 * NSA_KVH * NSA_HD
    winw = 2 * NSA_KVH * NSA_HD
    p = _prep_params(dict(
        lru_conv_w=lru_conv_w, lru_conv_b=lru_conv_b, lru_wa=lru_wa, lru_ba=lru_ba, lru_wx=lru_wx, lru_bx=lru_bx,
        lru_lambda=lru_lambda, nsa_cmp_pos=nsa_cmp_pos, nsa_cmp_w1=nsa_cmp_w1, nsa_cmp_w2=nsa_cmp_w2,
        gdn_conv_w=gdn_conv_w, gdn_a_log=gdn_a_log, gdn_dt_bias=gdn_dt_bias, gdn_norm_w=gdn_norm_w,
        s5_lam_re=s5_lam_re, s5_lam_im=s5_lam_im, s5_log_dt=s5_log_dt, s5_b_re=s5_b_re, s5_b_im=s5_b_im,
        s5_c_re=s5_c_re, s5_c_im=s5_c_im, s5_d=s5_d, s5_glu_w=s5_glu_w, w_branch=w_branch, w_out=w_out,
        ln_g=ln_g, ln_b=ln_b))
    wp = _pack_w_in(w_in)
    cache3 = cache_nsa_kv.reshape(depth * n_pool, PAGE_SIZE, kvw)
    pt_flat = page_table.reshape(-1).astype(jnp.int32)
    sb = db if ts == S5_L else 1
    c0 = C_KV // LANES
    zero_state = (jnp.zeros((bp, CONV_K - 1, W_BR), F32), jnp.zeros((bp, 1, W_BR), F32),
                  jnp.zeros((bp, CONV_K - 1, 3 * W_BR), F32), jnp.zeros((bp, GDN_H, GDN_HD, GDN_HD), F32),
                  jnp.zeros((bp, 1, S5_G * S5_P), F32), jnp.zeros((bp, 1, S5_G * S5_P), F32))

    def layer(carry, l):
        xp, xs = carry
        lidx = l.reshape(1)
        at = lambda a: lax.dynamic_index_in_dim(a, l, 0, keepdims=False)

        def nsa_prompt(z3):
            ckv = _compress(lidx, z3, c0, tp, p)
            return _attention(lidx, z3, ckv, z3, c0 + 4, c0 + 6, z3, c0 + 8, c0 + 10, 0, 0, BF16)

        xp, zp3, st_p = _mixer_layer(lidx, xp, bp, tp, zero_state, nsa_prompt, p, wp, alpha, BF16)

        win_state = {}

        def nsa_sample(z3):
            cmp_rows, sel_rows = _gather_ctx(lidx, pt_flat, cache3, z3, n_pool)
            ckv = _compress(lidx, cmp_rows, 0, past, p)
            win = jnp.concatenate([at(cache_win_kv).reshape(db, wbuf, winw),
                                   z3[:, :, C_KV + kvw:C_KV + kvw + winw]], axis=1)
            win_state["win"] = win
            pad = max(0, WINDOW + Q_BLOCK - (wbuf + ts))
            win_pad = jnp.concatenate([win, jnp.zeros((db, pad, winw), F32)], axis=1)
            return _attention(lidx, z3, ckv, sel_rows, 0, 2, win_pad, 0, 2, past, past - wbuf, F32)

        st_in = (at(state_lru_conv), at(state_lru_h)[:, None, :], at(state_gdn_conv), at(state_gdn_s),
                 at(state_s5_re).reshape(db // sb, sb, S5_G * S5_P), at(state_s5_im).reshape(db // sb, sb, S5_G * S5_P))
        xs, zs3, st_s = _mixer_layer(lidx, xs, db, ts, st_in, nsa_sample, p, wp, alpha, F32)

        def outs(st, z3, s, t, win):
            lru_buf, lru_h, gdn_buf, gdn_s, s5_re, s5_im = st
            return (lru_h.reshape(s, W_BR), lru_buf,
                    z3[:, :, C_KV:C_KV + kvw].reshape(s, t, 4, NSA_KVH, NSA_HD),
                    win.reshape(s, win.shape[1], 2, NSA_KVH, NSA_HD),
                    gdn_s, gdn_buf, s5_re.reshape(s, S5_G, S5_P), s5_im.reshape(s, S5_G, S5_P))

        win_p = zp3[:, tp - min(WINDOW, tp):, C_KV + kvw:C_KV + kvw + winw]
        win_s = win_state["win"][:, wbuf + ts - min(WINDOW, past + ts):]
        return (xp, xs), (outs(st_p, zp3, bp, tp, win_p), outs(st_s, zs3, db, ts, win_s))

    (xp, xs), (op, os_) = lax.scan(layer, (x_prompt.reshape(bp * tp, D_MODEL), x_sample.reshape(db * ts, D_MODEL)),
                                   jnp.arange(depth, dtype=jnp.int32))
    return (xp.reshape(bp, tp, D_MODEL), xs.reshape(db, ts, D_MODEL)) + tuple(op) + tuple(os_)
```

```python
import functools

import jax
import jax.numpy as jnp
from jax import lax
from jax.experimental import pallas as pl
from jax.experimental.pallas import tpu as pltpu

F32 = jnp.float32
BF16 = jnp.bfloat16
HI = lax.Precision.HIGHEST

D_MODEL = 2048
W_BR = D_MODEL // 2
N_BRANCH = 4
CONV_K = 4
LRU_BLOCKS = 8
LRU_BS = W_BR // LRU_BLOCKS
LRU_C = 8.0
NSA_HD = 128
NSA_H = 8
NSA_KVH = 2
NSA_GQ = NSA_H // NSA_KVH
CMP_STRIDE = 16
CMP_BLOCK = 32
SEL_BLOCK = 64
CMP_PER_SEL = SEL_BLOCK // CMP_STRIDE
TOP_N = 16
WINDOW = 512
Q_BLOCK = 128
PAGE_SIZE = 128
GDN_HD = 128
GDN_H = 8
GDN_CHUNK = 64
GDN_SPLIT_LEVELS = 2
S5_GS = 16
S5_G = W_BR // S5_GS
S5_P = 64
S5_L = 8
NEG = -1e30
FORCE = 1e9
LOG2E = 1.4426950408889634

LANES = 128
SUBLANES = 8
VMEM_LIMIT = 56 * 1024 * 1024

C_ULRU = 0
C_GLRU = 1024
C_QNSA = 2048
C_GNSA = 3072
C_GGDN = 4096
C_US5 = 5120
C_GS5 = 6144
C_QKV = 7168
C_KV = 10240
C_SMALL = 11776
C_MERGE = 12288
NP = C_MERGE + N_BRANCH * D_MODEL
SM_GL = 0
SM_A = 24
SM_B = 32


def _cparams(sem):
    return pltpu.CompilerParams(dimension_semantics=sem, vmem_limit_bytes=VMEM_LIMIT)


def _sigmoid(x):
    return 1.0 / (1.0 + jnp.exp(-x))


def _silu(x):
    return x * _sigmoid(x)


def _softplus(x):
    return jnp.maximum(x, 0.0) + jnp.log1p(jnp.exp(-jnp.abs(x)))


def _dot(a, b):
    return jnp.dot(a, b, preferred_element_type=F32)


def _dot_hi(a, b):
    return jnp.dot(a, b, preferred_element_type=F32, precision=HI)


def _dot_nt(a, b):
    return lax.dot_general(a, b, (((1,), (1,)), ((), ())), preferred_element_type=F32)


def _dot_tn(a, b):
    return lax.dot_general(a, b, (((0,), (0,)), ((), ())), preferred_element_type=F32)


def _iota(shape, axis):
    return lax.broadcasted_iota(jnp.int32, shape, axis)


def _pack_w_in(w_in):
    o = 0
    seg = {}
    for name, size in (("u_lru", W_BR), ("g_lru", W_BR), ("q_nsa", W_BR), ("kv", 6 * NSA_KVH * NSA_HD),
                       ("gl", 3 * NSA_H), ("g_nsa", W_BR), ("qkv", 3 * W_BR), ("a", GDN_H), ("b", GDN_H),
                       ("g_gdn", W_BR), ("u_s5", W_BR), ("g_s5", W_BR), ("merge", N_BRANCH * D_MODEL)):
        seg[name] = w_in[..., o:o + size]
        o += size
    pad = jnp.zeros(w_in.shape[:-1] + (C_MERGE - C_SMALL - 40,), w_in.dtype)
    packed = jnp.concatenate(
        [seg["u_lru"], seg["g_lru"], seg["q_nsa"], seg["g_nsa"], seg["g_gdn"], seg["u_s5"], seg["g_s5"],
         seg["qkv"], seg["kv"], seg["gl"], seg["a"], seg["b"], pad, seg["merge"]], axis=-1)
    return packed.astype(BF16)


def _inproj_kernel(l_ref, x_ref, w_ref, o_ref, xb_ref):
    @pl.when(pl.program_id(1) == 0)
    def _():
        xb_ref[...] = x_ref[...].astype(BF16)

    o_ref[...] = _dot(xb_ref[...], w_ref[...])


def _inproj(lidx, x2d, wp):
    n = x2d.shape[0]
    tm = min(n, 1024)
    tn = 512
    return pl.pallas_call(
        _inproj_kernel,
        grid_spec=pltpu.PrefetchScalarGridSpec(
            num_scalar_prefetch=1, grid=(n // tm, NP // tn),
            in_specs=[pl.BlockSpec((tm, D_MODEL), lambda i, j, l: (i, 0)),
                      pl.BlockSpec((None, D_MODEL, tn), lambda i, j, l: (l[0], 0, j))],
            out_specs=pl.BlockSpec((tm, tn), lambda i, j, l: (i, j)),
            scratch_shapes=[pltpu.VMEM((tm, D_MODEL), BF16)]),
        out_shape=jax.ShapeDtypeStruct((n, NP), F32),
        compiler_params=_cparams(("parallel", "arbitrary")),
        name="inproj",
    )(lidx, x2d, wp)


def _lru_kernel(l_ref, u_ref, g_ref, buf_ref, h0_ref, cw_ref, cb_ref, wa_ref, ba_ref, wx_ref, bx_ref, lam_ref,
                y_ref, bufo_ref, ho_ref, xp_scr, a_scr, b_scr, h_scr, *, tt, nt):
    ti = pl.program_id(1)

    @pl.when(ti == 0)
    def _():
        xp_scr[5:8, :] = buf_ref[...]
        h_scr[...] = h0_ref[...]

    u = u_ref[...]
    xp_scr[8:8 + tt, :] = u
    cw = cw_ref[...]
    xc = (cb_ref[...] + cw[3:4] * u + cw[2:3] * xp_scr[7:7 + tt, :]
          + cw[1:2] * xp_scr[6:6 + tt, :] + cw[0:1] * xp_scr[5:5 + tt, :])
    tail = u[tt - 3:tt, :]
    xp_scr[5:8, :] = tail
    sp = _softplus(-lam_ref[...])
    for n in range(LRU_BLOCKS):
        sl = slice(n * LRU_BS, (n + 1) * LRU_BS)
        xn = xc[:, sl]
        xb = xn.astype(BF16)
        r = _sigmoid(_dot(xb, wa_ref[n]) + ba_ref[:, sl])
        i = _sigmoid(_dot(xb, wx_ref[n]) + bx_ref[:, sl])
        a = jnp.exp(-LRU_C * r * sp[:, sl])
        a_scr[:, sl] = a
        b_scr[:, sl] = jnp.sqrt(1.0 - a * a) * (i * xn)

    def body(i, h):
        for k in range(SUBLANES):
            t = i * SUBLANES + k
            h = a_scr[pl.ds(t, 1), :] * h + b_scr[pl.ds(t, 1), :]
            b_scr[pl.ds(t, 1), :] = h
        return h

    h = lax.fori_loop(0, tt // SUBLANES, body, h_scr[...])
    h_scr[...] = h
    y_ref[...] = (b_scr[...] * _silu(g_ref[...])).astype(y_ref.dtype)

    @pl.when(ti == nt - 1)
    def _():
        bufo_ref[...] = tail
        ho_ref[...] = h


def _lru(lidx, z3, buf, h0, p, ydtype):
    s, t, _ = z3.shape
    tt = min(t, 512)
    nt = t // tt
    wspec = lambda shape: pl.BlockSpec((None,) + shape, lambda b, i, l: (l[0],) + (0,) * len(shape))
    return pl.pallas_call(
        functools.partial(_lru_kernel, tt=tt, nt=nt),
        grid_spec=pltpu.PrefetchScalarGridSpec(
            num_scalar_prefetch=1, grid=(s, nt),
            in_specs=[pl.BlockSpec((None, tt, W_BR), lambda b, i, l: (b, i, C_ULRU // W_BR)),
                      pl.BlockSpec((None, tt, W_BR), lambda b, i, l: (b, i, C_GLRU // W_BR)),
                      pl.BlockSpec((None, 3, W_BR), lambda b, i, l: (b, 0, 0)),
                      pl.BlockSpec((None, 1, W_BR), lambda b, i, l: (b, 0, 0)),
                      wspec((CONV_K, W_BR)), wspec((1, W_BR)),
                      wspec((LRU_BLOCKS, LRU_BS, LRU_BS)), wspec((1, W_BR)),
                      wspec((LRU_BLOCKS, LRU_BS, LRU_BS)), wspec((1, W_BR)), wspec((1, W_BR))],
            out_specs=[pl.BlockSpec((None, tt, W_BR), lambda b, i, l: (b, i, 0)),
                       pl.BlockSpec((None, 3, W_BR), lambda b, i, l: (b, 0, 0)),
                       pl.BlockSpec((None, 1, W_BR), lambda b, i, l: (b, 0, 0))],
            scratch_shapes=[pltpu.VMEM((tt + 8, W_BR), F32), pltpu.VMEM((tt, W_BR), F32),
                            pltpu.VMEM((tt, W_BR), F32), pltpu.VMEM((1, W_BR), F32)]),
        out_shape=[jax.ShapeDtypeStruct((s, t, W_BR), ydtype),
                   jax.ShapeDtypeStruct((s, 3, W_BR), F32),
                   jax.ShapeDtypeStruct((s, 1, W_BR), F32)],
        compiler_params=_cparams(("parallel", "arbitrary")),
        name="rglru",
    )(lidx, z3, z3, buf, h0, p["lru_conv_w"], p["lru_conv_b"], p["lru_wa"], p["lru_ba"],
      p["lru_wx"], p["lru_bx"], p["lru_lambda"])


S5_CB = LANES // S5_GS
S5_SW = S5_CB * S5_P


def _gelu_tanh(x):
    return 0.5 * x * (1.0 + jnp.tanh(0.7978845608028654 * (x + 0.044715 * (x * x * x))))


def _s5_kernel(l_ref, u_ref, h0re_ref, h0im_ref, wst_ref, vout_ref, kt_ref, alre_ref, alim_ref, d_ref,
               y_ref, hre_ref, him_ref, hin_scr, s_scr, *, n, sb):
    rows = sb * n
    us = [u_ref[pl.ds(j, rows, stride=S5_L), :] for j in range(S5_L)]
    ub = jnp.concatenate(us, axis=1).astype(BF16)
    s = _dot(ub, wst_ref[...])
    alre = alre_ref[...]
    alim = alim_ref[...]
    h0re = h0re_ref[...]
    h0im = h0im_ref[...]
    if n == 1:
        hin_scr[:, :S5_SW] = h0re
        hin_scr[:, S5_SW:] = h0im
        hre = alre * h0re - alim * h0im + s[:, :S5_SW]
        him = alre * h0im + alim * h0re + s[:, S5_SW:]
    else:
        s_scr[...] = s

        def body(c, carry):
            hre, him = carry
            hin_scr[pl.ds(c, 1), :S5_SW] = hre
            hin_scr[pl.ds(c, 1), S5_SW:] = him
            srow = s_scr[pl.ds(c, 1), :]
            return (alre * hre - alim * him + srow[:, :S5_SW],
                    alre * him + alim * hre + srow[:, S5_SW:])

        hre, him = lax.fori_loop(0, n, body, (h0re, h0im))
    hre_ref[...] = hre
    him_ref[...] = him
    ycat = _dot(hin_scr[...].astype(BF16), vout_ref[...]) + _dot(ub, kt_ref[...])
    d = d_ref[...]
    for j in range(S5_L):
        yj = ycat[:, j * LANES:(j + 1) * LANES] + d * us[j]
        y_ref[pl.ds(j, rows, stride=S5_L), :] = _gelu_tanh(yj)


def _s5_scan(lidx, z3, h0re, h0im, p, sb):
    s, t, _ = z3.shape
    n = t // S5_L
    assert sb == 1 or n == 1
    sg = s // sb
    zr = z3.reshape(sg, sb * t, NP)
    ncb = W_BR // LANES
    wspec = lambda shape: pl.BlockSpec((None, None) + shape, lambda b, c, l: (l[0], c) + (0,) * len(shape))
    hspec = pl.BlockSpec((None, sb, S5_SW), lambda b, c, l: (b, 0, c))
    y, hre, him = pl.pallas_call(
        functools.partial(_s5_kernel, n=n, sb=sb),
        grid_spec=pltpu.PrefetchScalarGridSpec(
            num_scalar_prefetch=1, grid=(sg, ncb),
            in_specs=[pl.BlockSpec((None, sb * t, LANES), lambda b, c, l: (b, 0, C_US5 // LANES + c)),
                      hspec, hspec,
                      wspec((S5_L * LANES, 2 * S5_SW)), wspec((2 * S5_SW, S5_L * LANES)),
                      wspec((S5_L * LANES, S5_L * LANES)), wspec((1, S5_SW)), wspec((1, S5_SW)),
                      wspec((1, LANES))],
            out_specs=[pl.BlockSpec((None, sb * t, LANES), lambda b, c, l: (b, 0, c)), hspec, hspec],
            scratch_shapes=[pltpu.VMEM((sb * n, 2 * S5_SW), F32), pltpu.VMEM((sb * n, 2 * S5_SW), F32)]),
        out_shape=[jax.ShapeDtypeStruct((sg, sb * t, W_BR), F32),
                   jax.ShapeDtypeStruct((sg, sb, S5_G * S5_P), F32),
                   jax.ShapeDtypeStruct((sg, sb, S5_G * S5_P), F32)],
        compiler_params=_cparams(("parallel", "arbitrary")),
        name="s5_scan",
    )(lidx, zr, h0re, h0im, p["s5_wst"], p["s5_vout"], p["s5_kt"], p["s5_alre"], p["s5_alim"], p["s5_d"])
    return y.reshape(s * t, W_BR), hre, him


def _s5_glu_kernel(l_ref, y_ref, g_ref, w_ref, o_ref):
    gl = _dot(y_ref[...].astype(BF16), w_ref[...])
    o_ref[...] = (gl[:, :W_BR] * _sigmoid(gl[:, W_BR:]) * _silu(g_ref[...])).astype(o_ref.dtype)


def _s5_glu(lidx, y2d, z2d, p, ydtype):
    n = y2d.shape[0]
    tm = min(n, 512)
    return pl.pallas_call(
        _s5_glu_kernel,
        grid_spec=pltpu.PrefetchScalarGridSpec(
            num_scalar_prefetch=1, grid=(n // tm,),
            in_specs=[pl.BlockSpec((tm, W_BR), lambda i, l: (i, 0)),
                      pl.BlockSpec((tm, W_BR), lambda i, l: (i, C_GS5 // W_BR)),
                      pl.BlockSpec((None, W_BR, 2 * W_BR), lambda i, l: (l[0], 0, 0))],
            out_specs=pl.BlockSpec((tm, W_BR), lambda i, l: (i, 0))),
        out_shape=jax.ShapeDtypeStruct((n, W_BR), ydtype),
        compiler_params=_cparams(("parallel",)),
        name="s5_glu",
    )(lidx, y2d, z2d, p["s5_glu_w"])


def _s5_expand_kernel(wst_ref, vout_ref, kt_ref, wst_o, vout_o, kt_o):
    n = S5_L * LANES
    lp = S5_P.bit_length() - 1
    lc = S5_GS.bit_length() - 1
    lg = S5_CB.bit_length() - 1
    row = _iota((n, n), 0)
    col = _iota((n, n), 1)
    src = _iota((LANES, n), 0)
    dst = _iota((LANES, n), 1)
    rep_state = jnp.where(src == (((dst >> (lp + lg)) << lp) | (dst & (S5_P - 1))), 1.0, 0.0).astype(BF16)
    rep_out = jnp.where(src == (((dst >> (lc + lg)) << lc) | (dst & (S5_GS - 1))), 1.0, 0.0).astype(BF16)
    gi_in_row = (row >> lc) & (S5_CB - 1)
    gi_state_row = (row >> lp) & (S5_CB - 1)
    gi_state_col = (col >> lp) & (S5_CB - 1)
    gi_out_col = (col >> lc) & (S5_CB - 1)
    wst_o[...] = jnp.where(gi_in_row == gi_state_col, _dot(wst_ref[...].astype(BF16), rep_state), 0.0).astype(BF16)
    vout_o[...] = jnp.where(gi_state_row == gi_out_col, _dot(vout_ref[...].astype(BF16), rep_out), 0.0).astype(BF16)
    kt_o[...] = jnp.where(gi_in_row == gi_out_col, _dot(kt_ref[...].astype(BF16), rep_out), 0.0).astype(BF16)


def _s5_expand(wst, vout, kt):
    dd, ncb, n, _ = wst.shape
    cspec = pl.BlockSpec((None, None, n, LANES), lambda d, c: (d, c, 0, 0))
    ospec = pl.BlockSpec((None, None, n, n), lambda d, c: (d, c, 0, 0))
    oshape = jax.ShapeDtypeStruct((dd, ncb, n, n), BF16)
    return pl.pallas_call(
        _s5_expand_kernel, grid=(dd, ncb), in_specs=[cspec, cspec, cspec], out_specs=[ospec, ospec, ospec],
        out_shape=[oshape, oshape, oshape], compiler_params=_cparams(("parallel", "parallel")),
        name="s5_expand",
    )(wst, vout, kt)


def _s5_weights(w):
    dt = jnp.exp(w["s5_log_dt"])[..., None]
    lr, li = w["s5_lam_re"], w["s5_lam_im"]
    mag = jnp.exp(lr * dt)
    a_re = mag * jnp.cos(li * dt)
    a_im = mag * jnp.sin(li * dt)
    den = lr * lr + li * li
    f_re = ((a_re - 1.0) * lr + a_im * li) / den
    f_im = (a_im * lr - (a_re - 1.0) * li) / den
    bb_re = f_re[..., None] * w["s5_b_re"] - f_im[..., None] * w["s5_b_im"]
    bb_im = f_re[..., None] * w["s5_b_im"] + f_im[..., None] * w["s5_b_re"]
    pw_re = [jnp.ones_like(a_re)]
    pw_im = [jnp.zeros_like(a_im)]
    for _ in range(S5_L):
        pr, pi = pw_re[-1], pw_im[-1]
        pw_re.append(pr * a_re - pi * a_im)
        pw_im.append(pr * a_im + pi * a_re)
    pw_re = jnp.stack(pw_re, axis=1)
    pw_im = jnp.stack(pw_im, axis=1)
    dd = lr.shape[0]
    ncb = S5_G // S5_CB
    grp = lambda a: a.reshape(a.shape[0], a.shape[1], ncb, S5_CB, *a.shape[3:])
    rev_re = jnp.stack([pw_re[:, S5_L - 1 - j] for j in range(S5_L)], axis=1)
    rev_im = jnp.stack([pw_im[:, S5_L - 1 - j] for j in range(S5_L)], axis=1)
    st_re = rev_re[..., None] * bb_re[:, None] - rev_im[..., None] * bb_im[:, None]
    st_im = rev_re[..., None] * bb_im[:, None] + rev_im[..., None] * bb_re[:, None]

    def state_w(a):
        return grp(a).transpose(0, 2, 1, 3, 5, 4).reshape(dd, ncb, S5_L * LANES, S5_P)

    wst = jnp.concatenate([state_w(st_re), state_w(st_im)], axis=-1)
    c_re, c_im = w["s5_c_re"], w["s5_c_im"]
    nx_re, nx_im = pw_re[:, 1:], pw_im[:, 1:]
    ca_re = c_re[:, None] * nx_re[:, :, :, None] - c_im[:, None] * nx_im[:, :, :, None]
    ca_im = c_re[:, None] * nx_im[:, :, :, None] + c_im[:, None] * nx_re[:, :, :, None]

    def out_w(a):
        return grp(a).transpose(0, 2, 3, 5, 1, 4).reshape(dd, ncb, S5_SW, S5_L * S5_GS)

    vout = jnp.concatenate([out_w(ca_re), out_w(-ca_im)], axis=2)
    cat_re = c_re[:, None] * pw_re[:, :S5_L, :, None] - c_im[:, None] * pw_im[:, :S5_L, :, None]
    cat_im = c_re[:, None] * pw_im[:, :S5_L, :, None] + c_im[:, None] * pw_re[:, :S5_L, :, None]
    kk = (jnp.einsum("dlgcp,dgpe->dlgce", cat_re, bb_re, precision=HI)
          - jnp.einsum("dlgcp,dgpe->dlgce", cat_im, bb_im, precision=HI))
    zero = jnp.zeros_like(kk[:, 0])
    kt = jnp.stack([jnp.stack([kk[:, j - i] if j >= i else zero for j in range(S5_L)], axis=1)
                    for i in range(S5_L)], axis=1)
    kt = kt.reshape(dd, S5_L, S5_L, ncb, S5_CB, S5_GS, S5_GS)
    kt = kt.transpose(0, 3, 1, 4, 6, 2, 5).reshape(dd, ncb, S5_L * LANES, S5_L * S5_GS)
    wst, vout, kt = _s5_expand(wst, vout, kt)
    sw = lambda a: a.reshape(dd, ncb, 1, S5_SW)
    return dict(s5_wst=wst, s5_vout=vout, s5_kt=kt, s5_alre=sw(pw_re[:, S5_L]), s5_alim=sw(pw_im[:, S5_L]),
                s5_d=w["s5_d"].reshape(dd, W_BR // LANES, 1, LANES), s5_glu_w=w["s5_glu_w"].astype(BF16))


def _unit_lower_solve(ms, rhs, c):
    def split(a):
        hi = a.astype(BF16)
        return hi, (a - hi.astype(F32)).astype(BF16)

    def dot3(a, b):
        return _dot(a[0], b[0]) + _dot(a[1], b[0]) + _dot(a[0], b[1])

    n = range(len(ms))
    ms = [split(m) for m in ms]
    rs = [split(r) for r in rhs]
    xs = [rhs[i] - dot3(ms[i], rs[i]) for i in n]
    k = 2
    while k < c:
        if k <= 2 ** GDN_SPLIT_LEVELS:
            ms = [split(dot3(ms[i], ms[i])) for i in n]
            rs = [split(x) for x in xs]
            xs = [xs[i] + dot3(ms[i], rs[i]) for i in n]
        else:
            ms = [(_dot(ms[i][0], ms[i][0]).astype(BF16), None) for i in n]
            xs = [xs[i] + _dot(ms[i][0], xs[i].astype(BF16)) for i in n]
        k *= 2
    return xs


def _gdn_kernel(l_ref, q_ref, k_ref, v_ref, sm_ref, gg_ref, bq_ref, bk_ref, bv_ref, cwq_ref, cwk_ref, cwv_ref,
                alog_ref, dtb_ref, nw_ref, s0_ref, y_ref, bufo_ref, so_ref,
                xp_scr, qkv_scr, gx_scr, bx_scr, gc_scr, s_scr, *, tt, nt, c):
    ti = pl.program_id(1)
    log2c = c.bit_length() - 1

    @pl.when(ti == 0)
    def _():
        for i, b_ref in enumerate((bq_ref, bk_ref, bv_ref)):
            xp_scr[i, 5:8, :] = b_ref[...]
        s_scr[...] = s0_ref[...]

    for i, (x_ref, cw_ref) in enumerate(((q_ref, cwq_ref), (k_ref, cwk_ref), (v_ref, cwv_ref))):
        x = x_ref[...]
        xp_scr[i, 8:8 + tt, :] = x
        cw = cw_ref[...]
        cv = (cw[3:4] * x + cw[2:3] * xp_scr[i, 7:7 + tt, :]
              + cw[1:2] * xp_scr[i, 6:6 + tt, :] + cw[0:1] * xp_scr[i, 5:5 + tt, :])
        tail = x[tt - 3:tt, :]
        xp_scr[i, 5:8, :] = tail
        bufo_ref[:, i * W_BR:(i + 1) * W_BR] = tail
        cv = _silu(cv)
        if i < 2:
            scale = GDN_HD ** -0.5 if i == 0 else 1.0
            for h in range(GDN_H):
                sl = slice(h * GDN_HD, (h + 1) * GDN_HD)
                xh = cv[:, sl]
                qkv_scr[i, :, sl] = xh * (lax.rsqrt(jnp.sum(xh * xh, axis=-1, keepdims=True) + 1e-6) * scale)
        else:
            qkv_scr[i] = cv

    sm = sm_ref[...]
    gsm = -jnp.exp(alog_ref[...]) * _softplus(sm + dtb_ref[...])
    bsm = _sigmoid(sm)
    src = _iota((LANES, W_BR), 0)
    head = _iota((LANES, W_BR), 1) >> 7
    gx_scr[...] = _dot_hi(gsm, jnp.where(src - SM_A == head, 1.0, 0.0))
    bx_scr[...] = _dot_hi(bsm, jnp.where(src - SM_B == head, 1.0, 0.0))
    headc = _iota((LANES, GDN_H * c), 1) >> log2c
    gc_scr[...] = _dot_hi(gsm, jnp.where(_iota((LANES, GDN_H * c), 0) - SM_A == headc, 1.0, 0.0))

    rowi = _iota((c, c), 0)
    coli = _iota((c, c), 1)
    incl = coli <= rowi
    strict = coli < rowi
    ltri = jnp.where(incl, 1.0, 0.0)
    upper = jnp.where(_iota((c, GDN_H * c), 0) > (_iota((c, GDN_H * c), 1) & (c - 1)), 1.0, 0.0)
    nw = nw_ref[...]

    def chunk(ci, carry):
        r0 = pl.multiple_of(ci * c, c)
        gcb_all = _dot_hi(ltri, gx_scr[pl.ds(r0, c), :])
        diffs = _dot_hi(ltri, gc_scr[pl.ds(r0, c), :] * upper)
        beta_all = bx_scr[pl.ds(r0, c), :]
        hs = range(GDN_H)
        sls = [slice(h * GDN_HD, (h + 1) * GDN_HD) for h in hs]
        q = [qkv_scr[0, pl.ds(r0, c), sl] for sl in sls]
        k = [qkv_scr[1, pl.ds(r0, c), sl] for sl in sls]
        gcb = [gcb_all[:, sl] for sl in sls]
        decay = [jnp.where(incl, jnp.exp(diffs[:, h * c:(h + 1) * c]), 0.0) for h in hs]
        kb = [k[h] * beta_all[:, sls[h]] for h in hs]
        kbf = [x.astype(BF16) for x in k]
        m = [jnp.where(strict, _dot_nt(kb[h].astype(BF16), kbf[h]) * decay[h], 0.0) for h in hs]
        qk = [(_dot_nt(q[h].astype(BF16), kbf[h]) * decay[h]).astype(BF16) for h in hs]
        rhs = [jnp.concatenate([qkv_scr[2, pl.ds(r0, c), sls[h]] * beta_all[:, sls[h]], kb[h] * jnp.exp(gcb[h])],
                               axis=1) for h in hs]
        sol = _unit_lower_solve(m, rhs, c)
        s = [s_scr[h] for h in hs]
        sb = [x.astype(BF16) for x in s]
        vnb = [(sol[h][:, :GDN_HD] - _dot(sol[h][:, GDN_HD:].astype(BF16), sb[h])).astype(BF16) for h in hs]
        o = [_dot((q[h] * jnp.exp(gcb[h])).astype(BF16), sb[h]) + _dot(qk[h], vnb[h]) for h in hs]
        for h in hs:
            glast = gcb[h][c - 1:c, :]
            s_scr[h] = s[h] * jnp.exp(glast) + _dot_tn((k[h] * jnp.exp(glast - gcb[h])).astype(BF16), vnb[h])
        for h in hs:
            on = o[h] * lax.rsqrt(jnp.mean(o[h] * o[h], axis=-1, keepdims=True) + 1e-6) * nw
            y_ref[pl.ds(r0, c), sls[h]] = (on * _silu(gg_ref[pl.ds(r0, c), sls[h]])).astype(y_ref.dtype)
        return carry

    lax.fori_loop(0, tt // c, chunk, 0)

    @pl.when(ti == nt - 1)
    def _():
        so_ref[...] = s_scr[...]


def _gdn(lidx, z3, buf, s0, p, ydtype, chunk):
    s, t, _ = z3.shape
    tt = min(t, 256)
    nt = t // tt
    c = min(chunk, tt)
    zspec = lambda col: pl.BlockSpec((None, tt, W_BR), lambda b, i, l: (b, i, col // W_BR))
    bspec = lambda j: pl.BlockSpec((None, 3, W_BR), lambda b, i, l: (b, 0, j))
    cspec = lambda j: pl.BlockSpec((None, CONV_K, W_BR), lambda b, i, l: (l[0], 0, j))
    rspec = pl.BlockSpec((None, 1, LANES), lambda b, i, l: (l[0], 0, 0))
    sspec = pl.BlockSpec((None, GDN_H, GDN_HD, GDN_HD), lambda b, i, l: (b, 0, 0, 0))
    return pl.pallas_call(
        functools.partial(_gdn_kernel, tt=tt, nt=nt, c=c),
        grid_spec=pltpu.PrefetchScalarGridSpec(
            num_scalar_prefetch=1, grid=(s, nt),
            in_specs=[zspec(C_QKV), zspec(C_QKV + W_BR), zspec(C_QKV + 2 * W_BR),
                      pl.BlockSpec((None, tt, LANES), lambda b, i, l: (b, i, C_SMALL // LANES)),
                      zspec(C_GGDN), bspec(0), bspec(1), bspec(2), cspec(0), cspec(1), cspec(2),
                      rspec, rspec, rspec, sspec],
            out_specs=[pl.BlockSpec((None, tt, W_BR), lambda b, i, l: (b, i, 0)),
                       pl.BlockSpec((None, 3, 3 * W_BR), lambda b, i, l: (b, 0, 0)),
                       sspec],
            scratch_shapes=[pltpu.VMEM((3, tt + 8, W_BR), F32), pltpu.VMEM((3, tt, W_BR), F32),
                            pltpu.VMEM((tt, W_BR), F32), pltpu.VMEM((tt, W_BR), F32),
                            pltpu.VMEM((tt, GDN_H * c), F32), pltpu.VMEM((GDN_H, GDN_HD, GDN_HD), F32)]),
        out_shape=[jax.ShapeDtypeStruct((s, t, W_BR), ydtype),
                   jax.ShapeDtypeStruct((s, 3, 3 * W_BR), F32),
                   jax.ShapeDtypeStruct((s, GDN_H, GDN_HD, GDN_HD), F32)],
        compiler_params=_cparams(("parallel", "arbitrary")),
        name="gdn",
    )(lidx, z3, z3, z3, z3, z3, buf, buf, buf, p["gdn_conv_w"], p["gdn_conv_w"], p["gdn_conv_w"],
      p["gdn_a_log"], p["gdn_dt_bias"], p["gdn_norm_w"], s0)


PAGES_PER_STEP = 4


def _gather_kernel(l_ref, pt_ref, p0_ref, p1_ref, p2_ref, p3_ref, zn_ref, cmp_ref, sel_ref, *, n_past):
    i = pl.program_id(1)
    half = 2 * NSA_KVH * NSA_HD

    @pl.when(i < n_past)
    def _():
        for k, p_ref in enumerate((p0_ref, p1_ref, p2_ref, p3_ref)):
            rows = slice(k * PAGE_SIZE, (k + 1) * PAGE_SIZE)
            for r in range(4):
                for g in range(NSA_KVH):
                    x = p_ref[:, r, g, :]
                    cols = slice(((r % 2) * NSA_KVH + g) * NSA_HD, ((r % 2) * NSA_KVH + g + 1) * NSA_HD)
                    if r < 2:
                        cmp_ref[rows, cols] = x
                    else:
                        sel_ref[rows, cols] = x.astype(BF16)

    @pl.when(i == n_past)
    def _():
        sel_ref[...] = jnp.zeros(sel_ref.shape, BF16)
        new = zn_ref[:, half:2 * half]
        sel_ref[0:2 * SUBLANES, :] = jnp.concatenate([new, jnp.zeros_like(new)], axis=0).astype(BF16)


def _gather_ctx(lidx, pt_flat, cache, z3):
    b, t_new, _ = z3.shape
    assert t_new == SUBLANES
    n_pages = pt_flat.shape[0] // b
    n_past = n_pages // PAGES_PER_STEP
    rows = PAGES_PER_STEP * PAGE_SIZE
    width = 4 * NSA_KVH * NSA_HD
    half = width // 2

    def pspec(k):
        def imap(bi, i, l, pt):
            page = jnp.minimum(i * PAGES_PER_STEP + k, n_pages - 1)
            return (l[0], pt[bi * n_pages + page], 0, 0, 0, 0)
        return pl.BlockSpec((None, None, PAGE_SIZE, 4, NSA_KVH, NSA_HD), imap)

    return pl.pallas_call(
        functools.partial(_gather_kernel, n_past=n_past),
        grid_spec=pltpu.PrefetchScalarGridSpec(
            num_scalar_prefetch=2, grid=(b, n_past + 1),
            in_specs=[pspec(k) for k in range(PAGES_PER_STEP)]
            + [pl.BlockSpec((None, t_new, width), lambda bi, i, l, pt: (bi, 0, C_KV // width))],
            out_specs=[pl.BlockSpec((None, rows, half), lambda bi, i, l, pt: (bi, jnp.minimum(i, n_past - 1), 0)),
                       pl.BlockSpec((None, rows, half), lambda bi, i, l, pt: (bi, i, 0))]),
        out_shape=[jax.ShapeDtypeStruct((b, n_past * rows, half), F32),
                   jax.ShapeDtypeStruct((b, (n_past + 1) * rows, half), BF16)],
        compiler_params=_cparams(("parallel", "arbitrary")),
        name="nsa_gather",
    )(lidx, pt_flat, cache, cache, cache, cache, z3)


def _cmp_kernel(l_ref, x0_ref, x1_ref, w1_ref, pe_ref, w2_ref, o_ref, carry_scr, *, nh):
    @pl.when(pl.program_id(2) == 0)
    def _():
        carry_scr[...] = jnp.zeros(carry_scr.shape, F32)

    w1 = w1_ref[...]
    bias = _dot_hi(pe_ref[...], w1.astype(F32))
    last = _iota((nh, NSA_HD), 0) == nh - 1
    for g, x_ref in enumerate((x0_ref, x1_ref)):
        ucat = jnp.concatenate([x_ref[pl.ds(s, nh, stride=CMP_STRIDE), :] for s in range(CMP_STRIDE)], axis=1)
        hh = _dot(ucat.astype(BF16), w1)
        hf = hh[:, :NSA_HD] + bias[0:1, :NSA_HD]
        hs = hh[:, NSA_HD:] + bias[1:2, NSA_HD:]
        hs_next = jnp.where(last, carry_scr[g], pltpu.roll(hs, nh - 1, axis=0))
        carry_scr[g] = hs[0:1, :]
        hid = _silu(hf + hs_next)
        o_ref[g] = _dot(hid.astype(BF16), w2_ref[...]).astype(BF16)


def _compress(lidx, src3, col0, n_rows, p):
    b = src3.shape[0]
    tr = min(n_rows, 4096)
    nt = n_rows // tr
    nh = tr // CMP_STRIDE
    xspec = lambda g: pl.BlockSpec((None, tr, NSA_HD), lambda bi, c, i, l: (bi, nt - 1 - i, col0 + NSA_KVH * c + g))
    return pl.pallas_call(
        functools.partial(_cmp_kernel, nh=nh),
        grid_spec=pltpu.PrefetchScalarGridSpec(
            num_scalar_prefetch=1, grid=(b, 2, nt),
            in_specs=[xspec(0), xspec(1),
                      pl.BlockSpec((None, None, CMP_STRIDE * NSA_HD, 2 * NSA_HD), lambda bi, c, i, l: (l[0], c, 0, 0)),
                      pl.BlockSpec((None, SUBLANES, CMP_STRIDE * NSA_HD), lambda bi, c, i, l: (l[0], 0, 0)),
                      pl.BlockSpec((None, None, NSA_HD, NSA_HD), lambda bi, c, i, l: (l[0], c, 0, 0))],
            out_specs=pl.BlockSpec((None, None, NSA_KVH, nh, NSA_HD), lambda bi, c, i, l: (bi, c, 0, nt - 1 - i, 0)),
            scratch_shapes=[pltpu.VMEM((NSA_KVH, 1, NSA_HD), F32)]),
        out_shape=jax.ShapeDtypeStruct((b, 2, NSA_KVH, n_rows // CMP_STRIDE, NSA_HD), BF16),
        compiler_params=_cparams(("parallel", "parallel", "arbitrary")),
        name="nsa_compress",
    )(lidx, src3, src3, p["nsa_w1"], p["nsa_pe"], p["nsa_w2"])


def _masked_softmax(s, valid):
    s = jnp.where(valid, s, NEG)
    m = jnp.max(s, axis=-1, keepdims=True)
    e = jnp.where(valid, jnp.exp2(s - m), 0.0)
    den = jnp.sum(e, axis=-1, keepdims=True)
    return e / jnp.where(den > 0.0, den, 1.0)


def _attn_kernel(l_ref, q_ref, sm_ref, gn_ref, kc_ref, vc_ref, ks_ref, vs_ref, kw_ref, vw_ref, y_ref, *,
                 qb, pos0, ncp, ns, nsp, kt, wn, wpos0, tw):
    g = pl.program_id(1)
    q0 = pl.program_id(2) * qb
    qpos0 = pos0 + q0
    hd = NSA_HD
    q = q_ref[...] * (hd ** -0.5 * LOG2E)
    qr = jnp.concatenate([q[:, j * hd:(j + 1) * hd] for j in range(NSA_GQ)], axis=0).astype(BF16)
    slope_g = jnp.where(g == 0, LOG2E, LOG2E * 2.0 ** -NSA_GQ)
    slopes = [slope_g * 2.0 ** -(j + 1) for j in range(NSA_GQ)]
    heads = lambda a: [a[j * qb:(j + 1) * qb] for j in range(NSA_GQ)]

    dist = (qpos0 + _iota((qb, ncp), 0)) - (_iota((qb, ncp), 1) * CMP_STRIDE + (CMP_BLOCK - 1))
    valid = dist >= 0
    distf = dist.astype(F32)
    sc = heads(_dot_nt(qr, kc_ref[...]))
    pc = [_masked_softmax(sc[j] - slopes[j] * distf, valid) for j in range(NSA_GQ)]
    o_c = heads(_dot(jnp.concatenate(pc, axis=0).astype(BF16), vc_ref[...]))
    imp = pc[0] + pc[1] + pc[2] + pc[3]
    pool = jnp.where((_iota((ncp, nsp), 0) >> 2) == _iota((ncp, nsp), 1), 1.0, 0.0)
    imp = _dot_hi(imp, pool)

    blk = _iota((qb, nsp), 1)
    qp = qpos0 + _iota((qb, nsp), 0)
    val = jnp.where(blk * SEL_BLOCK > qp, -FORCE, imp)
    val = jnp.where(blk == (qp >> 6), FORCE, jnp.where(blk == 0, FORCE, val))
    val = jnp.where(blk >= ns, -3.0 * FORCE, val)
    rank = jnp.zeros((qb, nsp), F32)
    for bidx in range(ns):
        col = val[:, bidx:bidx + 1]
        rank = rank + jnp.where(col > val, 1.0, jnp.where(col == val, jnp.where(blk > bidx, 1.0, 0.0), 0.0))
    selb = jnp.where(rank < float(min(TOP_N, ns)), 1.0, 0.0).astype(BF16)

    n_tiles = (qpos0 + qb + kt - 1) // kt
    qh = heads(qr)
    spread = jnp.where(_iota((LANES, kt), 0) == (_iota((LANES, kt), 1) >> 6), 1.0, 0.0).astype(BF16)

    def tile_bias(t):
        k0 = t * kt
        d = (qpos0 + _iota((qb, kt), 0)) - (k0 + _iota((qb, kt), 1))
        pick_blk = jnp.where(_iota((nsp, LANES), 0) == (k0 >> 6) + _iota((nsp, LANES), 1), 1.0, 0.0).astype(BF16)
        sel_tile = _dot(selb, pick_blk).astype(BF16)
        ok = jnp.where(d >= 0, _dot(sel_tile, spread), 0.0) > 0.5
        return jnp.where(ok, d.astype(F32), -NEG)

    def tile(t, carry):
        k0 = pl.multiple_of(t * kt, kt)
        kk = ks_ref[pl.ds(k0, kt), :].astype(BF16)
        vv = vs_ref[pl.ds(k0, kt), :].astype(BF16)
        s = [_dot_nt(qh[j], kk) for j in range(NSA_GQ)]
        bias = carry[NSA_GQ]
        bias_next = tile_bias(t + 1)
        new = []
        for j in range(NSA_GQ):
            m, lsum, acc = carry[j]
            sm = s[j] - slopes[j] * bias
            m_new = jnp.maximum(m, jnp.max(sm, axis=-1, keepdims=True))
            pr = jnp.exp2(sm - m_new)
            alpha = jnp.exp2(m - m_new)
            new.append((m_new, alpha * lsum + jnp.sum(pr, axis=-1, keepdims=True),
                        alpha * acc + _dot(pr.astype(BF16), vv)))
        return tuple(new) + (bias_next,)

    init = (jnp.full((qb, 1), NEG, F32), jnp.zeros((qb, 1), F32), jnp.zeros((qb, hd), F32))
    fin = lax.fori_loop(0, n_tiles, tile, (init,) * NSA_GQ + (tile_bias(0),))
    o_s = [acc / lsum for _, lsum, acc in fin[:NSA_GQ]]

    k0w = pl.multiple_of(jnp.clip(q0 - WINDOW, 0, tw - wn), SUBLANES)
    dw = (qpos0 + _iota((qb, wn), 0)) - (wpos0 + k0w + _iota((qb, wn), 1))
    okw = jnp.abs(2 * dw - (WINDOW - 1)) < WINDOW
    dwf = dw.astype(F32)
    sw = heads(_dot_nt(qr, kw_ref[pl.ds(k0w, wn), :].astype(BF16)))
    pw = [_masked_softmax(sw[j] - slopes[j] * dwf, okw) for j in range(NSA_GQ)]
    o_w = heads(_dot(jnp.concatenate(pw, axis=0).astype(BF16), vw_ref[pl.ds(k0w, wn), :].astype(BF16)))

    gate = _sigmoid(sm_ref[...])
    lane = _iota((qb, LANES), 1)
    pick = lambda idx: jnp.sum(jnp.where(lane == idx, gate, 0.0), axis=-1, keepdims=True)
    for j in range(NSA_GQ):
        base = SM_GL + 3 * (g * NSA_GQ + j)
        o = pick(base) * o_c[j] + pick(base + 1) * o_s[j] + pick(base + 2) * o_w[j]
        y_ref[:, j * hd:(j + 1) * hd] = (o * _silu(gn_ref[:, j * hd:(j + 1) * hd])).astype(y_ref.dtype)


def _attention(lidx, z3, cmp_kv, ks_src, ks_col, vs_col, kw_src, kw_col, vw_col, pos0, wpos0, ydtype):
    b, t, _ = z3.shape
    qb = min(Q_BLOCK, t)
    ncp = cmp_kv.shape[3]
    tk = ks_src.shape[1]
    tw = kw_src.shape[1]
    kt = 512
    ns = -(-(pos0 + t) // SEL_BLOCK)
    nsp = -(-ns // LANES) * LANES
    wn = WINDOW + Q_BLOCK
    assert tk % kt == 0 and tk >= pos0 + t and tw >= wn and ncp // CMP_PER_SEL <= nsp
    hw = NSA_GQ * NSA_HD
    kern = functools.partial(_attn_kernel, qb=qb, pos0=pos0, ncp=ncp, ns=ns, nsp=nsp, kt=kt, wn=wn, wpos0=wpos0, tw=tw)
    kvspec = lambda rows, col: pl.BlockSpec((None, rows, NSA_HD), lambda bi, g, i, l: (bi, 0, col + g))
    cspec = lambda c: pl.BlockSpec((None, None, None, ncp, NSA_HD), lambda bi, g, i, l: (bi, c, g, 0, 0))
    return pl.pallas_call(
        kern,
        grid_spec=pltpu.PrefetchScalarGridSpec(
            num_scalar_prefetch=1, grid=(b, NSA_KVH, t // qb),
            in_specs=[pl.BlockSpec((None, qb, hw), lambda bi, g, i, l: (bi, i, C_QNSA // hw + g)),
                      pl.BlockSpec((None, qb, LANES), lambda bi, g, i, l: (bi, i, C_SMALL // LANES)),
                      pl.BlockSpec((None, qb, hw), lambda bi, g, i, l: (bi, i, C_GNSA // hw + g)),
                      cspec(0), cspec(1),
                      kvspec(tk, ks_col), kvspec(tk, vs_col), kvspec(tw, kw_col), kvspec(tw, vw_col)],
            out_specs=pl.BlockSpec((None, qb, hw), lambda bi, g, i, l: (bi, i, g))),
        out_shape=jax.ShapeDtypeStruct((b, t, W_BR), ydtype),
        compiler_params=_cparams(("parallel", "parallel", "arbitrary")),
        name="nsa_attention",
    )(lidx, z3, z3, z3, cmp_kv, cmp_kv, ks_src, ks_src, kw_src, kw_src)


def _merge_kernel(l_ref, y0_ref, y1_ref, y2_ref, y3_ref, g0_ref, g1_ref, g2_ref, g3_ref, w_ref, o_ref):
    acc = None
    for m, (y_ref, g_ref) in enumerate(((y0_ref, g0_ref), (y1_ref, g1_ref), (y2_ref, g2_ref), (y3_ref, g3_ref))):
        term = _sigmoid(g_ref[...]) * _dot(y_ref[...].astype(BF16), w_ref[m])
        acc = term if acc is None else acc + term
    o_ref[...] = acc.astype(o_ref.dtype)


def _merge(lidx, ys, z2d, p):
    n = z2d.shape[0]
    tm = min(n, 512)
    tn = 512
    yspec = pl.BlockSpec((tm, W_BR), lambda i, j, l: (i, 0))
    gspec = lambda m: pl.BlockSpec((tm, tn), lambda i, j, l: (i, (C_MERGE + m * D_MODEL) // tn + j))
    return pl.pallas_call(
        _merge_kernel,
        grid_spec=pltpu.PrefetchScalarGridSpec(
            num_scalar_prefetch=1, grid=(n // tm, D_MODEL // tn),
            in_specs=[yspec] * N_BRANCH + [gspec(m) for m in range(N_BRANCH)]
            + [pl.BlockSpec((None, N_BRANCH, W_BR, tn), lambda i, j, l: (l[0], 0, 0, j))],
            out_specs=pl.BlockSpec((tm, tn), lambda i, j, l: (i, j))),
        out_shape=jax.ShapeDtypeStruct((n, D_MODEL), BF16),
        compiler_params=_cparams(("parallel", "arbitrary")),
        name="merge",
    )(lidx, *ys, z2d, z2d, z2d, z2d, p["w_branch"])


def _outproj_kernel(l_ref, m_ref, x_ref, w_ref, g_ref, b_ref, o_ref, *, alpha):
    v = alpha * x_ref[...] + _dot(m_ref[...], w_ref[...])
    mu = jnp.mean(v, axis=-1, keepdims=True)
    c = v - mu
    var = jnp.mean(c * c, axis=-1, keepdims=True)
    o_ref[...] = c * lax.rsqrt(var + 1e-5) * g_ref[...] + b_ref[...]


def _outproj(lidx, merged, x2d, p, alpha):
    n = x2d.shape[0]
    tm = min(n, 256)
    return pl.pallas_call(
        functools.partial(_outproj_kernel, alpha=alpha),
        grid_spec=pltpu.PrefetchScalarGridSpec(
            num_scalar_prefetch=1, grid=(n // tm,),
            in_specs=[pl.BlockSpec((tm, D_MODEL), lambda i, l: (i, 0)),
                      pl.BlockSpec((tm, D_MODEL), lambda i, l: (i, 0)),
                      pl.BlockSpec((None, D_MODEL, D_MODEL), lambda i, l: (l[0], 0, 0)),
                      pl.BlockSpec((None, 1, D_MODEL), lambda i, l: (l[0], 0, 0)),
                      pl.BlockSpec((None, 1, D_MODEL), lambda i, l: (l[0], 0, 0))],
            out_specs=pl.BlockSpec((tm, D_MODEL), lambda i, l: (i, 0))),
        out_shape=jax.ShapeDtypeStruct((n, D_MODEL), F32),
        compiler_params=_cparams(("parallel",)),
        name="outproj_ln",
    )(lidx, merged, x2d, p["w_out"], p["ln_g"], p["ln_b"])


def _prep_params(w):
    p = {}
    row = lambda a: a[:, None, :]
    p["lru_conv_w"] = w["lru_conv_w"]
    p["lru_conv_b"] = row(w["lru_conv_b"])
    p["lru_wa"] = w["lru_wa"].astype(BF16)
    p["lru_ba"] = row(w["lru_ba"])
    p["lru_wx"] = w["lru_wx"].astype(BF16)
    p["lru_bx"] = row(w["lru_bx"])
    p["lru_lambda"] = row(w["lru_lambda"])
    p.update(_s5_weights(w))
    p["w_branch"] = w["w_branch"].astype(BF16)
    p["w_out"] = w["w_out"].astype(BF16)
    p["ln_g"] = row(w["ln_g"])
    p["ln_b"] = row(w["ln_b"])
    dd = w["gdn_a_log"].shape[0]
    lane_row = lambda a, off: jnp.zeros((dd, 1, LANES), F32).at[:, 0, off:off + a.shape[-1]].set(a)
    p["gdn_conv_w"] = w["gdn_conv_w"]
    p["gdn_a_log"] = lane_row(w["gdn_a_log"], SM_A)
    p["gdn_dt_bias"] = lane_row(w["gdn_dt_bias"], SM_A)
    p["gdn_norm_w"] = row(w["gdn_norm_w"])
    flat = CMP_STRIDE * NSA_HD
    w1 = w["nsa_cmp_w1"].reshape(dd, 2, 2, flat, NSA_HD).transpose(0, 1, 3, 2, 4)
    p["nsa_w1"] = w1.reshape(dd, 2, flat, 2 * NSA_HD).astype(BF16)
    pe = w["nsa_cmp_pos"].reshape(dd, 2, flat)
    p["nsa_pe"] = jnp.concatenate([pe, jnp.zeros((dd, SUBLANES - 2, flat), F32)], axis=1)
    p["nsa_w2"] = w["nsa_cmp_w2"].astype(BF16)
    return p


def _mixer_layer(lidx, x2d, s, t, state, nsa_branch, p, wp, alpha, ydtype):
    lru_buf, lru_h, gdn_buf, gdn_s, s5_re, s5_im = state
    z = _inproj(lidx, x2d, wp)
    z3 = z.reshape(s, t, NP)
    y_lru, lru_buf, lru_h = _lru(lidx, z3, lru_buf, lru_h, p, ydtype)
    y_nsa = nsa_branch(z3)
    y_gdn, gdn_buf, gdn_s = _gdn(lidx, z3, gdn_buf, gdn_s, p, ydtype, GDN_CHUNK)
    y_s5, s5_re, s5_im = _s5_scan(lidx, z3, s5_re, s5_im, p, s5_re.shape[1])
    y_s5 = _s5_glu(lidx, y_s5, z, p, ydtype)
    flat = lambda y: y.reshape(s * t, W_BR)
    merged = _merge(lidx, (flat(y_lru), flat(y_nsa), flat(y_gdn), y_s5), z, p)
    x_new = _outproj(lidx, merged, x2d, p, alpha)
    return x_new, z3, (lru_buf, lru_h, gdn_buf, gdn_s, s5_re, s5_im)


def kernel(x_prompt, x_sample, state_lru_h, state_lru_conv, cache_nsa_kv, cache_win_kv, state_gdn_s, state_gdn_conv, state_s5_re, state_s5_im, page_table, w_in, lru_conv_w, lru_conv_b, lru_wa, lru_ba, lru_wx, lru_bx, lru_lambda, nsa_cmp_pos, nsa_cmp_w1, nsa_cmp_w2, gdn_conv_w, gdn_a_log, gdn_dt_bias, gdn_norm_w, s5_lam_re, s5_lam_im, s5_log_dt, s5_b_re, s5_b_im, s5_c_re, s5_c_im, s5_d, s5_glu_w, w_branch, w_out, ln_g, ln_b):
    depth = w_in.shape[0]
    bp, tp, _ = x_prompt.shape
    db, ts, _ = x_sample.shape
    n_pages = page_table.shape[1]
    past = n_pages * PAGE_SIZE
    wbuf = cache_win_kv.shape[2]
    alpha = (2.0 * depth) ** 0.25
    kvw = 4 * NSA_KVH * NSA_HD
    winw = 2 * NSA_KVH * NSA_HD
    p = _prep_params(dict(
        lru_conv_w=lru_conv_w, lru_conv_b=lru_conv_b, lru_wa=lru_wa, lru_ba=lru_ba, lru_wx=lru_wx, lru_bx=lru_bx,
        lru_lambda=lru_lambda, nsa_cmp_pos=nsa_cmp_pos, nsa_cmp_w1=nsa_cmp_w1, nsa_cmp_w2=nsa_cmp_w2,
        gdn_conv_w=gdn_conv_w, gdn_a_log=gdn_a_log, gdn_dt_bias=gdn_dt_bias, gdn_norm_w=gdn_norm_w,
        s5_lam_re=s5_lam_re, s5_lam_im=s5_lam_im, s5_log_dt=s5_log_dt, s5_b_re=s5_b_re, s5_b_im=s5_b_im,
        s5_c_re=s5_c_re, s5_c_im=s5_c_im, s5_d=s5_d, s5_glu_w=s5_glu_w, w_branch=w_branch, w_out=w_out,
        ln_g=ln_g, ln_b=ln_b))
    wp = _pack_w_in(w_in)
    p, wp = lax.optimization_barrier((p, wp))
    pt_flat = page_table.reshape(-1).astype(jnp.int32)
    sb = db if ts == S5_L else 1
    c0 = C_KV // LANES
    zero_state = (jnp.zeros((bp, CONV_K - 1, W_BR), F32), jnp.zeros((bp, 1, W_BR), F32),
                  jnp.zeros((bp, CONV_K - 1, 3 * W_BR), F32), jnp.zeros((bp, GDN_H, GDN_HD, GDN_HD), F32),
                  jnp.zeros((bp, 1, S5_G * S5_P), F32), jnp.zeros((bp, 1, S5_G * S5_P), F32))

    def layer(carry, l):
        xp, xs = carry
        lidx = l.reshape(1)
        at = lambda a: lax.dynamic_index_in_dim(a, l, 0, keepdims=False)

        def nsa_prompt(z3):
            ckv = _compress(lidx, z3, c0, tp, p)
            return _attention(lidx, z3, ckv, z3, c0 + 4, c0 + 6, z3, c0 + 8, c0 + 10, 0, 0, BF16)

        xp, zp3, st_p = _mixer_layer(lidx, xp, bp, tp, zero_state, nsa_prompt, p, wp, alpha, BF16)

        win_state = {}

        def nsa_sample(z3):
            cmp_rows, sel_rows = _gather_ctx(lidx, pt_flat, cache_nsa_kv, z3)
            ckv = _compress(lidx, cmp_rows, 0, past, p)
            win = jnp.concatenate([at(cache_win_kv).reshape(db, wbuf, winw),
                                   z3[:, :, C_KV + kvw:C_KV + kvw + winw]], axis=1)
            win_state["win"] = win
            pad = max(0, WINDOW + Q_BLOCK - (wbuf + ts))
            win_pad = jnp.concatenate([win, jnp.zeros((db, pad, winw), F32)], axis=1)
            return _attention(lidx, z3, ckv, sel_rows, 0, 2, win_pad, 0, 2, past, past - wbuf, F32)

        st_in = (at(state_lru_conv), at(state_lru_h)[:, None, :], at(state_gdn_conv), at(state_gdn_s),
                 at(state_s5_re).reshape(db // sb, sb, S5_G * S5_P), at(state_s5_im).reshape(db // sb, sb, S5_G * S5_P))
        xs, zs3, st_s = _mixer_layer(lidx, xs, db, ts, st_in, nsa_sample, p, wp, alpha, F32)

        def outs(st, z3, s, t, win):
            lru_buf, lru_h, gdn_buf, gdn_s, s5_re, s5_im = st
            return (lru_h.reshape(s, W_BR), lru_buf,
                    z3[:, :, C_KV:C_KV + kvw].reshape(s, t, 4, NSA_KVH, NSA_HD),
                    win.reshape(s, win.shape[1], 2, NSA_KVH, NSA_HD),
                    gdn_s, gdn_buf, s5_re.reshape(s, S5_G, S5_P), s5_im.reshape(s, S5_G, S5_P))

        win_p = zp3[:, tp - min(WINDOW, tp):, C_KV + kvw:C_KV + kvw + winw]
        win_s = win_state["win"][:, wbuf + ts - min(WINDOW, past + ts):]
        return (xp, xs), (outs(st_p, zp3, bp, tp, win_p), outs(st_s, zs3, db, ts, win_s))

    (xp, xs), (op, os_) = lax.scan(layer, (x_prompt.reshape(bp * tp, D_MODEL), x_sample.reshape(db * ts, D_MODEL)),
                                   jnp.arange(depth, dtype=jnp.int32))
    return (xp.reshape(bp, tp, D_MODEL), xs.reshape(db, ts, D_MODEL)) + tuple(op) + tuple(os_)
```

```python
import functools

import jax
import jax.numpy as jnp
from jax import lax
from jax.experimental import pallas as pl
from jax.experimental.pallas import tpu as pltpu

F32 = jnp.float32
BF16 = jnp.bfloat16
HI = lax.Precision.HIGHEST

D_MODEL = 2048
W_BR = D_MODEL // 2
N_BRANCH = 4
CONV_K = 4
LRU_BLOCKS = 8
LRU_BS = W_BR // LRU_BLOCKS
LRU_C = 8.0
NSA_HD = 128
NSA_H = 8
NSA_KVH = 2
NSA_GQ = NSA_H // NSA_KVH
CMP_STRIDE = 16
CMP_BLOCK = 32
SEL_BLOCK = 64
CMP_PER_SEL = SEL_BLOCK // CMP_STRIDE
TOP_N = 16
WINDOW = 512
Q_BLOCK = 128
PAGE_SIZE = 128
GDN_HD = 128
GDN_H = 8
GDN_CHUNK = 64
GDN_SPLIT_LEVELS = 2
S5_GS = 16
S5_G = W_BR // S5_GS
S5_P = 64
S5_L = 8
NEG = -1e30
FORCE = 1e9
LOG2E = 1.4426950408889634

LANES = 128
SUBLANES = 8
VMEM_LIMIT = 56 * 1024 * 1024

C_ULRU = 0
C_GLRU = 1024
C_QNSA = 2048
C_GNSA = 3072
C_GGDN = 4096
C_US5 = 5120
C_GS5 = 6144
C_QKV = 7168
C_KV = 10240
C_SMALL = 11776
C_MERGE = 12288
NP = C_MERGE + N_BRANCH * D_MODEL
SM_GL = 0
SM_A = 24
SM_B = 32


def _cparams(sem):
    return pltpu.CompilerParams(dimension_semantics=sem, vmem_limit_bytes=VMEM_LIMIT)


def _sigmoid(x):
    return 1.0 / (1.0 + jnp.exp(-x))


def _silu(x):
    return x * _sigmoid(x)


def _softplus(x):
    return jnp.maximum(x, 0.0) + jnp.log1p(jnp.exp(-jnp.abs(x)))


def _dot(a, b):
    return jnp.dot(a, b, preferred_element_type=F32)


def _dot_hi(a, b):
    return jnp.dot(a, b, preferred_element_type=F32, precision=HI)


def _dot_nt(a, b):
    return lax.dot_general(a, b, (((1,), (1,)), ((), ())), preferred_element_type=F32)


def _dot_tn(a, b):
    return lax.dot_general(a, b, (((0,), (0,)), ((), ())), preferred_element_type=F32)


def _iota(shape, axis):
    return lax.broadcasted_iota(jnp.int32, shape, axis)


IN_SEGMENTS = (("u_lru", W_BR, C_ULRU), ("g_lru", W_BR, C_GLRU), ("q_nsa", W_BR, C_QNSA),
               ("kv", 6 * NSA_KVH * NSA_HD, C_KV), ("gl", 3 * NSA_H, C_SMALL + SM_GL), ("g_nsa", W_BR, C_GNSA),
               ("qkv", 3 * W_BR, C_QKV), ("a", GDN_H, C_SMALL + SM_A), ("b", GDN_H, C_SMALL + SM_B),
               ("g_gdn", W_BR, C_GGDN), ("u_s5", W_BR, C_US5), ("g_s5", W_BR, C_GS5),
               ("merge", N_BRANCH * D_MODEL, C_MERGE))
PACK_CHUNK = 1024


def _pack_kernel(x_ref, o_ref):
    rows = x_ref.shape[0]
    o_ref[:, C_SMALL:C_MERGE] = jnp.zeros((rows, C_MERGE - C_SMALL), BF16)
    src = 0
    small = []
    for _, width, dst in IN_SEGMENTS:
        if width < LANES:
            small.append((dst - C_SMALL, x_ref[:, src:src + width]))
        else:
            for c in range(0, width, PACK_CHUNK):
                n = min(PACK_CHUNK, width - c)
                o_ref[:, dst + c:dst + c + n] = x_ref[:, src + c:src + c + n].astype(BF16)
        src += width
    small.sort(key=lambda t: t[0])
    pieces, at = [], 0
    for off, val in small:
        assert off == at
        pieces.append(val)
        at += val.shape[1]
    pieces.append(jnp.zeros((rows, LANES - at), F32))
    o_ref[:, C_SMALL:C_SMALL + LANES] = jnp.concatenate(pieces, axis=1).astype(BF16)


def _pack_w_in(w_in):
    depth, d, n_in = w_in.shape
    assert n_in == sum(s[1] for s in IN_SEGMENTS)
    tr = 64
    return pl.pallas_call(
        _pack_kernel, grid=(depth, d // tr),
        in_specs=[pl.BlockSpec((None, tr, n_in), lambda l, i: (l, i, 0))],
        out_specs=pl.BlockSpec((None, tr, NP), lambda l, i: (l, i, 0)),
        out_shape=jax.ShapeDtypeStruct((depth, d, NP), BF16),
        compiler_params=_cparams(("parallel", "parallel")),
        name="pack_w_in",
    )(w_in)


def _inproj_kernel(l_ref, x_ref, w_ref, o_ref, xb_ref):
    @pl.when(pl.program_id(1) == 0)
    def _():
        xb_ref[...] = x_ref[...].astype(BF16)

    o_ref[...] = _dot(xb_ref[...], w_ref[...])


def _inproj(lidx, x2d, wp):
    n = x2d.shape[0]
    tm = min(n, 1024)
    tn = 512
    return pl.pallas_call(
        _inproj_kernel,
        grid_spec=pltpu.PrefetchScalarGridSpec(
            num_scalar_prefetch=1, grid=(n // tm, NP // tn),
            in_specs=[pl.BlockSpec((tm, D_MODEL), lambda i, j, l: (i, 0)),
                      pl.BlockSpec((None, D_MODEL, tn), lambda i, j, l: (l[0], 0, j))],
            out_specs=pl.BlockSpec((tm, tn), lambda i, j, l: (i, j)),
            scratch_shapes=[pltpu.VMEM((tm, D_MODEL), BF16)]),
        out_shape=jax.ShapeDtypeStruct((n, NP), F32),
        compiler_params=_cparams(("parallel", "arbitrary")),
        name="inproj",
    )(lidx, x2d, wp)


def _lru_kernel(l_ref, u_ref, g_ref, buf_ref, h0_ref, cw_ref, cb_ref, wa_ref, ba_ref, wx_ref, bx_ref, lam_ref,
                y_ref, bufo_ref, ho_ref, xp_scr, a_scr, b_scr, h_scr, *, tt, nt):
    ti = pl.program_id(1)

    @pl.when(ti == 0)
    def _():
        xp_scr[5:8, :] = buf_ref[...]
        h_scr[...] = h0_ref[...]

    u = u_ref[...]
    xp_scr[8:8 + tt, :] = u
    cw = cw_ref[...]
    xc = (cb_ref[...] + cw[3:4] * u + cw[2:3] * xp_scr[7:7 + tt, :]
          + cw[1:2] * xp_scr[6:6 + tt, :] + cw[0:1] * xp_scr[5:5 + tt, :])
    tail = u[tt - 3:tt, :]
    xp_scr[5:8, :] = tail
    sp = _softplus(-lam_ref[...])
    for n in range(LRU_BLOCKS):
        sl = slice(n * LRU_BS, (n + 1) * LRU_BS)
        xn = xc[:, sl]
        xb = xn.astype(BF16)
        r = _sigmoid(_dot(xb, wa_ref[n]) + ba_ref[:, sl])
        i = _sigmoid(_dot(xb, wx_ref[n]) + bx_ref[:, sl])
        a = jnp.exp(-LRU_C * r * sp[:, sl])
        a_scr[:, sl] = a
        b_scr[:, sl] = jnp.sqrt(1.0 - a * a) * (i * xn)

    def body(i, h):
        for k in range(SUBLANES):
            t = i * SUBLANES + k
            h = a_scr[pl.ds(t, 1), :] * h + b_scr[pl.ds(t, 1), :]
            b_scr[pl.ds(t, 1), :] = h
        return h

    h = lax.fori_loop(0, tt // SUBLANES, body, h_scr[...])
    h_scr[...] = h
    y_ref[...] = (b_scr[...] * _silu(g_ref[...])).astype(y_ref.dtype)

    @pl.when(ti == nt - 1)
    def _():
        bufo_ref[...] = tail
        ho_ref[...] = h


def _lru(lidx, z3, buf, h0, p, ydtype):
    s, t, _ = z3.shape
    tt = min(t, 512)
    nt = t // tt
    wspec = lambda shape: pl.BlockSpec((None,) + shape, lambda b, i, l: (l[0],) + (0,) * len(shape))
    return pl.pallas_call(
        functools.partial(_lru_kernel, tt=tt, nt=nt),
        grid_spec=pltpu.PrefetchScalarGridSpec(
            num_scalar_prefetch=1, grid=(s, nt),
            in_specs=[pl.BlockSpec((None, tt, W_BR), lambda b, i, l: (b, i, C_ULRU // W_BR)),
                      pl.BlockSpec((None, tt, W_BR), lambda b, i, l: (b, i, C_GLRU // W_BR)),
                      pl.BlockSpec((None, 3, W_BR), lambda b, i, l: (b, 0, 0)),
                      pl.BlockSpec((None, 1, W_BR), lambda b, i, l: (b, 0, 0)),
                      wspec((CONV_K, W_BR)), wspec((1, W_BR)),
                      wspec((LRU_BLOCKS, LRU_BS, LRU_BS)), wspec((1, W_BR)),
                      wspec((LRU_BLOCKS, LRU_BS, LRU_BS)), wspec((1, W_BR)), wspec((1, W_BR))],
            out_specs=[pl.BlockSpec((None, tt, W_BR), lambda b, i, l: (b, i, 0)),
                       pl.BlockSpec((None, 3, W_BR), lambda b, i, l: (b, 0, 0)),
                       pl.BlockSpec((None, 1, W_BR), lambda b, i, l: (b, 0, 0))],
            scratch_shapes=[pltpu.VMEM((tt + 8, W_BR), F32), pltpu.VMEM((tt, W_BR), F32),
                            pltpu.VMEM((tt, W_BR), F32), pltpu.VMEM((1, W_BR), F32)]),
        out_shape=[jax.ShapeDtypeStruct((s, t, W_BR), ydtype),
                   jax.ShapeDtypeStruct((s, 3, W_BR), F32),
                   jax.ShapeDtypeStruct((s, 1, W_BR), F32)],
        compiler_params=_cparams(("parallel", "arbitrary")),
        name="rglru",
    )(lidx, z3, z3, buf, h0, p["lru_conv_w"], p["lru_conv_b"], p["lru_wa"], p["lru_ba"],
      p["lru_wx"], p["lru_bx"], p["lru_lambda"])


S5_CB = LANES // S5_GS
S5_SW = S5_CB * S5_P


def _gelu_tanh(x):
    return 0.5 * x * (1.0 + jnp.tanh(0.7978845608028654 * (x + 0.044715 * (x * x * x))))


def _s5_kernel(l_ref, u_ref, h0re_ref, h0im_ref, wst_ref, vout_ref, kt_ref, alre_ref, alim_ref, d_ref,
               y_ref, hre_ref, him_ref, hin_scr, s_scr, *, n, sb):
    rows = sb * n
    us = [u_ref[pl.ds(j, rows, stride=S5_L), :] for j in range(S5_L)]
    ub = jnp.concatenate(us, axis=1).astype(BF16)
    s = _dot(ub, wst_ref[...])
    alre = alre_ref[...]
    alim = alim_ref[...]
    h0re = h0re_ref[...]
    h0im = h0im_ref[...]
    if n == 1:
        hin_scr[:, :S5_SW] = h0re
        hin_scr[:, S5_SW:] = h0im
        hre = alre * h0re - alim * h0im + s[:, :S5_SW]
        him = alre * h0im + alim * h0re + s[:, S5_SW:]
    else:
        s_scr[...] = s

        def body(c, carry):
            hre, him = carry
            hin_scr[pl.ds(c, 1), :S5_SW] = hre
            hin_scr[pl.ds(c, 1), S5_SW:] = him
            srow = s_scr[pl.ds(c, 1), :]
            return (alre * hre - alim * him + srow[:, :S5_SW],
                    alre * him + alim * hre + srow[:, S5_SW:])

        hre, him = lax.fori_loop(0, n, body, (h0re, h0im))
    hre_ref[...] = hre
    him_ref[...] = him
    ycat = _dot(hin_scr[...].astype(BF16), vout_ref[...]) + _dot(ub, kt_ref[...])
    d = d_ref[...]
    for j in range(S5_L):
        yj = ycat[:, j * LANES:(j + 1) * LANES] + d * us[j]
        y_ref[pl.ds(j, rows, stride=S5_L), :] = _gelu_tanh(yj)


def _s5_scan(lidx, z3, h0re, h0im, p, sb):
    s, t, _ = z3.shape
    n = t // S5_L
    assert sb == 1 or n == 1
    sg = s // sb
    zr = z3.reshape(sg, sb * t, NP)
    ncb = W_BR // LANES
    wspec = lambda shape: pl.BlockSpec((None, None) + shape, lambda b, c, l: (l[0], c) + (0,) * len(shape))
    hspec = pl.BlockSpec((None, sb, S5_SW), lambda b, c, l: (b, 0, c))
    y, hre, him = pl.pallas_call(
        functools.partial(_s5_kernel, n=n, sb=sb),
        grid_spec=pltpu.PrefetchScalarGridSpec(
            num_scalar_prefetch=1, grid=(sg, ncb),
            in_specs=[pl.BlockSpec((None, sb * t, LANES), lambda b, c, l: (b, 0, C_US5 // LANES + c)),
                      hspec, hspec,
                      wspec((S5_L * LANES, 2 * S5_SW)), wspec((2 * S5_SW, S5_L * LANES)),
                      wspec((S5_L * LANES, S5_L * LANES)), wspec((1, S5_SW)), wspec((1, S5_SW)),
                      wspec((1, LANES))],
            out_specs=[pl.BlockSpec((None, sb * t, LANES), lambda b, c, l: (b, 0, c)), hspec, hspec],
            scratch_shapes=[pltpu.VMEM((sb * n, 2 * S5_SW), F32), pltpu.VMEM((sb * n, 2 * S5_SW), F32)]),
        out_shape=[jax.ShapeDtypeStruct((sg, sb * t, W_BR), F32),
                   jax.ShapeDtypeStruct((sg, sb, S5_G * S5_P), F32),
                   jax.ShapeDtypeStruct((sg, sb, S5_G * S5_P), F32)],
        compiler_params=_cparams(("parallel", "arbitrary")),
        name="s5_scan",
    )(lidx, zr, h0re, h0im, p["s5_wst"], p["s5_vout"], p["s5_kt"], p["s5_alre"], p["s5_alim"], p["s5_d"])
    return y.reshape(s * t, W_BR), hre, him


def _s5_glu_kernel(l_ref, y_ref, g_ref, w_ref, o_ref):
    gl = _dot(y_ref[...].astype(BF16), w_ref[...])
    o_ref[...] = (gl[:, :W_BR] * _sigmoid(gl[:, W_BR:]) * _silu(g_ref[...])).astype(o_ref.dtype)


def _s5_glu(lidx, y2d, z2d, p, ydtype):
    n = y2d.shape[0]
    tm = min(n, 512)
    return pl.pallas_call(
        _s5_glu_kernel,
        grid_spec=pltpu.PrefetchScalarGridSpec(
            num_scalar_prefetch=1, grid=(n // tm,),
            in_specs=[pl.BlockSpec((tm, W_BR), lambda i, l: (i, 0)),
                      pl.BlockSpec((tm, W_BR), lambda i, l: (i, C_GS5 // W_BR)),
                      pl.BlockSpec((None, W_BR, 2 * W_BR), lambda i, l: (l[0], 0, 0))],
            out_specs=pl.BlockSpec((tm, W_BR), lambda i, l: (i, 0))),
        out_shape=jax.ShapeDtypeStruct((n, W_BR), ydtype),
        compiler_params=_cparams(("parallel",)),
        name="s5_glu",
    )(lidx, y2d, z2d, p["s5_glu_w"])


def _s5_expand_kernel(wst_ref, vout_ref, kt_ref, wst_o, vout_o, kt_o):
    n = S5_L * LANES
    lp = S5_P.bit_length() - 1
    lc = S5_GS.bit_length() - 1
    lg = S5_CB.bit_length() - 1
    row = _iota((n, n), 0)
    col = _iota((n, n), 1)
    src = _iota((LANES, n), 0)
    dst = _iota((LANES, n), 1)
    rep_state = jnp.where(src == (((dst >> (lp + lg)) << lp) | (dst & (S5_P - 1))), 1.0, 0.0).astype(BF16)
    rep_out = jnp.where(src == (((dst >> (lc + lg)) << lc) | (dst & (S5_GS - 1))), 1.0, 0.0).astype(BF16)
    gi_in_row = (row >> lc) & (S5_CB - 1)
    gi_state_row = (row >> lp) & (S5_CB - 1)
    gi_state_col = (col >> lp) & (S5_CB - 1)
    gi_out_col = (col >> lc) & (S5_CB - 1)
    wst_o[...] = jnp.where(gi_in_row == gi_state_col, _dot(wst_ref[...].astype(BF16), rep_state), 0.0).astype(BF16)
    vout_o[...] = jnp.where(gi_state_row == gi_out_col, _dot(vout_ref[...].astype(BF16), rep_out), 0.0).astype(BF16)
    kt_o[...] = jnp.where(gi_in_row == gi_out_col, _dot(kt_ref[...].astype(BF16), rep_out), 0.0).astype(BF16)


def _s5_expand(wst, vout, kt):
    dd, ncb, n, _ = wst.shape
    cspec = pl.BlockSpec((None, None, n, LANES), lambda d, c: (d, c, 0, 0))
    ospec = pl.BlockSpec((None, None, n, n), lambda d, c: (d, c, 0, 0))
    oshape = jax.ShapeDtypeStruct((dd, ncb, n, n), BF16)
    return pl.pallas_call(
        _s5_expand_kernel, grid=(dd, ncb), in_specs=[cspec, cspec, cspec], out_specs=[ospec, ospec, ospec],
        out_shape=[oshape, oshape, oshape], compiler_params=_cparams(("parallel", "parallel")),
        name="s5_expand",
    )(wst, vout, kt)


def _s5_weights(w):
    dt = jnp.exp(w["s5_log_dt"])[..., None]
    lr, li = w["s5_lam_re"], w["s5_lam_im"]
    mag = jnp.exp(lr * dt)
    a_re = mag * jnp.cos(li * dt)
    a_im = mag * jnp.sin(li * dt)
    den = lr * lr + li * li
    f_re = ((a_re - 1.0) * lr + a_im * li) / den
    f_im = (a_im * lr - (a_re - 1.0) * li) / den
    bb_re = f_re[..., None] * w["s5_b_re"] - f_im[..., None] * w["s5_b_im"]
    bb_im = f_re[..., None] * w["s5_b_im"] + f_im[..., None] * w["s5_b_re"]
    pw_re = [jnp.ones_like(a_re)]
    pw_im = [jnp.zeros_like(a_im)]
    for _ in range(S5_L):
        pr, pi = pw_re[-1], pw_im[-1]
        pw_re.append(pr * a_re - pi * a_im)
        pw_im.append(pr * a_im + pi * a_re)
    pw_re = jnp.stack(pw_re, axis=1)
    pw_im = jnp.stack(pw_im, axis=1)
    dd = lr.shape[0]
    ncb = S5_G // S5_CB
    grp = lambda a: a.reshape(a.shape[0], a.shape[1], ncb, S5_CB, *a.shape[3:])
    rev_re = jnp.stack([pw_re[:, S5_L - 1 - j] for j in range(S5_L)], axis=1)
    rev_im = jnp.stack([pw_im[:, S5_L - 1 - j] for j in range(S5_L)], axis=1)
    st_re = rev_re[..., None] * bb_re[:, None] - rev_im[..., None] * bb_im[:, None]
    st_im = rev_re[..., None] * bb_im[:, None] + rev_im[..., None] * bb_re[:, None]

    def state_w(a):
        return grp(a).transpose(0, 2, 1, 3, 5, 4).reshape(dd, ncb, S5_L * LANES, S5_P)

    wst = jnp.concatenate([state_w(st_re), state_w(st_im)], axis=-1)
    c_re, c_im = w["s5_c_re"], w["s5_c_im"]
    nx_re, nx_im = pw_re[:, 1:], pw_im[:, 1:]
    ca_re = c_re[:, None] * nx_re[:, :, :, None] - c_im[:, None] * nx_im[:, :, :, None]
    ca_im = c_re[:, None] * nx_im[:, :, :, None] + c_im[:, None] * nx_re[:, :, :, None]

    def out_w(a):
        return grp(a).transpose(0, 2, 3, 5, 1, 4).reshape(dd, ncb, S5_SW, S5_L * S5_GS)

    vout = jnp.concatenate([out_w(ca_re), out_w(-ca_im)], axis=2)
    cat_re = c_re[:, None] * pw_re[:, :S5_L, :, None] - c_im[:, None] * pw_im[:, :S5_L, :, None]
    cat_im = c_re[:, None] * pw_im[:, :S5_L, :, None] + c_im[:, None] * pw_re[:, :S5_L, :, None]
    kk = (jnp.einsum("dlgcp,dgpe->dlgce", cat_re, bb_re, precision=HI)
          - jnp.einsum("dlgcp,dgpe->dlgce", cat_im, bb_im, precision=HI))
    zero = jnp.zeros_like(kk[:, 0])
    kt = jnp.stack([jnp.stack([kk[:, j - i] if j >= i else zero for j in range(S5_L)], axis=1)
                    for i in range(S5_L)], axis=1)
    kt = kt.reshape(dd, S5_L, S5_L, ncb, S5_CB, S5_GS, S5_GS)
    kt = kt.transpose(0, 3, 1, 4, 6, 2, 5).reshape(dd, ncb, S5_L * LANES, S5_L * S5_GS)
    wst, vout, kt = _s5_expand(wst, vout, kt)
    sw = lambda a: a.reshape(dd, ncb, 1, S5_SW)
    return dict(s5_wst=wst, s5_vout=vout, s5_kt=kt, s5_alre=sw(pw_re[:, S5_L]), s5_alim=sw(pw_im[:, S5_L]),
                s5_d=w["s5_d"].reshape(dd, W_BR // LANES, 1, LANES), s5_glu_w=w["s5_glu_w"].astype(BF16))


def _unit_lower_solve(ms, rhs, c):
    def split(a):
        hi = a.astype(BF16)
        return hi, (a - hi.astype(F32)).astype(BF16)

    def dot3(a, b):
        return _dot(a[0], b[0]) + _dot(a[1], b[0]) + _dot(a[0], b[1])

    n = range(len(ms))
    ms = [split(m) for m in ms]
    rs = [split(r) for r in rhs]
    xs = [rhs[i] - dot3(ms[i], rs[i]) for i in n]
    k = 2
    while k < c:
        if k <= 2 ** GDN_SPLIT_LEVELS:
            ms = [split(dot3(ms[i], ms[i])) for i in n]
            rs = [split(x) for x in xs]
            xs = [xs[i] + dot3(ms[i], rs[i]) for i in n]
        else:
            ms = [(_dot(ms[i][0], ms[i][0]).astype(BF16), None) for i in n]
            xs = [xs[i] + _dot(ms[i][0], xs[i].astype(BF16)) for i in n]
        k *= 2
    return xs


def _gdn_kernel(l_ref, q_ref, k_ref, v_ref, sm_ref, gg_ref, bq_ref, bk_ref, bv_ref, cwq_ref, cwk_ref, cwv_ref,
                alog_ref, dtb_ref, nw_ref, s0_ref, y_ref, bufo_ref, so_ref,
                xp_scr, qkv_scr, gx_scr, bx_scr, gc_scr, s_scr, *, tt, nt, c):
    ti = pl.program_id(1)
    log2c = c.bit_length() - 1

    @pl.when(ti == 0)
    def _():
        for i, b_ref in enumerate((bq_ref, bk_ref, bv_ref)):
            xp_scr[i, 5:8, :] = b_ref[...]
        s_scr[...] = s0_ref[...]

    for i, (x_ref, cw_ref) in enumerate(((q_ref, cwq_ref), (k_ref, cwk_ref), (v_ref, cwv_ref))):
        x = x_ref[...]
        xp_scr[i, 8:8 + tt, :] = x
        cw = cw_ref[...]
        cv = (cw[3:4] * x + cw[2:3] * xp_scr[i, 7:7 + tt, :]
              + cw[1:2] * xp_scr[i, 6:6 + tt, :] + cw[0:1] * xp_scr[i, 5:5 + tt, :])
        tail = x[tt - 3:tt, :]
        xp_scr[i, 5:8, :] = tail
        bufo_ref[:, i * W_BR:(i + 1) * W_BR] = tail
        cv = _silu(cv)
        if i < 2:
            scale = GDN_HD ** -0.5 if i == 0 else 1.0
            for h in range(GDN_H):
                sl = slice(h * GDN_HD, (h + 1) * GDN_HD)
                xh = cv[:, sl]
                qkv_scr[i, :, sl] = xh * (lax.rsqrt(jnp.sum(xh * xh, axis=-1, keepdims=True) + 1e-6) * scale)
        else:
            qkv_scr[i] = cv

    sm = sm_ref[...]
    gsm = -jnp.exp(alog_ref[...]) * _softplus(sm + dtb_ref[...])
    bsm = _sigmoid(sm)
    src = _iota((LANES, W_BR), 0)
    head = _iota((LANES, W_BR), 1) >> 7
    gx_scr[...] = _dot_hi(gsm, jnp.where(src - SM_A == head, 1.0, 0.0))
    bx_scr[...] = _dot_hi(bsm, jnp.where(src - SM_B == head, 1.0, 0.0))
    headc = _iota((LANES, GDN_H * c), 1) >> log2c
    gc_scr[...] = _dot_hi(gsm, jnp.where(_iota((LANES, GDN_H * c), 0) - SM_A == headc, 1.0, 0.0))

    rowi = _iota((c, c), 0)
    coli = _iota((c, c), 1)
    incl = coli <= rowi
    strict = coli < rowi
    ltri = jnp.where(incl, 1.0, 0.0)
    upper = jnp.where(_iota((c, GDN_H * c), 0) > (_iota((c, GDN_H * c), 1) & (c - 1)), 1.0, 0.0)
    nw = nw_ref[...]

    def chunk(ci, carry):
        r0 = pl.multiple_of(ci * c, c)
        gcb_all = _dot_hi(ltri, gx_scr[pl.ds(r0, c), :])
        diffs = _dot_hi(ltri, gc_scr[pl.ds(r0, c), :] * upper)
        beta_all = bx_scr[pl.ds(r0, c), :]
        hs = range(GDN_H)
        sls = [slice(h * GDN_HD, (h + 1) * GDN_HD) for h in hs]
        q = [qkv_scr[0, pl.ds(r0, c), sl] for sl in sls]
        k = [qkv_scr[1, pl.ds(r0, c), sl] for sl in sls]
        gcb = [gcb_all[:, sl] for sl in sls]
        decay = [jnp.where(incl, jnp.exp(diffs[:, h * c:(h + 1) * c]), 0.0) for h in hs]
        kb = [k[h] * beta_all[:, sls[h]] for h in hs]
        kbf = [x.astype(BF16) for x in k]
        m = [jnp.where(strict, _dot_nt(kb[h].astype(BF16), kbf[h]) * decay[h], 0.0) for h in hs]
        qk = [(_dot_nt(q[h].astype(BF16), kbf[h]) * decay[h]).astype(BF16) for h in hs]
        rhs = [jnp.concatenate([qkv_scr[2, pl.ds(r0, c), sls[h]] * beta_all[:, sls[h]], kb[h] * jnp.exp(gcb[h])],
                               axis=1) for h in hs]
        sol = _unit_lower_solve(m, rhs, c)
        s = [s_scr[h] for h in hs]
        sb = [x.astype(BF16) for x in s]
        vnb = [(sol[h][:, :GDN_HD] - _dot(sol[h][:, GDN_HD:].astype(BF16), sb[h])).astype(BF16) for h in hs]
        o = [_dot((q[h] * jnp.exp(gcb[h])).astype(BF16), sb[h]) + _dot(qk[h], vnb[h]) for h in hs]
        for h in hs:
            glast = gcb[h][c - 1:c, :]
            s_scr[h] = s[h] * jnp.exp(glast) + _dot_tn((k[h] * jnp.exp(glast - gcb[h])).astype(BF16), vnb[h])
        for h in hs:
            on = o[h] * lax.rsqrt(jnp.mean(o[h] * o[h], axis=-1, keepdims=True) + 1e-6) * nw
            y_ref[pl.ds(r0, c), sls[h]] = (on * _silu(gg_ref[pl.ds(r0, c), sls[h]])).astype(y_ref.dtype)
        return carry

    lax.fori_loop(0, tt // c, chunk, 0)

    @pl.when(ti == nt - 1)
    def _():
        so_ref[...] = s_scr[...]


def _gdn(lidx, z3, buf, s0, p, ydtype, chunk):
    s, t, _ = z3.shape
    tt = min(t, 256)
    nt = t // tt
    c = min(chunk, tt)
    zspec = lambda col: pl.BlockSpec((None, tt, W_BR), lambda b, i, l: (b, i, col // W_BR))
    bspec = lambda j: pl.BlockSpec((None, 3, W_BR), lambda b, i, l: (b, 0, j))
    cspec = lambda j: pl.BlockSpec((None, CONV_K, W_BR), lambda b, i, l: (l[0], 0, j))
    rspec = pl.BlockSpec((None, 1, LANES), lambda b, i, l: (l[0], 0, 0))
    sspec = pl.BlockSpec((None, GDN_H, GDN_HD, GDN_HD), lambda b, i, l: (b, 0, 0, 0))
    return pl.pallas_call(
        functools.partial(_gdn_kernel, tt=tt, nt=nt, c=c),
        grid_spec=pltpu.PrefetchScalarGridSpec(
            num_scalar_prefetch=1, grid=(s, nt),
            in_specs=[zspec(C_QKV), zspec(C_QKV + W_BR), zspec(C_QKV + 2 * W_BR),
                      pl.BlockSpec((None, tt, LANES), lambda b, i, l: (b, i, C_SMALL // LANES)),
                      zspec(C_GGDN), bspec(0), bspec(1), bspec(2), cspec(0), cspec(1), cspec(2),
                      rspec, rspec, rspec, sspec],
            out_specs=[pl.BlockSpec((None, tt, W_BR), lambda b, i, l: (b, i, 0)),
                       pl.BlockSpec((None, 3, 3 * W_BR), lambda b, i, l: (b, 0, 0)),
                       sspec],
            scratch_shapes=[pltpu.VMEM((3, tt + 8, W_BR), F32), pltpu.VMEM((3, tt, W_BR), F32),
                            pltpu.VMEM((tt, W_BR), F32), pltpu.VMEM((tt, W_BR), F32),
                            pltpu.VMEM((tt, GDN_H * c), F32), pltpu.VMEM((GDN_H, GDN_HD, GDN_HD), F32)]),
        out_shape=[jax.ShapeDtypeStruct((s, t, W_BR), ydtype),
                   jax.ShapeDtypeStruct((s, 3, 3 * W_BR), F32),
                   jax.ShapeDtypeStruct((s, GDN_H, GDN_HD, GDN_HD), F32)],
        compiler_params=_cparams(("parallel", "arbitrary")),
        name="gdn",
    )(lidx, z3, z3, z3, z3, z3, buf, buf, buf, p["gdn_conv_w"], p["gdn_conv_w"], p["gdn_conv_w"],
      p["gdn_a_log"], p["gdn_dt_bias"], p["gdn_norm_w"], s0)


PAGES_PER_STEP = 4


def _gather_kernel(l_ref, pt_ref, p0_ref, p1_ref, p2_ref, p3_ref, zn_ref, cmp_ref, sel_ref, *, n_past):
    i = pl.program_id(1)
    half = 2 * NSA_KVH * NSA_HD

    @pl.when(i < n_past)
    def _():
        for k, p_ref in enumerate((p0_ref, p1_ref, p2_ref, p3_ref)):
            rows = slice(k * PAGE_SIZE, (k + 1) * PAGE_SIZE)
            for r in range(4):
                for g in range(NSA_KVH):
                    x = p_ref[pl.ds(r * NSA_KVH + g, PAGE_SIZE, stride=4 * NSA_KVH), :]
                    cols = slice(((r % 2) * NSA_KVH + g) * NSA_HD, ((r % 2) * NSA_KVH + g + 1) * NSA_HD)
                    if r < 2:
                        cmp_ref[rows, cols] = x
                    else:
                        sel_ref[rows, cols] = x.astype(BF16)

    @pl.when(i == n_past)
    def _():
        sel_ref[...] = jnp.zeros(sel_ref.shape, BF16)
        new = zn_ref[:, half:2 * half]
        sel_ref[0:2 * SUBLANES, :] = jnp.concatenate([new, jnp.zeros_like(new)], axis=0).astype(BF16)


def _gather_ctx(lidx, pt_flat, cache, z3):
    b, t_new, _ = z3.shape
    assert t_new == SUBLANES
    n_pages = pt_flat.shape[0] // b
    n_past = n_pages // PAGES_PER_STEP
    rows = PAGES_PER_STEP * PAGE_SIZE
    width = 4 * NSA_KVH * NSA_HD
    half = width // 2
    n_pool = cache.shape[1]
    cache = cache.reshape(cache.shape[0] * n_pool, PAGE_SIZE * 4 * NSA_KVH, NSA_HD)

    def pspec(k):
        def imap(bi, i, l, pt):
            page = jnp.minimum(i * PAGES_PER_STEP + k, n_pages - 1)
            return (l[0] * n_pool + pt[bi * n_pages + page], 0, 0)
        return pl.BlockSpec((None, PAGE_SIZE * 4 * NSA_KVH, NSA_HD), imap)

    return pl.pallas_call(
        functools.partial(_gather_kernel, n_past=n_past),
        grid_spec=pltpu.PrefetchScalarGridSpec(
            num_scalar_prefetch=2, grid=(b, n_past + 1),
            in_specs=[pspec(k) for k in range(PAGES_PER_STEP)]
            + [pl.BlockSpec((None, t_new, width), lambda bi, i, l, pt: (bi, 0, C_KV // width))],
            out_specs=[pl.BlockSpec((None, rows, half), lambda bi, i, l, pt: (bi, jnp.minimum(i, n_past - 1), 0)),
                       pl.BlockSpec((None, rows, half), lambda bi, i, l, pt: (bi, i, 0))]),
        out_shape=[jax.ShapeDtypeStruct((b, n_past * rows, half), F32),
                   jax.ShapeDtypeStruct((b, (n_past + 1) * rows, half), BF16)],
        compiler_params=_cparams(("parallel", "arbitrary")),
        name="nsa_gather",
    )(lidx, pt_flat, cache, cache, cache, cache, z3)


def _cmp_kernel(l_ref, x0_ref, x1_ref, w1_ref, pe_ref, w2_ref, o_ref, carry_scr, *, nh):
    @pl.when(pl.program_id(2) == 0)
    def _():
        carry_scr[...] = jnp.zeros(carry_scr.shape, F32)

    w1 = w1_ref[...]
    bias = _dot_hi(pe_ref[...], w1.astype(F32))
    last = _iota((nh, NSA_HD), 0) == nh - 1
    for g, x_ref in enumerate((x0_ref, x1_ref)):
        ucat = jnp.concatenate([x_ref[pl.ds(s, nh, stride=CMP_STRIDE), :] for s in range(CMP_STRIDE)], axis=1)
        hh = _dot(ucat.astype(BF16), w1)
        hf = hh[:, :NSA_HD] + bias[0:1, :NSA_HD]
        hs = hh[:, NSA_HD:] + bias[1:2, NSA_HD:]
        hs_next = jnp.where(last, carry_scr[g], pltpu.roll(hs, nh - 1, axis=0))
        carry_scr[g] = hs[0:1, :]
        hid = _silu(hf + hs_next)
        o_ref[g] = _dot(hid.astype(BF16), w2_ref[...]).astype(BF16)


def _compress(lidx, src3, col0, n_rows, p):
    b = src3.shape[0]
    tr = min(n_rows, 4096)
    nt = n_rows // tr
    nh = tr // CMP_STRIDE
    xspec = lambda g: pl.BlockSpec((None, tr, NSA_HD), lambda bi, c, i, l: (bi, nt - 1 - i, col0 + NSA_KVH * c + g))
    return pl.pallas_call(
        functools.partial(_cmp_kernel, nh=nh),
        grid_spec=pltpu.PrefetchScalarGridSpec(
            num_scalar_prefetch=1, grid=(b, 2, nt),
            in_specs=[xspec(0), xspec(1),
                      pl.BlockSpec((None, None, CMP_STRIDE * NSA_HD, 2 * NSA_HD), lambda bi, c, i, l: (l[0], c, 0, 0)),
                      pl.BlockSpec((None, SUBLANES, CMP_STRIDE * NSA_HD), lambda bi, c, i, l: (l[0], 0, 0)),
                      pl.BlockSpec((None, None, NSA_HD, NSA_HD), lambda bi, c, i, l: (l[0], c, 0, 0))],
            out_specs=pl.BlockSpec((None, None, NSA_KVH, nh, NSA_HD), lambda bi, c, i, l: (bi, c, 0, nt - 1 - i, 0)),
            scratch_shapes=[pltpu.VMEM((NSA_KVH, 1, NSA_HD), F32)]),
        out_shape=jax.ShapeDtypeStruct((b, 2, NSA_KVH, n_rows // CMP_STRIDE, NSA_HD), BF16),
        compiler_params=_cparams(("parallel", "parallel", "arbitrary")),
        name="nsa_compress",
    )(lidx, src3, src3, p["nsa_w1"], p["nsa_pe"], p["nsa_w2"])


def _masked_softmax(s, valid):
    s = jnp.where(valid, s, NEG)
    m = jnp.max(s, axis=-1, keepdims=True)
    e = jnp.where(valid, jnp.exp2(s - m), 0.0)
    den = jnp.sum(e, axis=-1, keepdims=True)
    return e / jnp.where(den > 0.0, den, 1.0)


def _attn_kernel(l_ref, q_ref, sm_ref, gn_ref, kc_ref, vc_ref, ks_ref, vs_ref, kw_ref, vw_ref, y_ref, *,
                 qb, pos0, ncp, ns, nsp, kt, wn, wpos0, tw):
    g = pl.program_id(1)
    q0 = pl.program_id(2) * qb
    qpos0 = pos0 + q0
    hd = NSA_HD
    q = q_ref[...] * (hd ** -0.5 * LOG2E)
    qr = jnp.concatenate([q[:, j * hd:(j + 1) * hd] for j in range(NSA_GQ)], axis=0).astype(BF16)
    slope_g = jnp.where(g == 0, LOG2E, LOG2E * 2.0 ** -NSA_GQ)
    slopes = [slope_g * 2.0 ** -(j + 1) for j in range(NSA_GQ)]
    heads = lambda a: [a[j * qb:(j + 1) * qb] for j in range(NSA_GQ)]

    dist = (qpos0 + _iota((qb, ncp), 0)) - (_iota((qb, ncp), 1) * CMP_STRIDE + (CMP_BLOCK - 1))
    valid = dist >= 0
    distf = dist.astype(F32)
    sc = heads(_dot_nt(qr, kc_ref[...]))
    pc = [_masked_softmax(sc[j] - slopes[j] * distf, valid) for j in range(NSA_GQ)]
    o_c = heads(_dot(jnp.concatenate(pc, axis=0).astype(BF16), vc_ref[...]))
    imp = pc[0] + pc[1] + pc[2] + pc[3]
    pool = jnp.where((_iota((ncp, nsp), 0) >> 2) == _iota((ncp, nsp), 1), 1.0, 0.0)
    imp = _dot_hi(imp, pool)

    blk = _iota((qb, nsp), 1)
    qp = qpos0 + _iota((qb, nsp), 0)
    val = jnp.where(blk * SEL_BLOCK > qp, -FORCE, imp)
    val = jnp.where(blk == (qp >> 6), FORCE, jnp.where(blk == 0, FORCE, val))
    val = jnp.where(blk >= ns, -3.0 * FORCE, val)
    top = float(min(TOP_N, ns))
    if qb == LANES and nsp == LANES:
        nsr = -(-ns // SUBLANES) * SUBLANES
        val_t = val.T[:nsr]
        blk_t = _iota((nsr, qb), 0)
        rank = jnp.zeros((nsr, qb), F32)
        for bidx in range(ns):
            cand = val_t[bidx:bidx + 1, :]
            rank = rank + jnp.where(cand > val_t, 1.0, jnp.where(cand == val_t, jnp.where(blk_t > bidx, 1.0, 0.0), 0.0))
        sel_t = jnp.where(rank < top, 1.0, 0.0)
        selb = jnp.concatenate([sel_t, jnp.zeros((nsp - nsr, qb), F32)], axis=0).T.astype(BF16)
    else:
        rank = jnp.zeros((qb, nsp), F32)
        for bidx in range(ns):
            cand = val[:, bidx:bidx + 1]
            rank = rank + jnp.where(cand > val, 1.0, jnp.where(cand == val, jnp.where(blk > bidx, 1.0, 0.0), 0.0))
        selb = jnp.where(rank < top, 1.0, 0.0).astype(BF16)

    n_tiles = (qpos0 + qb + kt - 1) // kt
    qh = heads(qr)
    spread = jnp.where(_iota((LANES, kt), 0) == (_iota((LANES, kt), 1) >> 6), 1.0, 0.0).astype(BF16)

    def tile_bias(t):
        k0 = t * kt
        d = (qpos0 + _iota((qb, kt), 0)) - (k0 + _iota((qb, kt), 1))
        pick_blk = jnp.where(_iota((nsp, LANES), 0) == (k0 >> 6) + _iota((nsp, LANES), 1), 1.0, 0.0).astype(BF16)
        sel_tile = _dot(selb, pick_blk).astype(BF16)
        ok = jnp.where(d >= 0, _dot(sel_tile, spread), 0.0) > 0.5
        return jnp.where(ok, d.astype(F32), -NEG)

    def tile(t, carry):
        k0 = pl.multiple_of(t * kt, kt)
        kk = ks_ref[pl.ds(k0, kt), :].astype(BF16)
        vv = vs_ref[pl.ds(k0, kt), :].astype(BF16)
        s = [_dot_nt(qh[j], kk) for j in range(NSA_GQ)]
        bias = carry[NSA_GQ]
        bias_next = tile_bias(t + 1)
        new = []
        for j in range(NSA_GQ):
            m, lsum, acc = carry[j]
            sm = s[j] - slopes[j] * bias
            m_new = jnp.maximum(m, jnp.max(sm, axis=-1, keepdims=True))
            pr = jnp.exp2(sm - m_new)
            alpha = jnp.exp2(m - m_new)
            new.append((m_new, alpha * lsum + jnp.sum(pr, axis=-1, keepdims=True),
                        alpha * acc + _dot(pr.astype(BF16), vv)))
        return tuple(new) + (bias_next,)

    init = (jnp.full((qb, 1), NEG, F32), jnp.zeros((qb, 1), F32), jnp.zeros((qb, hd), F32))
    fin = lax.fori_loop(0, n_tiles, tile, (init,) * NSA_GQ + (tile_bias(0),))
    o_s = [acc / lsum for _, lsum, acc in fin[:NSA_GQ]]

    k0w = pl.multiple_of(jnp.clip(q0 - WINDOW, 0, tw - wn), SUBLANES)
    dw = (qpos0 + _iota((qb, wn), 0)) - (wpos0 + k0w + _iota((qb, wn), 1))
    okw = jnp.abs(2 * dw - (WINDOW - 1)) < WINDOW
    dwf = dw.astype(F32)
    sw = heads(_dot_nt(qr, kw_ref[pl.ds(k0w, wn), :].astype(BF16)))
    pw = [_masked_softmax(sw[j] - slopes[j] * dwf, okw) for j in range(NSA_GQ)]
    o_w = heads(_dot(jnp.concatenate(pw, axis=0).astype(BF16), vw_ref[pl.ds(k0w, wn), :].astype(BF16)))

    gate = _sigmoid(sm_ref[...])
    lane = _iota((qb, LANES), 1)
    pick = lambda idx: jnp.sum(jnp.where(lane == idx, gate, 0.0), axis=-1, keepdims=True)
    for j in range(NSA_GQ):
        base = SM_GL + 3 * (g * NSA_GQ + j)
        o = pick(base) * o_c[j] + pick(base + 1) * o_s[j] + pick(base + 2) * o_w[j]
        y_ref[:, j * hd:(j + 1) * hd] = (o * _silu(gn_ref[:, j * hd:(j + 1) * hd])).astype(y_ref.dtype)


def _attention(lidx, z3, cmp_kv, ks_src, ks_col, vs_col, kw_src, kw_col, vw_col, pos0, wpos0, ydtype):
    b, t, _ = z3.shape
    qb = min(Q_BLOCK, t)
    ncp = cmp_kv.shape[3]
    tk = ks_src.shape[1]
    tw = kw_src.shape[1]
    if qb == Q_BLOCK:
        kt = 512
    else:
        kt = max(k for k in range(512, min(tk, LANES * SEL_BLOCK) + 1, 512) if tk % k == 0)
    ns = -(-(pos0 + t) // SEL_BLOCK)
    nsp = -(-ns // LANES) * LANES
    wn = WINDOW + Q_BLOCK
    assert tk % kt == 0 and tk >= pos0 + t and tw >= wn and ncp // CMP_PER_SEL <= nsp
    hw = NSA_GQ * NSA_HD
    kern = functools.partial(_attn_kernel, qb=qb, pos0=pos0, ncp=ncp, ns=ns, nsp=nsp, kt=kt, wn=wn, wpos0=wpos0, tw=tw)
    kvspec = lambda rows, col: pl.BlockSpec((None, rows, NSA_HD), lambda bi, g, i, l: (bi, 0, col + g))
    cspec = lambda c: pl.BlockSpec((None, None, None, ncp, NSA_HD), lambda bi, g, i, l: (bi, c, g, 0, 0))
    return pl.pallas_call(
        kern,
        grid_spec=pltpu.PrefetchScalarGridSpec(
            num_scalar_prefetch=1, grid=(b, NSA_KVH, t // qb),
            in_specs=[pl.BlockSpec((None, qb, hw), lambda bi, g, i, l: (bi, i, C_QNSA // hw + g)),
                      pl.BlockSpec((None, qb, LANES), lambda bi, g, i, l: (bi, i, C_SMALL // LANES)),
                      pl.BlockSpec((None, qb, hw), lambda bi, g, i, l: (bi, i, C_GNSA // hw + g)),
                      cspec(0), cspec(1),
                      kvspec(tk, ks_col), kvspec(tk, vs_col), kvspec(tw, kw_col), kvspec(tw, vw_col)],
            out_specs=pl.BlockSpec((None, qb, hw), lambda bi, g, i, l: (bi, i, g))),
        out_shape=jax.ShapeDtypeStruct((b, t, W_BR), ydtype),
        compiler_params=_cparams(("parallel", "parallel", "arbitrary")),
        name="nsa_attention",
    )(lidx, z3, z3, z3, cmp_kv, cmp_kv, ks_src, ks_src, kw_src, kw_src)


def _merge_kernel(l_ref, y0_ref, y1_ref, y2_ref, y3_ref, g0_ref, g1_ref, g2_ref, g3_ref, w_ref, o_ref):
    acc = None
    for m, (y_ref, g_ref) in enumerate(((y0_ref, g0_ref), (y1_ref, g1_ref), (y2_ref, g2_ref), (y3_ref, g3_ref))):
        term = _sigmoid(g_ref[...]) * _dot(y_ref[...].astype(BF16), w_ref[m])
        acc = term if acc is None else acc + term
    o_ref[...] = acc.astype(o_ref.dtype)


def _merge(lidx, ys, z2d, p):
    n = z2d.shape[0]
    tm = min(n, 256)
    yspec = pl.BlockSpec((tm, W_BR), lambda i, l: (i, 0))
    gspec = lambda m: pl.BlockSpec((tm, D_MODEL), lambda i, l: (i, C_MERGE // D_MODEL + m))
    wspec = pl.BlockSpec((None, N_BRANCH, W_BR, D_MODEL), lambda i, l: (l[0], 0, 0, 0), pipeline_mode=pl.Buffered(1))
    return pl.pallas_call(
        _merge_kernel,
        grid_spec=pltpu.PrefetchScalarGridSpec(
            num_scalar_prefetch=1, grid=(n // tm,),
            in_specs=[yspec] * N_BRANCH + [gspec(m) for m in range(N_BRANCH)] + [wspec],
            out_specs=pl.BlockSpec((tm, D_MODEL), lambda i, l: (i, 0))),
        out_shape=jax.ShapeDtypeStruct((n, D_MODEL), BF16),
        compiler_params=_cparams(("parallel",)),
        name="merge",
    )(lidx, *ys, z2d, z2d, z2d, z2d, p["w_branch"])


def _outproj_kernel(l_ref, m_ref, x_ref, w_ref, g_ref, b_ref, o_ref, *, alpha):
    v = alpha * x_ref[...] + _dot(m_ref[...], w_ref[...])
    mu = jnp.mean(v, axis=-1, keepdims=True)
    c = v - mu
    var = jnp.mean(c * c, axis=-1, keepdims=True)
    o_ref[...] = c * lax.rsqrt(var + 1e-5) * g_ref[...] + b_ref[...]


def _outproj(lidx, merged, x2d, p, alpha):
    n = x2d.shape[0]
    tm = min(n, 256)
    return pl.pallas_call(
        functools.partial(_outproj_kernel, alpha=alpha),
        grid_spec=pltpu.PrefetchScalarGridSpec(
            num_scalar_prefetch=1, grid=(n // tm,),
            in_specs=[pl.BlockSpec((tm, D_MODEL), lambda i, l: (i, 0)),
                      pl.BlockSpec((tm, D_MODEL), lambda i, l: (i, 0)),
                      pl.BlockSpec((None, D_MODEL, D_MODEL), lambda i, l: (l[0], 0, 0)),
                      pl.BlockSpec((None, 1, D_MODEL), lambda i, l: (l[0], 0, 0)),
                      pl.BlockSpec((None, 1, D_MODEL), lambda i, l: (l[0], 0, 0))],
            out_specs=pl.BlockSpec((tm, D_MODEL), lambda i, l: (i, 0))),
        out_shape=jax.ShapeDtypeStruct((n, D_MODEL), F32),
        compiler_params=_cparams(("parallel",)),
        name="outproj_ln",
    )(lidx, merged, x2d, p["w_out"], p["ln_g"], p["ln_b"])


def _prep_params(w):
    p = {}
    row = lambda a: a[:, None, :]
    p["lru_conv_w"] = w["lru_conv_w"]
    p["lru_conv_b"] = row(w["lru_conv_b"])
    p["lru_wa"] = w["lru_wa"].astype(BF16)
    p["lru_ba"] = row(w["lru_ba"])
    p["lru_wx"] = w["lru_wx"].astype(BF16)
    p["lru_bx"] = row(w["lru_bx"])
    p["lru_lambda"] = row(w["lru_lambda"])
    p.update(_s5_weights(w))
    p["w_branch"] = w["w_branch"].astype(BF16)
    p["w_out"] = w["w_out"].astype(BF16)
    p["ln_g"] = row(w["ln_g"])
    p["ln_b"] = row(w["ln_b"])
    dd = w["gdn_a_log"].shape[0]
    lane_row = lambda a, off: jnp.zeros((dd, 1, LANES), F32).at[:, 0, off:off + a.shape[-1]].set(a)
    p["gdn_conv_w"] = w["gdn_conv_w"]
    p["gdn_a_log"] = lane_row(w["gdn_a_log"], SM_A)
    p["gdn_dt_bias"] = lane_row(w["gdn_dt_bias"], SM_A)
    p["gdn_norm_w"] = row(w["gdn_norm_w"])
    flat = CMP_STRIDE * NSA_HD
    w1 = w["nsa_cmp_w1"].reshape(dd, 2, 2, flat, NSA_HD).transpose(0, 1, 3, 2, 4)
    p["nsa_w1"] = w1.reshape(dd, 2, flat, 2 * NSA_HD).astype(BF16)
    pe = w["nsa_cmp_pos"].reshape(dd, 2, flat)
    p["nsa_pe"] = jnp.concatenate([pe, jnp.zeros((dd, SUBLANES - 2, flat), F32)], axis=1)
    p["nsa_w2"] = w["nsa_cmp_w2"].astype(BF16)
    return p


def _mixer_layer(lidx, x2d, s, t, state, nsa_branch, p, wp, alpha, ydtype):
    lru_buf, lru_h, gdn_buf, gdn_s, s5_re, s5_im = state
    z = _inproj(lidx, x2d, wp)
    z3 = z.reshape(s, t, NP)
    y_lru, lru_buf, lru_h = _lru(lidx, z3, lru_buf, lru_h, p, ydtype)
    y_nsa = nsa_branch(z3)
    y_gdn, gdn_buf, gdn_s = _gdn(lidx, z3, gdn_buf, gdn_s, p, ydtype, GDN_CHUNK)
    y_s5, s5_re, s5_im = _s5_scan(lidx, z3, s5_re, s5_im, p, s5_re.shape[1])
    y_s5 = _s5_glu(lidx, y_s5, z, p, ydtype)
    flat = lambda y: y.reshape(s * t, W_BR)
    merged = _merge(lidx, (flat(y_lru), flat(y_nsa), flat(y_gdn), y_s5), z, p)
    x_new = _outproj(lidx, merged, x2d, p, alpha)
    return x_new, z3, (lru_buf, lru_h, gdn_buf, gdn_s, s5_re, s5_im)


def kernel(x_prompt, x_sample, state_lru_h, state_lru_conv, cache_nsa_kv, cache_win_kv, state_gdn_s, state_gdn_conv, state_s5_re, state_s5_im, page_table, w_in, lru_conv_w, lru_conv_b, lru_wa, lru_ba, lru_wx, lru_bx, lru_lambda, nsa_cmp_pos, nsa_cmp_w1, nsa_cmp_w2, gdn_conv_w, gdn_a_log, gdn_dt_bias, gdn_norm_w, s5_lam_re, s5_lam_im, s5_log_dt, s5_b_re, s5_b_im, s5_c_re, s5_c_im, s5_d, s5_glu_w, w_branch, w_out, ln_g, ln_b):
    depth = w_in.shape[0]
    bp, tp, _ = x_prompt.shape
    db, ts, _ = x_sample.shape
    n_pages = page_table.shape[1]
    past = n_pages * PAGE_SIZE
    wbuf = cache_win_kv.shape[2]
    alpha = (2.0 * depth) ** 0.25
    kvw = 4 * NSA_KVH * NSA_HD
    winw = 2 * NSA_KVH * NSA_HD
    p = _prep_params(dict(
        lru_conv_w=lru_conv_w, lru_conv_b=lru_conv_b, lru_wa=lru_wa, lru_ba=lru_ba, lru_wx=lru_wx, lru_bx=lru_bx,
        lru_lambda=lru_lambda, nsa_cmp_pos=nsa_cmp_pos, nsa_cmp_w1=nsa_cmp_w1, nsa_cmp_w2=nsa_cmp_w2,
        gdn_conv_w=gdn_conv_w, gdn_a_log=gdn_a_log, gdn_dt_bias=gdn_dt_bias, gdn_norm_w=gdn_norm_w,
        s5_lam_re=s5_lam_re, s5_lam_im=s5_lam_im, s5_log_dt=s5_log_dt, s5_b_re=s5_b_re, s5_b_im=s5_b_im,
        s5_c_re=s5_c_re, s5_c_im=s5_c_im, s5_d=s5_d, s5_glu_w=s5_glu_w, w_branch=w_branch, w_out=w_out,
        ln_g=ln_g, ln_b=ln_b))
    wp = _pack_w_in(w_in)
    p, wp = lax.optimization_barrier((p, wp))
    pt_flat = page_table.reshape(-1).astype(jnp.int32)
    sb = db if ts == S5_L else 1
    c0 = C_KV // LANES
    zero_state = (jnp.zeros((bp, CONV_K - 1, W_BR), F32), jnp.zeros((bp, 1, W_BR), F32),
                  jnp.zeros((bp, CONV_K - 1, 3 * W_BR), F32), jnp.zeros((bp, GDN_H, GDN_HD, GDN_HD), F32),
                  jnp.zeros((bp, 1, S5_G * S5_P), F32), jnp.zeros((bp, 1, S5_G * S5_P), F32))

    def layer(carry, l):
        xp, xs = carry
        lidx = l.reshape(1)
        at = lambda a: lax.dynamic_index_in_dim(a, l, 0, keepdims=False)

        def nsa_prompt(z3):
            ckv = _compress(lidx, z3, c0, tp, p)
            return _attention(lidx, z3, ckv, z3, c0 + 4, c0 + 6, z3, c0 + 8, c0 + 10, 0, 0, BF16)

        xp, zp3, st_p = _mixer_layer(lidx, xp, bp, tp, zero_state, nsa_prompt, p, wp, alpha, BF16)

        win_state = {}

        def nsa_sample(z3):
            cmp_rows, sel_rows = _gather_ctx(lidx, pt_flat, cache_nsa_kv, z3)
            ckv = _compress(lidx, cmp_rows, 0, past, p)
            win = jnp.concatenate([at(cache_win_kv).reshape(db, wbuf, winw),
                                   z3[:, :, C_KV + kvw:C_KV + kvw + winw]], axis=1)
            win_state["win"] = win
            pad = max(0, WINDOW + Q_BLOCK - (wbuf + ts))
            win_pad = jnp.concatenate([win, jnp.zeros((db, pad, winw), F32)], axis=1)
            return _attention(lidx, z3, ckv, sel_rows, 0, 2, win_pad, 0, 2, past, past - wbuf, F32)

        st_in = (at(state_lru_conv), at(state_lru_h)[:, None, :], at(state_gdn_conv), at(state_gdn_s),
                 at(state_s5_re).reshape(db // sb, sb, S5_G * S5_P), at(state_s5_im).reshape(db // sb, sb, S5_G * S5_P))
        xs, zs3, st_s = _mixer_layer(lidx, xs, db, ts, st_in, nsa_sample, p, wp, alpha, F32)

        def outs(st, z3, s, t, win):
            lru_buf, lru_h, gdn_buf, gdn_s, s5_re, s5_im = st
            return (lru_h.reshape(s, W_BR), lru_buf,
                    z3[:, :, C_KV:C_KV + kvw].reshape(s, t, 4, NSA_KVH, NSA_HD),
                    win.reshape(s, win.shape[1], 2, NSA_KVH, NSA_HD),
                    gdn_s, gdn_buf, s5_re.reshape(s, S5_G, S5_P), s5_im.reshape(s, S5_G, S5_P))

        win_p = zp3[:, tp - min(WINDOW, tp):, C_KV + kvw:C_KV + kvw + winw]
        win_s = win_state["win"][:, wbuf + ts - min(WINDOW, past + ts):]
        return (xp, xs), (outs(st_p, zp3, bp, tp, win_p), outs(st_s, zs3, db, ts, win_s))

    (xp, xs), (op, os_) = lax.scan(layer, (x_prompt.reshape(bp * tp, D_MODEL), x_sample.reshape(db * ts, D_MODEL)),
                                   jnp.arange(depth, dtype=jnp.int32))
    return (xp.reshape(bp, tp, D_MODEL), xs.reshape(db, ts, D_MODEL)) + tuple(op) + tuple(os_)
```

```python
import functools

import jax
import jax.numpy as jnp
from jax import lax
from jax.experimental import pallas as pl
from jax.experimental.pallas import tpu as pltpu

F32 = jnp.float32
BF16 = jnp.bfloat16
HI = lax.Precision.HIGHEST

D_MODEL = 2048
W_BR = D_MODEL // 2
N_BRANCH = 4
CONV_K = 4
LRU_BLOCKS = 8
LRU_BS = W_BR // LRU_BLOCKS
LRU_C = 8.0
NSA_HD = 128
NSA_H = 8
NSA_KVH = 2
NSA_GQ = NSA_H // NSA_KVH
CMP_STRIDE = 16
CMP_BLOCK = 32
SEL_BLOCK = 64
CMP_PER_SEL = SEL_BLOCK // CMP_STRIDE
TOP_N = 16
WINDOW = 512
Q_BLOCK = 128
PAGE_SIZE = 128
GDN_HD = 128
GDN_H = 8
GDN_CHUNK = 64
GDN_SPLIT_LEVELS = 2
S5_GS = 16
S5_G = W_BR // S5_GS
S5_P = 64
S5_L = 8
NEG = -1e30
FORCE = 1e9
LOG2E = 1.4426950408889634

LANES = 128
SUBLANES = 8
VMEM_LIMIT = 56 * 1024 * 1024

IN_TILE = 512
C_MERGE = 0
C_ULRU = 8192
C_GLRU = 9216
C_QNSA = 10240
C_GNSA = 11264
C_GGDN = 12288
C_US5 = 13312
C_GS5 = 14336
C_QKV = 15360
C_KV = 18432
C_SMALL_A = 19968
C_SMALL_B = 20480
NP = C_SMALL_B + IN_TILE
IN_SEGMENTS = (("u_lru", W_BR, C_ULRU), ("g_lru", W_BR, C_GLRU), ("q_nsa", W_BR, C_QNSA),
               ("kv", 6 * NSA_KVH * NSA_HD, C_KV), ("gl", 3 * NSA_H, None), ("g_nsa", W_BR, C_GNSA),
               ("qkv", 3 * W_BR, C_QKV), ("a", GDN_H, None), ("b", GDN_H, None),
               ("g_gdn", W_BR, C_GGDN), ("u_s5", W_BR, C_US5), ("g_s5", W_BR, C_GS5),
               ("merge", N_BRANCH * D_MODEL, C_MERGE))


def _segment_start(name):
    return sum(w for n, w, _ in IN_SEGMENTS[:[s[0] for s in IN_SEGMENTS].index(name)])


SRC_SMALL_A = _segment_start("gl") // LANES * LANES
SRC_SMALL_B = _segment_start("a") // LANES * LANES
SM_GL = _segment_start("gl") - SRC_SMALL_A
SM_A = _segment_start("a") - SRC_SMALL_B
SM_B = _segment_start("b") - SRC_SMALL_B


def _tile_sources():
    src = [None] * (NP // IN_TILE)
    at = 0
    for _, width, dst in IN_SEGMENTS:
        if dst is not None:
            for c in range(0, width, IN_TILE):
                src[(dst + c) // IN_TILE] = at + c
        at += width
    src[C_SMALL_A // IN_TILE] = SRC_SMALL_A
    src[C_SMALL_B // IN_TILE] = SRC_SMALL_B
    assert all(s is not None and s % SUBLANES == 0 and s + IN_TILE <= at for s in src)
    return src


def _cparams(sem):
    return pltpu.CompilerParams(dimension_semantics=sem, vmem_limit_bytes=VMEM_LIMIT)


def _sigmoid(x):
    return 1.0 / (1.0 + jnp.exp(-x))


def _silu(x):
    return x * _sigmoid(x)


def _softplus(x):
    return jnp.maximum(x, 0.0) + jnp.log1p(jnp.exp(-jnp.abs(x)))


def _dot(a, b):
    return jnp.dot(a, b, preferred_element_type=F32)


def _dot_hi(a, b):
    return jnp.dot(a, b, preferred_element_type=F32, precision=HI)


def _dot_nt(a, b):
    return lax.dot_general(a, b, (((1,), (1,)), ((), ())), preferred_element_type=F32)


def _dot_tn(a, b):
    return lax.dot_general(a, b, (((0,), (0,)), ((), ())), preferred_element_type=F32)


def _iota(shape, axis):
    return lax.broadcasted_iota(jnp.int32, shape, axis)


def _inproj_kernel(l_ref, src_ref, x_ref, w_ref, o_ref, xb_ref):
    @pl.when(pl.program_id(1) == 0)
    def _():
        xb_ref[...] = x_ref[...].astype(BF16)

    o_ref[...] = _dot_nt(xb_ref[...], w_ref[...].astype(BF16))


def _inproj(lidx, srcs, x2d, wt):
    n = x2d.shape[0]
    tm = min(n, 1024)
    wspec = pl.BlockSpec((pl.Squeezed(), pl.Element(IN_TILE), pl.Element(D_MODEL)),
                         lambda i, j, l, src: (l[0], pl.multiple_of(src[j], SUBLANES), 0))
    return pl.pallas_call(
        _inproj_kernel,
        grid_spec=pltpu.PrefetchScalarGridSpec(
            num_scalar_prefetch=2, grid=(n // tm, NP // IN_TILE),
            in_specs=[pl.BlockSpec((tm, D_MODEL), lambda i, j, l, src: (i, 0)), wspec],
            out_specs=pl.BlockSpec((tm, IN_TILE), lambda i, j, l, src: (i, j)),
            scratch_shapes=[pltpu.VMEM((tm, D_MODEL), BF16)]),
        out_shape=jax.ShapeDtypeStruct((n, NP), F32),
        compiler_params=_cparams(("parallel", "arbitrary")),
        name="inproj",
    )(lidx, srcs, x2d, wt)


def _lru_kernel(l_ref, u_ref, g_ref, buf_ref, h0_ref, cw_ref, cb_ref, wa_ref, ba_ref, wx_ref, bx_ref, lam_ref,
                y_ref, bufo_ref, ho_ref, xp_scr, a_scr, b_scr, h_scr, *, tt, nt):
    ti = pl.program_id(1)

    @pl.when(ti == 0)
    def _():
        xp_scr[5:8, :] = buf_ref[...]
        h_scr[...] = h0_ref[...]

    u = u_ref[...]
    xp_scr[8:8 + tt, :] = u
    cw = cw_ref[...]
    xc = (cb_ref[...] + cw[3:4] * u + cw[2:3] * xp_scr[7:7 + tt, :]
          + cw[1:2] * xp_scr[6:6 + tt, :] + cw[0:1] * xp_scr[5:5 + tt, :])
    tail = u[tt - 3:tt, :]
    xp_scr[5:8, :] = tail
    sp = _softplus(-lam_ref[...])
    for n in range(LRU_BLOCKS):
        sl = slice(n * LRU_BS, (n + 1) * LRU_BS)
        xn = xc[:, sl]
        xb = xn.astype(BF16)
        r = _sigmoid(_dot(xb, wa_ref[n]) + ba_ref[:, sl])
        i = _sigmoid(_dot(xb, wx_ref[n]) + bx_ref[:, sl])
        a = jnp.exp(-LRU_C * r * sp[:, sl])
        a_scr[:, sl] = a
        b_scr[:, sl] = jnp.sqrt(1.0 - a * a) * (i * xn)

    def body(i, h):
        for k in range(SUBLANES):
            t = i * SUBLANES + k
            h = a_scr[pl.ds(t, 1), :] * h + b_scr[pl.ds(t, 1), :]
            b_scr[pl.ds(t, 1), :] = h
        return h

    h = lax.fori_loop(0, tt // SUBLANES, body, h_scr[...])
    h_scr[...] = h
    y_ref[...] = (b_scr[...] * _silu(g_ref[...])).astype(y_ref.dtype)

    @pl.when(ti == nt - 1)
    def _():
        bufo_ref[...] = tail
        ho_ref[...] = h


def _lru(lidx, z3, buf, h0, p, ydtype):
    s, t, _ = z3.shape
    tt = min(t, 512)
    nt = t // tt
    wspec = lambda shape: pl.BlockSpec((None,) + shape, lambda b, i, l: (l[0],) + (0,) * len(shape))
    return pl.pallas_call(
        functools.partial(_lru_kernel, tt=tt, nt=nt),
        grid_spec=pltpu.PrefetchScalarGridSpec(
            num_scalar_prefetch=1, grid=(s, nt),
            in_specs=[pl.BlockSpec((None, tt, W_BR), lambda b, i, l: (b, i, C_ULRU // W_BR)),
                      pl.BlockSpec((None, tt, W_BR), lambda b, i, l: (b, i, C_GLRU // W_BR)),
                      pl.BlockSpec((None, 3, W_BR), lambda b, i, l: (b, 0, 0)),
                      pl.BlockSpec((None, 1, W_BR), lambda b, i, l: (b, 0, 0)),
                      wspec((CONV_K, W_BR)), wspec((1, W_BR)),
                      wspec((LRU_BLOCKS, LRU_BS, LRU_BS)), wspec((1, W_BR)),
                      wspec((LRU_BLOCKS, LRU_BS, LRU_BS)), wspec((1, W_BR)), wspec((1, W_BR))],
            out_specs=[pl.BlockSpec((None, tt, W_BR), lambda b, i, l: (b, i, 0)),
                       pl.BlockSpec((None, 3, W_BR), lambda b, i, l: (b, 0, 0)),
                       pl.BlockSpec((None, 1, W_BR), lambda b, i, l: (b, 0, 0))],
            scratch_shapes=[pltpu.VMEM((tt + 8, W_BR), F32), pltpu.VMEM((tt, W_BR), F32),
                            pltpu.VMEM((tt, W_BR), F32), pltpu.VMEM((1, W_BR), F32)]),
        out_shape=[jax.ShapeDtypeStruct((s, t, W_BR), ydtype),
                   jax.ShapeDtypeStruct((s, 3, W_BR), F32),
                   jax.ShapeDtypeStruct((s, 1, W_BR), F32)],
        compiler_params=_cparams(("parallel", "arbitrary")),
        name="rglru",
    )(lidx, z3, z3, buf, h0, p["lru_conv_w"], p["lru_conv_b"], p["lru_wa"], p["lru_ba"],
      p["lru_wx"], p["lru_bx"], p["lru_lambda"])


S5_CB = LANES // S5_GS
S5_SW = S5_CB * S5_P


def _gelu_tanh(x):
    return 0.5 * x * (1.0 + jnp.tanh(0.7978845608028654 * (x + 0.044715 * (x * x * x))))


def _s5_kernel(l_ref, u_ref, h0re_ref, h0im_ref, wst_ref, vout_ref, kt_ref, alre_ref, alim_ref, d_ref,
               y_ref, hre_ref, him_ref, hin_scr, s_scr, *, n, sb):
    rows = sb * n
    us = [u_ref[pl.ds(j, rows, stride=S5_L), :] for j in range(S5_L)]
    ub = jnp.concatenate(us, axis=1).astype(BF16)
    s = _dot(ub, wst_ref[...])
    alre = alre_ref[...]
    alim = alim_ref[...]
    h0re = h0re_ref[...]
    h0im = h0im_ref[...]
    if n == 1:
        hin_scr[:, :S5_SW] = h0re
        hin_scr[:, S5_SW:] = h0im
        hre = alre * h0re - alim * h0im + s[:, :S5_SW]
        him = alre * h0im + alim * h0re + s[:, S5_SW:]
    else:
        s_scr[...] = s

        def body(c, carry):
            hre, him = carry
            hin_scr[pl.ds(c, 1), :S5_SW] = hre
            hin_scr[pl.ds(c, 1), S5_SW:] = him
            srow = s_scr[pl.ds(c, 1), :]
            return (alre * hre - alim * him + srow[:, :S5_SW],
                    alre * him + alim * hre + srow[:, S5_SW:])

        hre, him = lax.fori_loop(0, n, body, (h0re, h0im))
    hre_ref[...] = hre
    him_ref[...] = him
    ycat = _dot(hin_scr[...].astype(BF16), vout_ref[...]) + _dot(ub, kt_ref[...])
    d = d_ref[...]
    for j in range(S5_L):
        yj = ycat[:, j * LANES:(j + 1) * LANES] + d * us[j]
        y_ref[pl.ds(j, rows, stride=S5_L), :] = _gelu_tanh(yj)


def _s5_scan(lidx, z3, h0re, h0im, p, sb):
    s, t, _ = z3.shape
    n = t // S5_L
    assert sb == 1 or n == 1
    sg = s // sb
    zr = z3.reshape(sg, sb * t, NP)
    ncb = W_BR // LANES
    wspec = lambda shape: pl.BlockSpec((None, None) + shape, lambda b, c, l: (l[0], c) + (0,) * len(shape))
    hspec = pl.BlockSpec((None, sb, S5_SW), lambda b, c, l: (b, 0, c))
    y, hre, him = pl.pallas_call(
        functools.partial(_s5_kernel, n=n, sb=sb),
        grid_spec=pltpu.PrefetchScalarGridSpec(
            num_scalar_prefetch=1, grid=(sg, ncb),
            in_specs=[pl.BlockSpec((None, sb * t, LANES), lambda b, c, l: (b, 0, C_US5 // LANES + c)),
                      hspec, hspec,
                      wspec((S5_L * LANES, 2 * S5_SW)), wspec((2 * S5_SW, S5_L * LANES)),
                      wspec((S5_L * LANES, S5_L * LANES)), wspec((1, S5_SW)), wspec((1, S5_SW)),
                      wspec((1, LANES))],
            out_specs=[pl.BlockSpec((None, sb * t, LANES), lambda b, c, l: (b, 0, c)), hspec, hspec],
            scratch_shapes=[pltpu.VMEM((sb * n, 2 * S5_SW), F32), pltpu.VMEM((sb * n, 2 * S5_SW), F32)]),
        out_shape=[jax.ShapeDtypeStruct((sg, sb * t, W_BR), F32),
                   jax.ShapeDtypeStruct((sg, sb, S5_G * S5_P), F32),
                   jax.ShapeDtypeStruct((sg, sb, S5_G * S5_P), F32)],
        compiler_params=_cparams(("parallel", "arbitrary")),
        name="s5_scan",
    )(lidx, zr, h0re, h0im, p["s5_wst"], p["s5_vout"], p["s5_kt"], p["s5_alre"], p["s5_alim"], p["s5_d"])
    return y.reshape(s * t, W_BR), hre, him


def _s5_glu_kernel(l_ref, y_ref, g_ref, w_ref, o_ref):
    gl = _dot(y_ref[...].astype(BF16), w_ref[...])
    o_ref[...] = (gl[:, :W_BR] * _sigmoid(gl[:, W_BR:]) * _silu(g_ref[...])).astype(o_ref.dtype)


def _s5_glu(lidx, y2d, z2d, p, ydtype):
    n = y2d.shape[0]
    tm = min(n, 512)
    return pl.pallas_call(
        _s5_glu_kernel,
        grid_spec=pltpu.PrefetchScalarGridSpec(
            num_scalar_prefetch=1, grid=(n // tm,),
            in_specs=[pl.BlockSpec((tm, W_BR), lambda i, l: (i, 0)),
                      pl.BlockSpec((tm, W_BR), lambda i, l: (i, C_GS5 // W_BR)),
                      pl.BlockSpec((None, W_BR, 2 * W_BR), lambda i, l: (l[0], 0, 0))],
            out_specs=pl.BlockSpec((tm, W_BR), lambda i, l: (i, 0))),
        out_shape=jax.ShapeDtypeStruct((n, W_BR), ydtype),
        compiler_params=_cparams(("parallel",)),
        name="s5_glu",
    )(lidx, y2d, z2d, p["s5_glu_w"])


def _s5_expand_kernel(wst_ref, vout_ref, kt_ref, wst_o, vout_o, kt_o):
    n = S5_L * LANES
    lp = S5_P.bit_length() - 1
    lc = S5_GS.bit_length() - 1
    lg = S5_CB.bit_length() - 1
    row = _iota((n, n), 0)
    col = _iota((n, n), 1)
    src = _iota((LANES, n), 0)
    dst = _iota((LANES, n), 1)
    rep_state = jnp.where(src == (((dst >> (lp + lg)) << lp) | (dst & (S5_P - 1))), 1.0, 0.0).astype(BF16)
    rep_out = jnp.where(src == (((dst >> (lc + lg)) << lc) | (dst & (S5_GS - 1))), 1.0, 0.0).astype(BF16)
    gi_in_row = (row >> lc) & (S5_CB - 1)
    gi_state_row = (row >> lp) & (S5_CB - 1)
    gi_state_col = (col >> lp) & (S5_CB - 1)
    gi_out_col = (col >> lc) & (S5_CB - 1)
    wst_o[...] = jnp.where(gi_in_row == gi_state_col, _dot(wst_ref[...].astype(BF16), rep_state), 0.0).astype(BF16)
    vout_o[...] = jnp.where(gi_state_row == gi_out_col, _dot(vout_ref[...].astype(BF16), rep_out), 0.0).astype(BF16)
    kt_o[...] = jnp.where(gi_in_row == gi_out_col, _dot(kt_ref[...].astype(BF16), rep_out), 0.0).astype(BF16)


def _s5_expand(wst, vout, kt):
    dd, ncb, n, _ = wst.shape
    cspec = pl.BlockSpec((None, None, n, LANES), lambda d, c: (d, c, 0, 0))
    ospec = pl.BlockSpec((None, None, n, n), lambda d, c: (d, c, 0, 0))
    oshape = jax.ShapeDtypeStruct((dd, ncb, n, n), BF16)
    return pl.pallas_call(
        _s5_expand_kernel, grid=(dd, ncb), in_specs=[cspec, cspec, cspec], out_specs=[ospec, ospec, ospec],
        out_shape=[oshape, oshape, oshape], compiler_params=_cparams(("parallel", "parallel")),
        name="s5_expand",
    )(wst, vout, kt)


def _s5_weights(w):
    dt = jnp.exp(w["s5_log_dt"])[..., None]
    lr, li = w["s5_lam_re"], w["s5_lam_im"]
    mag = jnp.exp(lr * dt)
    a_re = mag * jnp.cos(li * dt)
    a_im = mag * jnp.sin(li * dt)
    den = lr * lr + li * li
    f_re = ((a_re - 1.0) * lr + a_im * li) / den
    f_im = (a_im * lr - (a_re - 1.0) * li) / den
    bb_re = f_re[..., None] * w["s5_b_re"] - f_im[..., None] * w["s5_b_im"]
    bb_im = f_re[..., None] * w["s5_b_im"] + f_im[..., None] * w["s5_b_re"]
    pw_re = [jnp.ones_like(a_re)]
    pw_im = [jnp.zeros_like(a_im)]
    for _ in range(S5_L):
        pr, pi = pw_re[-1], pw_im[-1]
        pw_re.append(pr * a_re - pi * a_im)
        pw_im.append(pr * a_im + pi * a_re)
    pw_re = jnp.stack(pw_re, axis=1)
    pw_im = jnp.stack(pw_im, axis=1)
    dd = lr.shape[0]
    ncb = S5_G // S5_CB
    grp = lambda a: a.reshape(a.shape[0], a.shape[1], ncb, S5_CB, *a.shape[3:])
    rev_re = jnp.stack([pw_re[:, S5_L - 1 - j] for j in range(S5_L)], axis=1)
    rev_im = jnp.stack([pw_im[:, S5_L - 1 - j] for j in range(S5_L)], axis=1)
    st_re = rev_re[..., None] * bb_re[:, None] - rev_im[..., None] * bb_im[:, None]
    st_im = rev_re[..., None] * bb_im[:, None] + rev_im[..., None] * bb_re[:, None]

    def state_w(a):
        return grp(a).transpose(0, 2, 1, 3, 5, 4).reshape(dd, ncb, S5_L * LANES, S5_P)

    wst = jnp.concatenate([state_w(st_re), state_w(st_im)], axis=-1)
    c_re, c_im = w["s5_c_re"], w["s5_c_im"]
    nx_re, nx_im = pw_re[:, 1:], pw_im[:, 1:]
    ca_re = c_re[:, None] * nx_re[:, :, :, None] - c_im[:, None] * nx_im[:, :, :, None]
    ca_im = c_re[:, None] * nx_im[:, :, :, None] + c_im[:, None] * nx_re[:, :, :, None]

    def out_w(a):
        return grp(a).transpose(0, 2, 3, 5, 1, 4).reshape(dd, ncb, S5_SW, S5_L * S5_GS)

    vout = jnp.concatenate([out_w(ca_re), out_w(-ca_im)], axis=2)
    cat_re = c_re[:, None] * pw_re[:, :S5_L, :, None] - c_im[:, None] * pw_im[:, :S5_L, :, None]
    cat_im = c_re[:, None] * pw_im[:, :S5_L, :, None] + c_im[:, None] * pw_re[:, :S5_L, :, None]
    kk = (jnp.einsum("dlgcp,dgpe->dlgce", cat_re, bb_re, precision=HI)
          - jnp.einsum("dlgcp,dgpe->dlgce", cat_im, bb_im, precision=HI))
    zero = jnp.zeros_like(kk[:, 0])
    kt = jnp.stack([jnp.stack([kk[:, j - i] if j >= i else zero for j in range(S5_L)], axis=1)
                    for i in range(S5_L)], axis=1)
    kt = kt.reshape(dd, S5_L, S5_L, ncb, S5_CB, S5_GS, S5_GS)
    kt = kt.transpose(0, 3, 1, 4, 6, 2, 5).reshape(dd, ncb, S5_L * LANES, S5_L * S5_GS)
    wst, vout, kt = _s5_expand(wst, vout, kt)
    sw = lambda a: a.reshape(dd, ncb, 1, S5_SW)
    return dict(s5_wst=wst, s5_vout=vout, s5_kt=kt, s5_alre=sw(pw_re[:, S5_L]), s5_alim=sw(pw_im[:, S5_L]),
                s5_d=w["s5_d"].reshape(dd, W_BR // LANES, 1, LANES), s5_glu_w=w["s5_glu_w"].astype(BF16))


def _unit_lower_solve(ms, rhs, c):
    def split(a):
        hi = a.astype(BF16)
        return hi, (a - hi.astype(F32)).astype(BF16)

    def dot3(a, b):
        return _dot(a[0], b[0]) + _dot(a[1], b[0]) + _dot(a[0], b[1])

    n = range(len(ms))
    ms = [split(m) for m in ms]
    rs = [split(r) for r in rhs]
    xs = [rhs[i] - dot3(ms[i], rs[i]) for i in n]
    k = 2
    while k < c:
        if k <= 2 ** GDN_SPLIT_LEVELS:
            ms = [split(dot3(ms[i], ms[i])) for i in n]
            rs = [split(x) for x in xs]
            xs = [xs[i] + dot3(ms[i], rs[i]) for i in n]
        else:
            ms = [(_dot(ms[i][0], ms[i][0]).astype(BF16), None) for i in n]
            xs = [xs[i] + _dot(ms[i][0], xs[i].astype(BF16)) for i in n]
        k *= 2
    return xs


def _gdn_kernel(l_ref, q_ref, k_ref, v_ref, sm_ref, gg_ref, bq_ref, bk_ref, bv_ref, cwq_ref, cwk_ref, cwv_ref,
                alog_ref, dtb_ref, nw_ref, s0_ref, y_ref, bufo_ref, so_ref,
                xp_scr, qkv_scr, gx_scr, bx_scr, gc_scr, s_scr, *, tt, nt, c):
    ti = pl.program_id(1)
    log2c = c.bit_length() - 1

    @pl.when(ti == 0)
    def _():
        for i, b_ref in enumerate((bq_ref, bk_ref, bv_ref)):
            xp_scr[i, 5:8, :] = b_ref[...]
        s_scr[...] = s0_ref[...]

    for i, (x_ref, cw_ref) in enumerate(((q_ref, cwq_ref), (k_ref, cwk_ref), (v_ref, cwv_ref))):
        x = x_ref[...]
        xp_scr[i, 8:8 + tt, :] = x
        cw = cw_ref[...]
        cv = (cw[3:4] * x + cw[2:3] * xp_scr[i, 7:7 + tt, :]
              + cw[1:2] * xp_scr[i, 6:6 + tt, :] + cw[0:1] * xp_scr[i, 5:5 + tt, :])
        tail = x[tt - 3:tt, :]
        xp_scr[i, 5:8, :] = tail
        bufo_ref[:, i * W_BR:(i + 1) * W_BR] = tail
        cv = _silu(cv)
        if i < 2:
            scale = GDN_HD ** -0.5 if i == 0 else 1.0
            for h in range(GDN_H):
                sl = slice(h * GDN_HD, (h + 1) * GDN_HD)
                xh = cv[:, sl]
                qkv_scr[i, :, sl] = xh * (lax.rsqrt(jnp.sum(xh * xh, axis=-1, keepdims=True) + 1e-6) * scale)
        else:
            qkv_scr[i] = cv

    sm = sm_ref[...]
    gsm = -jnp.exp(alog_ref[...]) * _softplus(sm + dtb_ref[...])
    bsm = _sigmoid(sm)
    src = _iota((LANES, W_BR), 0)
    head = _iota((LANES, W_BR), 1) >> 7
    gx_scr[...] = _dot_hi(gsm, jnp.where(src - SM_A == head, 1.0, 0.0))
    bx_scr[...] = _dot_hi(bsm, jnp.where(src - SM_B == head, 1.0, 0.0))
    headc = _iota((LANES, GDN_H * c), 1) >> log2c
    gc_scr[...] = _dot_hi(gsm, jnp.where(_iota((LANES, GDN_H * c), 0) - SM_A == headc, 1.0, 0.0))

    rowi = _iota((c, c), 0)
    coli = _iota((c, c), 1)
    incl = coli <= rowi
    strict = coli < rowi
    ltri = jnp.where(incl, 1.0, 0.0)
    upper = jnp.where(_iota((c, GDN_H * c), 0) > (_iota((c, GDN_H * c), 1) & (c - 1)), 1.0, 0.0)
    nw = nw_ref[...]

    def chunk(ci, carry):
        r0 = pl.multiple_of(ci * c, c)
        gcb_all = _dot_hi(ltri, gx_scr[pl.ds(r0, c), :])
        diffs = _dot_hi(ltri, gc_scr[pl.ds(r0, c), :] * upper)
        beta_all = bx_scr[pl.ds(r0, c), :]
        hs = range(GDN_H)
        sls = [slice(h * GDN_HD, (h + 1) * GDN_HD) for h in hs]
        q = [qkv_scr[0, pl.ds(r0, c), sl] for sl in sls]
        k = [qkv_scr[1, pl.ds(r0, c), sl] for sl in sls]
        gcb = [gcb_all[:, sl] for sl in sls]
        decay = [jnp.where(incl, jnp.exp(diffs[:, h * c:(h + 1) * c]), 0.0) for h in hs]
        kb = [k[h] * beta_all[:, sls[h]] for h in hs]
        kbf = [x.astype(BF16) for x in k]
        kq = [_dot_nt(jnp.concatenate([kb[h], q[h]], axis=0).astype(BF16), kbf[h]) for h in hs]
        m = [jnp.where(strict, kq[h][:c] * decay[h], 0.0) for h in hs]
        qk = [(kq[h][c:] * decay[h]).astype(BF16) for h in hs]
        rhs = [jnp.concatenate([qkv_scr[2, pl.ds(r0, c), sls[h]] * beta_all[:, sls[h]], kb[h] * jnp.exp(gcb[h])],
                               axis=1) for h in hs]
        sol = _unit_lower_solve(m, rhs, c)
        s = [s_scr[h] for h in hs]
        sb = [x.astype(BF16) for x in s]
        ws = [_dot(jnp.concatenate([sol[h][:, GDN_HD:], q[h] * jnp.exp(gcb[h])], axis=0).astype(BF16), sb[h])
              for h in hs]
        vnb = [(sol[h][:, :GDN_HD] - ws[h][:c]).astype(BF16) for h in hs]
        o = [ws[h][c:] + _dot(qk[h], vnb[h]) for h in hs]
        for h in hs:
            glast = gcb[h][c - 1:c, :]
            s_scr[h] = s[h] * jnp.exp(glast) + _dot_tn((k[h] * jnp.exp(glast - gcb[h])).astype(BF16), vnb[h])
        for h in hs:
            on = o[h] * lax.rsqrt(jnp.mean(o[h] * o[h], axis=-1, keepdims=True) + 1e-6) * nw
            y_ref[pl.ds(r0, c), sls[h]] = (on * _silu(gg_ref[pl.ds(r0, c), sls[h]])).astype(y_ref.dtype)
        return carry

    lax.fori_loop(0, tt // c, chunk, 0)

    @pl.when(ti == nt - 1)
    def _():
        so_ref[...] = s_scr[...]


def _gdn(lidx, z3, buf, s0, p, ydtype, chunk):
    s, t, _ = z3.shape
    tt = min(t, 256)
    nt = t // tt
    c = min(chunk, tt)
    zspec = lambda col: pl.BlockSpec((None, tt, W_BR), lambda b, i, l: (b, i, col // W_BR))
    bspec = lambda j: pl.BlockSpec((None, 3, W_BR), lambda b, i, l: (b, 0, j))
    cspec = lambda j: pl.BlockSpec((None, CONV_K, W_BR), lambda b, i, l: (l[0], 0, j))
    rspec = pl.BlockSpec((None, 1, LANES), lambda b, i, l: (l[0], 0, 0))
    sspec = pl.BlockSpec((None, GDN_H, GDN_HD, GDN_HD), lambda b, i, l: (b, 0, 0, 0))
    return pl.pallas_call(
        functools.partial(_gdn_kernel, tt=tt, nt=nt, c=c),
        grid_spec=pltpu.PrefetchScalarGridSpec(
            num_scalar_prefetch=1, grid=(s, nt),
            in_specs=[zspec(C_QKV), zspec(C_QKV + W_BR), zspec(C_QKV + 2 * W_BR),
                      pl.BlockSpec((None, tt, LANES), lambda b, i, l: (b, i, C_SMALL_B // LANES)),
                      zspec(C_GGDN), bspec(0), bspec(1), bspec(2), cspec(0), cspec(1), cspec(2),
                      rspec, rspec, rspec, sspec],
            out_specs=[pl.BlockSpec((None, tt, W_BR), lambda b, i, l: (b, i, 0)),
                       pl.BlockSpec((None, 3, 3 * W_BR), lambda b, i, l: (b, 0, 0)),
                       sspec],
            scratch_shapes=[pltpu.VMEM((3, tt + 8, W_BR), F32), pltpu.VMEM((3, tt, W_BR), F32),
                            pltpu.VMEM((tt, W_BR), F32), pltpu.VMEM((tt, W_BR), F32),
                            pltpu.VMEM((tt, GDN_H * c), F32), pltpu.VMEM((GDN_H, GDN_HD, GDN_HD), F32)]),
        out_shape=[jax.ShapeDtypeStruct((s, t, W_BR), ydtype),
                   jax.ShapeDtypeStruct((s, 3, 3 * W_BR), F32),
                   jax.ShapeDtypeStruct((s, GDN_H, GDN_HD, GDN_HD), F32)],
        compiler_params=_cparams(("parallel", "arbitrary")),
        name="gdn",
    )(lidx, z3, z3, z3, z3, z3, buf, buf, buf, p["gdn_conv_w"], p["gdn_conv_w"], p["gdn_conv_w"],
      p["gdn_a_log"], p["gdn_dt_bias"], p["gdn_norm_w"], s0)


PAGES_PER_STEP = 4


def _gather_kernel(l_ref, pt_ref, p0_ref, p1_ref, p2_ref, p3_ref, zn_ref, cmp_ref, sel_ref, *, n_past):
    i = pl.program_id(1)
    half = 2 * NSA_KVH * NSA_HD

    @pl.when(i < n_past)
    def _():
        for k, p_ref in enumerate((p0_ref, p1_ref, p2_ref, p3_ref)):
            rows = slice(k * PAGE_SIZE, (k + 1) * PAGE_SIZE)
            for r in range(4):
                for g in range(NSA_KVH):
                    x = p_ref[pl.ds(r * NSA_KVH + g, PAGE_SIZE, stride=4 * NSA_KVH), :]
                    cols = slice(((r % 2) * NSA_KVH + g) * NSA_HD, ((r % 2) * NSA_KVH + g + 1) * NSA_HD)
                    if r < 2:
                        cmp_ref[rows, cols] = x
                    else:
                        sel_ref[rows, cols] = x.astype(BF16)

    @pl.when(i == n_past)
    def _():
        sel_ref[...] = jnp.zeros(sel_ref.shape, BF16)
        new = zn_ref[:, half:2 * half]
        sel_ref[0:2 * SUBLANES, :] = jnp.concatenate([new, jnp.zeros_like(new)], axis=0).astype(BF16)


def _gather_ctx(lidx, pt_flat, cache, z3):
    b, t_new, _ = z3.shape
    assert t_new == SUBLANES
    n_pages = pt_flat.shape[0] // b
    n_past = n_pages // PAGES_PER_STEP
    rows = PAGES_PER_STEP * PAGE_SIZE
    width = 4 * NSA_KVH * NSA_HD
    half = width // 2
    n_pool = cache.shape[1]
    cache = cache.reshape(cache.shape[0] * n_pool, PAGE_SIZE * 4 * NSA_KVH, NSA_HD)

    def pspec(k):
        def imap(bi, i, l, pt):
            page = jnp.minimum(i * PAGES_PER_STEP + k, n_pages - 1)
            return (l[0] * n_pool + pt[bi * n_pages + page], 0, 0)
        return pl.BlockSpec((None, PAGE_SIZE * 4 * NSA_KVH, NSA_HD), imap)

    return pl.pallas_call(
        functools.partial(_gather_kernel, n_past=n_past),
        grid_spec=pltpu.PrefetchScalarGridSpec(
            num_scalar_prefetch=2, grid=(b, n_past + 1),
            in_specs=[pspec(k) for k in range(PAGES_PER_STEP)]
            + [pl.BlockSpec((None, t_new, width), lambda bi, i, l, pt: (bi, 0, C_KV // width))],
            out_specs=[pl.BlockSpec((None, rows, half), lambda bi, i, l, pt: (bi, jnp.minimum(i, n_past - 1), 0)),
                       pl.BlockSpec((None, rows, half), lambda bi, i, l, pt: (bi, i, 0))]),
        out_shape=[jax.ShapeDtypeStruct((b, n_past * rows, half), F32),
                   jax.ShapeDtypeStruct((b, (n_past + 1) * rows, half), BF16)],
        compiler_params=_cparams(("parallel", "arbitrary")),
        name="nsa_gather",
    )(lidx, pt_flat, cache, cache, cache, cache, z3)


def _cmp_kernel(l_ref, x0_ref, x1_ref, w1_ref, pe_ref, w2_ref, o_ref, carry_scr, *, nh):
    @pl.when(pl.program_id(2) == 0)
    def _():
        carry_scr[...] = jnp.zeros(carry_scr.shape, F32)

    w1 = w1_ref[...]
    bias = _dot_hi(pe_ref[...], w1.astype(F32))
    last = _iota((nh, NSA_HD), 0) == nh - 1
    for g, x_ref in enumerate((x0_ref, x1_ref)):
        ucat = jnp.concatenate([x_ref[pl.ds(s, nh, stride=CMP_STRIDE), :] for s in range(CMP_STRIDE)], axis=1)
        hh = _dot(ucat.astype(BF16), w1)
        hf = hh[:, :NSA_HD] + bias[0:1, :NSA_HD]
        hs = hh[:, NSA_HD:] + bias[1:2, NSA_HD:]
        hs_next = jnp.where(last, carry_scr[g], pltpu.roll(hs, nh - 1, axis=0))
        carry_scr[g] = hs[0:1, :]
        hid = _silu(hf + hs_next)
        o_ref[g] = _dot(hid.astype(BF16), w2_ref[...]).astype(BF16)


def _compress(lidx, src3, col0, n_rows, p):
    b = src3.shape[0]
    tr = min(n_rows, 4096)
    nt = n_rows // tr
    nh = tr // CMP_STRIDE
    xspec = lambda g: pl.BlockSpec((None, tr, NSA_HD), lambda bi, c, i, l: (bi, nt - 1 - i, col0 + NSA_KVH * c + g))
    return pl.pallas_call(
        functools.partial(_cmp_kernel, nh=nh),
        grid_spec=pltpu.PrefetchScalarGridSpec(
            num_scalar_prefetch=1, grid=(b, 2, nt),
            in_specs=[xspec(0), xspec(1),
                      pl.BlockSpec((None, None, CMP_STRIDE * NSA_HD, 2 * NSA_HD), lambda bi, c, i, l: (l[0], c, 0, 0)),
                      pl.BlockSpec((None, SUBLANES, CMP_STRIDE * NSA_HD), lambda bi, c, i, l: (l[0], 0, 0)),
                      pl.BlockSpec((None, None, NSA_HD, NSA_HD), lambda bi, c, i, l: (l[0], c, 0, 0))],
            out_specs=pl.BlockSpec((None, None, NSA_KVH, nh, NSA_HD), lambda bi, c, i, l: (bi, c, 0, nt - 1 - i, 0)),
            scratch_shapes=[pltpu.VMEM((NSA_KVH, 1, NSA_HD), F32)]),
        out_shape=jax.ShapeDtypeStruct((b, 2, NSA_KVH, n_rows // CMP_STRIDE, NSA_HD), BF16),
        compiler_params=_cparams(("parallel", "parallel", "arbitrary")),
        name="nsa_compress",
    )(lidx, src3, src3, p["nsa_w1"], p["nsa_pe"], p["nsa_w2"])


def _masked_softmax(s, valid):
    s = jnp.where(valid, s, NEG)
    m = jnp.max(s, axis=-1, keepdims=True)
    e = jnp.where(valid, jnp.exp2(s - m), 0.0)
    den = jnp.sum(e, axis=-1, keepdims=True)
    return e / jnp.where(den > 0.0, den, 1.0)


def _attn_kernel(l_ref, q_ref, sm_ref, gn_ref, kc_ref, vc_ref, ks_ref, vs_ref, kw_ref, vw_ref, y_ref, *,
                 qb, pos0, ncp, ns, nsp, kt, wn, wpos0, tw):
    g = pl.program_id(1)
    q0 = pl.program_id(2) * qb
    qpos0 = pos0 + q0
    hd = NSA_HD
    q = q_ref[...] * (hd ** -0.5 * LOG2E)
    qr = jnp.concatenate([q[:, j * hd:(j + 1) * hd] for j in range(NSA_GQ)], axis=0).astype(BF16)
    slope_g = jnp.where(g == 0, LOG2E, LOG2E * 2.0 ** -NSA_GQ)
    slopes = [slope_g * 2.0 ** -(j + 1) for j in range(NSA_GQ)]
    heads = lambda a: [a[j * qb:(j + 1) * qb] for j in range(NSA_GQ)]

    dist = (qpos0 + _iota((qb, ncp), 0)) - (_iota((qb, ncp), 1) * CMP_STRIDE + (CMP_BLOCK - 1))
    valid = dist >= 0
    distf = dist.astype(F32)
    sc = heads(_dot_nt(qr, kc_ref[...]))
    pc = [_masked_softmax(sc[j] - slopes[j] * distf, valid) for j in range(NSA_GQ)]
    o_c = heads(_dot(jnp.concatenate(pc, axis=0).astype(BF16), vc_ref[...]))
    imp = pc[0] + pc[1] + pc[2] + pc[3]
    pool = jnp.where((_iota((ncp, nsp), 0) >> 2) == _iota((ncp, nsp), 1), 1.0, 0.0)
    imp = _dot_hi(imp, pool)

    blk = _iota((qb, nsp), 1)
    qp = qpos0 + _iota((qb, nsp), 0)
    val = jnp.where(blk * SEL_BLOCK > qp, -FORCE, imp)
    val = jnp.where(blk == (qp >> 6), FORCE, jnp.where(blk == 0, FORCE, val))
    val = jnp.where(blk >= ns, -3.0 * FORCE, val)
    top = float(min(TOP_N, ns))
    if qb == LANES and nsp == LANES:
        nsr = -(-ns // SUBLANES) * SUBLANES
        val_t = val.T[:nsr]
        blk_t = _iota((nsr, qb), 0)
        rank = jnp.zeros((nsr, qb), F32)
        for bidx in range(ns):
            cand = val_t[bidx:bidx + 1, :]
            rank = rank + jnp.where(cand > val_t, 1.0, jnp.where(cand == val_t, jnp.where(blk_t > bidx, 1.0, 0.0), 0.0))
        sel_t = jnp.where(rank < top, 1.0, 0.0)
        selb = jnp.concatenate([sel_t, jnp.zeros((nsp - nsr, qb), F32)], axis=0).T.astype(BF16)
    else:
        rank = jnp.zeros((qb, nsp), F32)
        for bidx in range(ns):
            cand = val[:, bidx:bidx + 1]
            rank = rank + jnp.where(cand > val, 1.0, jnp.where(cand == val, jnp.where(blk > bidx, 1.0, 0.0), 0.0))
        selb = jnp.where(rank < top, 1.0, 0.0).astype(BF16)

    n_tiles = (qpos0 + qb + kt - 1) // kt
    qh = heads(qr)
    spread = jnp.where(_iota((LANES, kt), 0) == (_iota((LANES, kt), 1) >> 6), 1.0, 0.0).astype(BF16)

    def tile_bias(t):
        k0 = t * kt
        d = (qpos0 + _iota((qb, kt), 0)) - (k0 + _iota((qb, kt), 1))
        pick_blk = jnp.where(_iota((nsp, LANES), 0) == (k0 >> 6) + _iota((nsp, LANES), 1), 1.0, 0.0).astype(BF16)
        sel_tile = _dot(selb, pick_blk).astype(BF16)
        ok = jnp.where(d >= 0, _dot(sel_tile, spread), 0.0) > 0.5
        return jnp.where(ok, d.astype(F32), -NEG)

    def tile(t, carry):
        k0 = pl.multiple_of(t * kt, kt)
        kk = ks_ref[pl.ds(k0, kt), :].astype(BF16)
        vv = vs_ref[pl.ds(k0, kt), :].astype(BF16)
        s = [_dot_nt(qh[j], kk) for j in range(NSA_GQ)]
        bias = carry[NSA_GQ]
        bias_next = tile_bias(t + 1)
        new = []
        for j in range(NSA_GQ):
            m, lsum, acc = carry[j]
            sm = s[j] - slopes[j] * bias
            m_new = jnp.maximum(m, jnp.max(sm, axis=-1, keepdims=True))
            pr = jnp.exp2(sm - m_new)
            alpha = jnp.exp2(m - m_new)
            new.append((m_new, alpha * lsum + jnp.sum(pr, axis=-1, keepdims=True),
                        alpha * acc + _dot(pr.astype(BF16), vv)))
        return tuple(new) + (bias_next,)

    init = (jnp.full((qb, 1), NEG, F32), jnp.zeros((qb, 1), F32), jnp.zeros((qb, hd), F32))
    fin = lax.fori_loop(0, n_tiles, tile, (init,) * NSA_GQ + (tile_bias(0),))
    o_s = [acc / lsum for _, lsum, acc in fin[:NSA_GQ]]

    k0w = pl.multiple_of(jnp.clip(q0 - WINDOW, 0, tw - wn), SUBLANES)
    dw = (qpos0 + _iota((qb, wn), 0)) - (wpos0 + k0w + _iota((qb, wn), 1))
    okw = jnp.abs(2 * dw - (WINDOW - 1)) < WINDOW
    dwf = dw.astype(F32)
    sw = heads(_dot_nt(qr, kw_ref[pl.ds(k0w, wn), :].astype(BF16)))
    pw = [_masked_softmax(sw[j] - slopes[j] * dwf, okw) for j in range(NSA_GQ)]
    o_w = heads(_dot(jnp.concatenate(pw, axis=0).astype(BF16), vw_ref[pl.ds(k0w, wn), :].astype(BF16)))

    gate = _sigmoid(sm_ref[...])
    lane = _iota((qb, LANES), 1)
    pick = lambda idx: jnp.sum(jnp.where(lane == idx, gate, 0.0), axis=-1, keepdims=True)
    for j in range(NSA_GQ):
        base = SM_GL + 3 * (g * NSA_GQ + j)
        o = pick(base) * o_c[j] + pick(base + 1) * o_s[j] + pick(base + 2) * o_w[j]
        y_ref[:, j * hd:(j + 1) * hd] = (o * _silu(gn_ref[:, j * hd:(j + 1) * hd])).astype(y_ref.dtype)


def _attention(lidx, z3, cmp_kv, ks_src, ks_col, vs_col, kw_src, kw_col, vw_col, pos0, wpos0, ydtype):
    b, t, _ = z3.shape
    qb = min(Q_BLOCK, t)
    ncp = cmp_kv.shape[3]
    tk = ks_src.shape[1]
    tw = kw_src.shape[1]
    if qb == Q_BLOCK:
        kt = 512
    else:
        kt = max(k for k in range(512, min(tk, LANES * SEL_BLOCK) + 1, 512) if tk % k == 0)
    ns = -(-(pos0 + t) // SEL_BLOCK)
    nsp = -(-ns // LANES) * LANES
    wn = WINDOW + Q_BLOCK
    assert tk % kt == 0 and tk >= pos0 + t and tw >= wn and ncp // CMP_PER_SEL <= nsp
    hw = NSA_GQ * NSA_HD
    kern = functools.partial(_attn_kernel, qb=qb, pos0=pos0, ncp=ncp, ns=ns, nsp=nsp, kt=kt, wn=wn, wpos0=wpos0, tw=tw)
    kvspec = lambda rows, col: pl.BlockSpec((None, rows, NSA_HD), lambda bi, g, i, l: (bi, 0, col + g))
    cspec = lambda c: pl.BlockSpec((None, None, None, ncp, NSA_HD), lambda bi, g, i, l: (bi, c, g, 0, 0))
    return pl.pallas_call(
        kern,
        grid_spec=pltpu.PrefetchScalarGridSpec(
            num_scalar_prefetch=1, grid=(b, NSA_KVH, t // qb),
            in_specs=[pl.BlockSpec((None, qb, hw), lambda bi, g, i, l: (bi, i, C_QNSA // hw + g)),
                      pl.BlockSpec((None, qb, LANES), lambda bi, g, i, l: (bi, i, C_SMALL_A // LANES)),
                      pl.BlockSpec((None, qb, hw), lambda bi, g, i, l: (bi, i, C_GNSA // hw + g)),
                      cspec(0), cspec(1),
                      kvspec(tk, ks_col), kvspec(tk, vs_col), kvspec(tw, kw_col), kvspec(tw, vw_col)],
            out_specs=pl.BlockSpec((None, qb, hw), lambda bi, g, i, l: (bi, i, g))),
        out_shape=jax.ShapeDtypeStruct((b, t, W_BR), ydtype),
        compiler_params=_cparams(("parallel", "parallel", "arbitrary")),
        name="nsa_attention",
    )(lidx, z3, z3, z3, cmp_kv, cmp_kv, ks_src, ks_src, kw_src, kw_src)


def _merge_kernel(l_ref, y0_ref, y1_ref, y2_ref, y3_ref, g0_ref, g1_ref, g2_ref, g3_ref, w_ref, o_ref):
    acc = None
    for m, (y_ref, g_ref) in enumerate(((y0_ref, g0_ref), (y1_ref, g1_ref), (y2_ref, g2_ref), (y3_ref, g3_ref))):
        term = _sigmoid(g_ref[...]) * _dot(y_ref[...].astype(BF16), w_ref[m])
        acc = term if acc is None else acc + term
    o_ref[...] = acc.astype(o_ref.dtype)


def _merge(lidx, ys, z2d, p):
    n = z2d.shape[0]
    tm = min(n, 256)
    yspec = pl.BlockSpec((tm, W_BR), lambda i, l: (i, 0))
    gspec = lambda m: pl.BlockSpec((tm, D_MODEL), lambda i, l: (i, C_MERGE // D_MODEL + m))
    wspec = pl.BlockSpec((None, N_BRANCH, W_BR, D_MODEL), lambda i, l: (l[0], 0, 0, 0), pipeline_mode=pl.Buffered(1))
    return pl.pallas_call(
        _merge_kernel,
        grid_spec=pltpu.PrefetchScalarGridSpec(
            num_scalar_prefetch=1, grid=(n // tm,),
            in_specs=[yspec] * N_BRANCH + [gspec(m) for m in range(N_BRANCH)] + [wspec],
            out_specs=pl.BlockSpec((tm, D_MODEL), lambda i, l: (i, 0))),
        out_shape=jax.ShapeDtypeStruct((n, D_MODEL), BF16),
        compiler_params=_cparams(("parallel",)),
        name="merge",
    )(lidx, *ys, z2d, z2d, z2d, z2d, p["w_branch"])


def _outproj_kernel(l_ref, m_ref, x_ref, w_ref, g_ref, b_ref, o_ref, *, alpha):
    v = alpha * x_ref[...] + _dot(m_ref[...], w_ref[...])
    mu = jnp.mean(v, axis=-1, keepdims=True)
    c = v - mu
    var = jnp.mean(c * c, axis=-1, keepdims=True)
    o_ref[...] = c * lax.rsqrt(var + 1e-5) * g_ref[...] + b_ref[...]


def _outproj(lidx, merged, x2d, p, alpha):
    n = x2d.shape[0]
    tm = min(n, 256)
    return pl.pallas_call(
        functools.partial(_outproj_kernel, alpha=alpha),
        grid_spec=pltpu.PrefetchScalarGridSpec(
            num_scalar_prefetch=1, grid=(n // tm,),
            in_specs=[pl.BlockSpec((tm, D_MODEL), lambda i, l: (i, 0)),
                      pl.BlockSpec((tm, D_MODEL), lambda i, l: (i, 0)),
                      pl.BlockSpec((None, D_MODEL, D_MODEL), lambda i, l: (l[0], 0, 0)),
                      pl.BlockSpec((None, 1, D_MODEL), lambda i, l: (l[0], 0, 0)),
                      pl.BlockSpec((None, 1, D_MODEL), lambda i, l: (l[0], 0, 0))],
            out_specs=pl.BlockSpec((tm, D_MODEL), lambda i, l: (i, 0))),
        out_shape=jax.ShapeDtypeStruct((n, D_MODEL), F32),
        compiler_params=_cparams(("parallel",)),
        name="outproj_ln",
    )(lidx, merged, x2d, p["w_out"], p["ln_g"], p["ln_b"])


def _prep_params(w):
    p = {}
    row = lambda a: a[:, None, :]
    p["lru_conv_w"] = w["lru_conv_w"]
    p["lru_conv_b"] = row(w["lru_conv_b"])
    p["lru_wa"] = w["lru_wa"].astype(BF16)
    p["lru_ba"] = row(w["lru_ba"])
    p["lru_wx"] = w["lru_wx"].astype(BF16)
    p["lru_bx"] = row(w["lru_bx"])
    p["lru_lambda"] = row(w["lru_lambda"])
    p.update(_s5_weights(w))
    p["w_branch"] = w["w_branch"].astype(BF16)
    p["w_out"] = w["w_out"].astype(BF16)
    p["ln_g"] = row(w["ln_g"])
    p["ln_b"] = row(w["ln_b"])
    dd = w["gdn_a_log"].shape[0]
    lane_row = lambda a, off: jnp.zeros((dd, 1, LANES), F32).at[:, 0, off:off + a.shape[-1]].set(a)
    p["gdn_conv_w"] = w["gdn_conv_w"]
    p["gdn_a_log"] = lane_row(w["gdn_a_log"], SM_A)
    p["gdn_dt_bias"] = lane_row(w["gdn_dt_bias"], SM_A)
    p["gdn_norm_w"] = row(w["gdn_norm_w"])
    flat = CMP_STRIDE * NSA_HD
    w1 = w["nsa_cmp_w1"].reshape(dd, 2, 2, flat, NSA_HD).transpose(0, 1, 3, 2, 4)
    p["nsa_w1"] = w1.reshape(dd, 2, flat, 2 * NSA_HD).astype(BF16)
    pe = w["nsa_cmp_pos"].reshape(dd, 2, flat)
    p["nsa_pe"] = jnp.concatenate([pe, jnp.zeros((dd, SUBLANES - 2, flat), F32)], axis=1)
    p["nsa_w2"] = w["nsa_cmp_w2"].astype(BF16)
    return p


def _mixer_layer(lidx, x2d, s, t, state, nsa_branch, p, win, alpha, ydtype):
    lru_buf, lru_h, gdn_buf, gdn_s, s5_re, s5_im = state
    z = _inproj(lidx, win[0], x2d, win[1])
    z3 = z.reshape(s, t, NP)
    y_lru, lru_buf, lru_h = _lru(lidx, z3, lru_buf, lru_h, p, ydtype)
    y_nsa = nsa_branch(z3)
    y_gdn, gdn_buf, gdn_s = _gdn(lidx, z3, gdn_buf, gdn_s, p, ydtype, GDN_CHUNK)
    y_s5, s5_re, s5_im = _s5_scan(lidx, z3, s5_re, s5_im, p, s5_re.shape[1])
    y_s5 = _s5_glu(lidx, y_s5, z, p, ydtype)
    flat = lambda y: y.reshape(s * t, W_BR)
    merged = _merge(lidx, (flat(y_lru), flat(y_nsa), flat(y_gdn), y_s5), z, p)
    x_new = _outproj(lidx, merged, x2d, p, alpha)
    return x_new, z3, (lru_buf, lru_h, gdn_buf, gdn_s, s5_re, s5_im)


def kernel(x_prompt, x_sample, state_lru_h, state_lru_conv, cache_nsa_kv, cache_win_kv, state_gdn_s, state_gdn_conv, state_s5_re, state_s5_im, page_table, w_in, lru_conv_w, lru_conv_b, lru_wa, lru_ba, lru_wx, lru_bx, lru_lambda, nsa_cmp_pos, nsa_cmp_w1, nsa_cmp_w2, gdn_conv_w, gdn_a_log, gdn_dt_bias, gdn_norm_w, s5_lam_re, s5_lam_im, s5_log_dt, s5_b_re, s5_b_im, s5_c_re, s5_c_im, s5_d, s5_glu_w, w_branch, w_out, ln_g, ln_b):
    depth = w_in.shape[0]
    bp, tp, _ = x_prompt.shape
    db, ts, _ = x_sample.shape
    n_pages = page_table.shape[1]
    past = n_pages * PAGE_SIZE
    wbuf = cache_win_kv.shape[2]
    alpha = (2.0 * depth) ** 0.25
    kvw = 4 * NSA_KVH * NSA_HD
    winw = 2 * NSA_KVH * NSA_HD
    p = _prep_params(dict(
        lru_conv_w=lru_conv_w, lru_conv_b=lru_conv_b, lru_wa=lru_wa, lru_ba=lru_ba, lru_wx=lru_wx, lru_bx=lru_bx,
        lru_lambda=lru_lambda, nsa_cmp_pos=nsa_cmp_pos, nsa_cmp_w1=nsa_cmp_w1, nsa_cmp_w2=nsa_cmp_w2,
        gdn_conv_w=gdn_conv_w, gdn_a_log=gdn_a_log, gdn_dt_bias=gdn_dt_bias, gdn_norm_w=gdn_norm_w,
        s5_lam_re=s5_lam_re, s5_lam_im=s5_lam_im, s5_log_dt=s5_log_dt, s5_b_re=s5_b_re, s5_b_im=s5_b_im,
        s5_c_re=s5_c_re, s5_c_im=s5_c_im, s5_d=s5_d, s5_glu_w=s5_glu_w, w_branch=w_branch, w_out=w_out,
        ln_g=ln_g, ln_b=ln_b))
    assert w_in.shape[2] == sum(s[1] for s in IN_SEGMENTS)
    wp = (jnp.asarray(_tile_sources(), jnp.int32), jnp.swapaxes(w_in, 1, 2))
    p = lax.optimization_barrier(p)
    pt_flat = page_table.reshape(-1).astype(jnp.int32)
    sb = db if ts == S5_L else 1
    c0 = C_KV // LANES
    zero_state = (jnp.zeros((bp, CONV_K - 1, W_BR), F32), jnp.zeros((bp, 1, W_BR), F32),
                  jnp.zeros((bp, CONV_K - 1, 3 * W_BR), F32), jnp.zeros((bp, GDN_H, GDN_HD, GDN_HD), F32),
                  jnp.zeros((bp, 1, S5_G * S5_P), F32), jnp.zeros((bp, 1, S5_G * S5_P), F32))

    def layer(carry, l):
        xp, xs = carry
        lidx = l.reshape(1)
        at = lambda a: lax.dynamic_index_in_dim(a, l, 0, keepdims=False)

        def nsa_prompt(z3):
            ckv = _compress(lidx, z3, c0, tp, p)
            return _attention(lidx, z3, ckv, z3, c0 + 4, c0 + 6, z3, c0 + 8, c0 + 10, 0, 0, BF16)

        xp, zp3, st_p = _mixer_layer(lidx, xp, bp, tp, zero_state, nsa_prompt, p, wp, alpha, BF16)

        win_state = {}

        def nsa_sample(z3):
            cmp_rows, sel_rows = _gather_ctx(lidx, pt_flat, cache_nsa_kv, z3)
            ckv = _compress(lidx, cmp_rows, 0, past, p)
            win = jnp.concatenate([at(cache_win_kv).reshape(db, wbuf, winw),
                                   z3[:, :, C_KV + kvw:C_KV + kvw + winw]], axis=1)
            win_state["win"] = win
            pad = max(0, WINDOW + Q_BLOCK - (wbuf + ts))
            win_pad = jnp.concatenate([win, jnp.zeros((db, pad, winw), F32)], axis=1)
            return _attention(lidx, z3, ckv, sel_rows, 0, 2, win_pad, 0, 2, past, past - wbuf, F32)

        st_in = (at(state_lru_conv), at(state_lru_h)[:, None, :], at(state_gdn_conv), at(state_gdn_s),
                 at(state_s5_re).reshape(db // sb, sb, S5_G * S5_P), at(state_s5_im).reshape(db // sb, sb, S5_G * S5_P))
        xs, zs3, st_s = _mixer_layer(lidx, xs, db, ts, st_in, nsa_sample, p, wp, alpha, F32)

        def outs(st, z3, s, t, win):
            lru_buf, lru_h, gdn_buf, gdn_s, s5_re, s5_im = st
            return (lru_h.reshape(s, W_BR), lru_buf,
                    z3[:, :, C_KV:C_KV + kvw].reshape(s, t, 4, NSA_KVH, NSA_HD),
                    win.reshape(s, win.shape[1], 2, NSA_KVH, NSA_HD),
                    gdn_s, gdn_buf, s5_re.reshape(s, S5_G, S5_P), s5_im.reshape(s, S5_G, S5_P))

        win_p = zp3[:, tp - min(WINDOW, tp):, C_KV + kvw:C_KV + kvw + winw]
        win_s = win_state["win"][:, wbuf + ts - min(WINDOW, past + ts):]
        return (xp, xs), (outs(st_p, zp3, bp, tp, win_p), outs(st_s, zs3, db, ts, win_s))

    (xp, xs), (op, os_) = lax.scan(layer, (x_prompt.reshape(bp * tp, D_MODEL), x_sample.reshape(db * ts, D_MODEL)),
                                   jnp.arange(depth, dtype=jnp.int32))
    return (xp.reshape(bp, tp, D_MODEL), xs.reshape(db, ts, D_MODEL)) + tuple(op) + tuple(os_)
```

```python
import functools

import jax
import jax.numpy as jnp
from jax import lax
from jax.experimental import pallas as pl
from jax.experimental.pallas import tpu as pltpu

F32 = jnp.float32
BF16 = jnp.bfloat16
HI = lax.Precision.HIGHEST

D_MODEL = 2048
W_BR = D_MODEL // 2
N_BRANCH = 4
CONV_K = 4
LRU_BLOCKS = 8
LRU_BS = W_BR // LRU_BLOCKS
LRU_C = 8.0
NSA_HD = 128
NSA_H = 8
NSA_KVH = 2
NSA_GQ = NSA_H // NSA_KVH
CMP_STRIDE = 16
CMP_BLOCK = 32
SEL_BLOCK = 64
CMP_PER_SEL = SEL_BLOCK // CMP_STRIDE
TOP_N = 16
WINDOW = 512
Q_BLOCK = 128
PAGE_SIZE = 128
GDN_HD = 128
GDN_H = 8
GDN_CHUNK = 64
GDN_SPLIT_LEVELS = 2
S5_GS = 16
S5_G = W_BR // S5_GS
S5_P = 64
S5_L = 8
NEG = -1e30
FORCE = 1e9
LOG2E = 1.4426950408889634

LANES = 128
SUBLANES = 8
VMEM_LIMIT = 56 * 1024 * 1024

IN_TILE = 512
C_MERGE = 0
C_ULRU = 8192
C_GLRU = 9216
C_QNSA = 10240
C_GNSA = 11264
C_GGDN = 12288
C_US5 = 13312
C_GS5 = 14336
C_QKV = 15360
C_KV = 18432
C_SMALL_A = 19968
C_SMALL_B = 20480
NP = C_SMALL_B + IN_TILE
IN_SEGMENTS = (("u_lru", W_BR, C_ULRU), ("g_lru", W_BR, C_GLRU), ("q_nsa", W_BR, C_QNSA),
               ("kv", 6 * NSA_KVH * NSA_HD, C_KV), ("gl", 3 * NSA_H, None), ("g_nsa", W_BR, C_GNSA),
               ("qkv", 3 * W_BR, C_QKV), ("a", GDN_H, None), ("b", GDN_H, None),
               ("g_gdn", W_BR, C_GGDN), ("u_s5", W_BR, C_US5), ("g_s5", W_BR, C_GS5),
               ("merge", N_BRANCH * D_MODEL, C_MERGE))


def _segment_start(name):
    return sum(w for n, w, _ in IN_SEGMENTS[:[s[0] for s in IN_SEGMENTS].index(name)])


SRC_SMALL_A = _segment_start("gl") // LANES * LANES
SRC_SMALL_B = _segment_start("a") // LANES * LANES
SM_GL = _segment_start("gl") - SRC_SMALL_A
SM_A = _segment_start("a") - SRC_SMALL_B
SM_B = _segment_start("b") - SRC_SMALL_B


def _tile_sources():
    src = [None] * (NP // IN_TILE)
    at = 0
    for _, width, dst in IN_SEGMENTS:
        if dst is not None:
            for c in range(0, width, IN_TILE):
                src[(dst + c) // IN_TILE] = at + c
        at += width
    src[C_SMALL_A // IN_TILE] = SRC_SMALL_A
    src[C_SMALL_B // IN_TILE] = SRC_SMALL_B
    assert all(s is not None and s % SUBLANES == 0 and s + IN_TILE <= at for s in src)
    return src


def _cparams(sem):
    return pltpu.CompilerParams(dimension_semantics=sem, vmem_limit_bytes=VMEM_LIMIT)


def _sigmoid(x):
    return 1.0 / (1.0 + jnp.exp(-x))


def _silu(x):
    return x * _sigmoid(x)


def _softplus(x):
    return jnp.maximum(x, 0.0) + jnp.log1p(jnp.exp(-jnp.abs(x)))


def _dot(a, b):
    return jnp.dot(a, b, preferred_element_type=F32)


def _dot_hi(a, b):
    return jnp.dot(a, b, preferred_element_type=F32, precision=HI)


def _dot_nt(a, b):
    return lax.dot_general(a, b, (((1,), (1,)), ((), ())), preferred_element_type=F32)


def _dot_tn(a, b):
    return lax.dot_general(a, b, (((0,), (0,)), ((), ())), preferred_element_type=F32)


def _iota(shape, axis):
    return lax.broadcasted_iota(jnp.int32, shape, axis)


def _inproj_kernel(l_ref, src_ref, x_ref, w_ref, o_ref, xb_ref):
    @pl.when(pl.program_id(1) == 0)
    def _():
        xb_ref[...] = x_ref[...].astype(BF16)

    o_ref[...] = _dot_nt(xb_ref[...], w_ref[...].astype(BF16))


def _inproj(lidx, srcs, x2d, wt):
    n = x2d.shape[0]
    tm = min(n, 2048)
    wspec = pl.BlockSpec((pl.Squeezed(), pl.Element(IN_TILE), pl.Element(D_MODEL)),
                         lambda i, j, l, src: (l[0], pl.multiple_of(src[j], SUBLANES), 0))
    xspec = pl.BlockSpec((tm, D_MODEL), lambda i, j, l, src: (i, 0), pipeline_mode=pl.Buffered(1))
    return pl.pallas_call(
        _inproj_kernel,
        grid_spec=pltpu.PrefetchScalarGridSpec(
            num_scalar_prefetch=2, grid=(n // tm, NP // IN_TILE),
            in_specs=[xspec, wspec],
            out_specs=pl.BlockSpec((tm, IN_TILE), lambda i, j, l, src: (i, j)),
            scratch_shapes=[pltpu.VMEM((tm, D_MODEL), BF16)]),
        out_shape=jax.ShapeDtypeStruct((n, NP), F32),
        compiler_params=_cparams(("parallel", "arbitrary")),
        name="inproj",
    )(lidx, srcs, x2d, wt)


def _lru_kernel(l_ref, u_ref, g_ref, buf_ref, h0_ref, cw_ref, cb_ref, wa_ref, ba_ref, wx_ref, bx_ref, lam_ref,
                y_ref, bufo_ref, ho_ref, xp_scr, a_scr, b_scr, h_scr, *, tt, nt):
    ti = pl.program_id(1)

    @pl.when(ti == 0)
    def _():
        xp_scr[5:8, :] = buf_ref[...]
        h_scr[...] = h0_ref[...]

    u = u_ref[...]
    xp_scr[8:8 + tt, :] = u
    cw = cw_ref[...]
    xc = (cb_ref[...] + cw[3:4] * u + cw[2:3] * xp_scr[7:7 + tt, :]
          + cw[1:2] * xp_scr[6:6 + tt, :] + cw[0:1] * xp_scr[5:5 + tt, :])
    tail = u[tt - 3:tt, :]
    xp_scr[5:8, :] = tail
    sp = _softplus(-lam_ref[...])
    for n in range(LRU_BLOCKS):
        sl = slice(n * LRU_BS, (n + 1) * LRU_BS)
        xn = xc[:, sl]
        xb = xn.astype(BF16)
        r = _sigmoid(_dot(xb, wa_ref[n]) + ba_ref[:, sl])
        i = _sigmoid(_dot(xb, wx_ref[n]) + bx_ref[:, sl])
        a = jnp.exp(-LRU_C * r * sp[:, sl])
        a_scr[:, sl] = a
        b_scr[:, sl] = jnp.sqrt(1.0 - a * a) * (i * xn)

    def body(i, h):
        for k in range(SUBLANES):
            t = i * SUBLANES + k
            h = a_scr[pl.ds(t, 1), :] * h + b_scr[pl.ds(t, 1), :]
            b_scr[pl.ds(t, 1), :] = h
        return h

    h = lax.fori_loop(0, tt // SUBLANES, body, h_scr[...])
    h_scr[...] = h
    y_ref[...] = (b_scr[...] * _silu(g_ref[...])).astype(y_ref.dtype)

    @pl.when(ti == nt - 1)
    def _():
        bufo_ref[...] = tail
        ho_ref[...] = h


def _lru(lidx, z3, buf, h0, p, ydtype):
    s, t, _ = z3.shape
    tt = min(t, 512)
    nt = t // tt
    wspec = lambda shape: pl.BlockSpec((None,) + shape, lambda b, i, l: (l[0],) + (0,) * len(shape))
    return pl.pallas_call(
        functools.partial(_lru_kernel, tt=tt, nt=nt),
        grid_spec=pltpu.PrefetchScalarGridSpec(
            num_scalar_prefetch=1, grid=(s, nt),
            in_specs=[pl.BlockSpec((None, tt, W_BR), lambda b, i, l: (b, i, C_ULRU // W_BR)),
                      pl.BlockSpec((None, tt, W_BR), lambda b, i, l: (b, i, C_GLRU // W_BR)),
                      pl.BlockSpec((None, 3, W_BR), lambda b, i, l: (b, 0, 0)),
                      pl.BlockSpec((None, 1, W_BR), lambda b, i, l: (b, 0, 0)),
                      wspec((CONV_K, W_BR)), wspec((1, W_BR)),
                      wspec((LRU_BLOCKS, LRU_BS, LRU_BS)), wspec((1, W_BR)),
                      wspec((LRU_BLOCKS, LRU_BS, LRU_BS)), wspec((1, W_BR)), wspec((1, W_BR))],
            out_specs=[pl.BlockSpec((None, tt, W_BR), lambda b, i, l: (b, i, 0)),
                       pl.BlockSpec((None, 3, W_BR), lambda b, i, l: (b, 0, 0)),
                       pl.BlockSpec((None, 1, W_BR), lambda b, i, l: (b, 0, 0))],
            scratch_shapes=[pltpu.VMEM((tt + 8, W_BR), F32), pltpu.VMEM((tt, W_BR), F32),
                            pltpu.VMEM((tt, W_BR), F32), pltpu.VMEM((1, W_BR), F32)]),
        out_shape=[jax.ShapeDtypeStruct((s, t, W_BR), ydtype),
                   jax.ShapeDtypeStruct((s, 3, W_BR), F32),
                   jax.ShapeDtypeStruct((s, 1, W_BR), F32)],
        compiler_params=_cparams(("parallel", "arbitrary")),
        name="rglru",
    )(lidx, z3, z3, buf, h0, p["lru_conv_w"], p["lru_conv_b"], p["lru_wa"], p["lru_ba"],
      p["lru_wx"], p["lru_bx"], p["lru_lambda"])


S5_CB = LANES // S5_GS
S5_SW = S5_CB * S5_P


def _gelu_tanh(x):
    return 0.5 * x * (1.0 + jnp.tanh(0.7978845608028654 * (x + 0.044715 * (x * x * x))))


def _s5_kernel(l_ref, u_ref, h0re_ref, h0im_ref, wst_ref, vout_ref, kt_ref, alre_ref, alim_ref, d_ref,
               y_ref, hre_ref, him_ref, hin_scr, s_scr, *, n, sb):
    rows = sb * n
    us = [u_ref[pl.ds(j, rows, stride=S5_L), :] for j in range(S5_L)]
    ub = jnp.concatenate(us, axis=1).astype(BF16)
    s = _dot(ub, wst_ref[...])
    alre = alre_ref[...]
    alim = alim_ref[...]
    h0re = h0re_ref[...]
    h0im = h0im_ref[...]
    if n == 1:
        hin_scr[:, :S5_SW] = h0re
        hin_scr[:, S5_SW:] = h0im
        hre = alre * h0re - alim * h0im + s[:, :S5_SW]
        him = alre * h0im + alim * h0re + s[:, S5_SW:]
    else:
        s_scr[...] = s

        def body(c, carry):
            hre, him = carry
            hin_scr[pl.ds(c, 1), :S5_SW] = hre
            hin_scr[pl.ds(c, 1), S5_SW:] = him
            srow = s_scr[pl.ds(c, 1), :]
            return (alre * hre - alim * him + srow[:, :S5_SW],
                    alre * him + alim * hre + srow[:, S5_SW:])

        hre, him = lax.fori_loop(0, n, body, (h0re, h0im))
    hre_ref[...] = hre
    him_ref[...] = him
    ycat = _dot(hin_scr[...].astype(BF16), vout_ref[...]) + _dot(ub, kt_ref[...])
    d = d_ref[...]
    for j in range(S5_L):
        yj = ycat[:, j * LANES:(j + 1) * LANES] + d * us[j]
        y_ref[pl.ds(j, rows, stride=S5_L), :] = _gelu_tanh(yj)


def _s5_scan(lidx, z3, h0re, h0im, p, sb):
    s, t, _ = z3.shape
    n = t // S5_L
    assert sb == 1 or n == 1
    sg = s // sb
    zr = z3.reshape(sg, sb * t, NP)
    ncb = W_BR // LANES
    wspec = lambda shape: pl.BlockSpec((None, None) + shape, lambda b, c, l: (l[0], c) + (0,) * len(shape))
    hspec = pl.BlockSpec((None, sb, S5_SW), lambda b, c, l: (b, 0, c))
    y, hre, him = pl.pallas_call(
        functools.partial(_s5_kernel, n=n, sb=sb),
        grid_spec=pltpu.PrefetchScalarGridSpec(
            num_scalar_prefetch=1, grid=(sg, ncb),
            in_specs=[pl.BlockSpec((None, sb * t, LANES), lambda b, c, l: (b, 0, C_US5 // LANES + c)),
                      hspec, hspec,
                      wspec((S5_L * LANES, 2 * S5_SW)), wspec((2 * S5_SW, S5_L * LANES)),
                      wspec((S5_L * LANES, S5_L * LANES)), wspec((1, S5_SW)), wspec((1, S5_SW)),
                      wspec((1, LANES))],
            out_specs=[pl.BlockSpec((None, sb * t, LANES), lambda b, c, l: (b, 0, c)), hspec, hspec],
            scratch_shapes=[pltpu.VMEM((sb * n, 2 * S5_SW), F32), pltpu.VMEM((sb * n, 2 * S5_SW), F32)]),
        out_shape=[jax.ShapeDtypeStruct((sg, sb * t, W_BR), F32),
                   jax.ShapeDtypeStruct((sg, sb, S5_G * S5_P), F32),
                   jax.ShapeDtypeStruct((sg, sb, S5_G * S5_P), F32)],
        compiler_params=_cparams(("parallel", "arbitrary")),
        name="s5_scan",
    )(lidx, zr, h0re, h0im, p["s5_wst"], p["s5_vout"], p["s5_kt"], p["s5_alre"], p["s5_alim"], p["s5_d"])
    return y.reshape(s * t, W_BR), hre, him


def _s5_glu_kernel(l_ref, y_ref, g_ref, w_ref, o_ref):
    gl = _dot(y_ref[...].astype(BF16), w_ref[...])
    o_ref[...] = (gl[:, :W_BR] * _sigmoid(gl[:, W_BR:]) * _silu(g_ref[...])).astype(o_ref.dtype)


def _s5_glu(lidx, y2d, z2d, p, ydtype):
    n = y2d.shape[0]
    tm = min(n, 512)
    return pl.pallas_call(
        _s5_glu_kernel,
        grid_spec=pltpu.PrefetchScalarGridSpec(
            num_scalar_prefetch=1, grid=(n // tm,),
            in_specs=[pl.BlockSpec((tm, W_BR), lambda i, l: (i, 0)),
                      pl.BlockSpec((tm, W_BR), lambda i, l: (i, C_GS5 // W_BR)),
                      pl.BlockSpec((None, W_BR, 2 * W_BR), lambda i, l: (l[0], 0, 0))],
            out_specs=pl.BlockSpec((tm, W_BR), lambda i, l: (i, 0))),
        out_shape=jax.ShapeDtypeStruct((n, W_BR), ydtype),
        compiler_params=_cparams(("parallel",)),
        name="s5_glu",
    )(lidx, y2d, z2d, p["s5_glu_w"])


def _s5_expand_kernel(wst_ref, vout_ref, kt_ref, wst_o, vout_o, kt_o):
    n = S5_L * LANES
    lp = S5_P.bit_length() - 1
    lc = S5_GS.bit_length() - 1
    lg = S5_CB.bit_length() - 1
    row = _iota((n, n), 0)
    col = _iota((n, n), 1)
    src = _iota((LANES, n), 0)
    dst = _iota((LANES, n), 1)
    rep_state = jnp.where(src == (((dst >> (lp + lg)) << lp) | (dst & (S5_P - 1))), 1.0, 0.0).astype(BF16)
    rep_out = jnp.where(src == (((dst >> (lc + lg)) << lc) | (dst & (S5_GS - 1))), 1.0, 0.0).astype(BF16)
    gi_in_row = (row >> lc) & (S5_CB - 1)
    gi_state_row = (row >> lp) & (S5_CB - 1)
    gi_state_col = (col >> lp) & (S5_CB - 1)
    gi_out_col = (col >> lc) & (S5_CB - 1)
    wst_o[...] = jnp.where(gi_in_row == gi_state_col, _dot(wst_ref[...].astype(BF16), rep_state), 0.0).astype(BF16)
    vout_o[...] = jnp.where(gi_state_row == gi_out_col, _dot(vout_ref[...].astype(BF16), rep_out), 0.0).astype(BF16)
    kt_o[...] = jnp.where(gi_in_row == gi_out_col, _dot(kt_ref[...].astype(BF16), rep_out), 0.0).astype(BF16)


def _s5_expand(wst, vout, kt):
    dd, ncb, n, _ = wst.shape
    cspec = pl.BlockSpec((None, None, n, LANES), lambda d, c: (d, c, 0, 0))
    ospec = pl.BlockSpec((None, None, n, n), lambda d, c: (d, c, 0, 0))
    oshape = jax.ShapeDtypeStruct((dd, ncb, n, n), BF16)
    return pl.pallas_call(
        _s5_expand_kernel, grid=(dd, ncb), in_specs=[cspec, cspec, cspec], out_specs=[ospec, ospec, ospec],
        out_shape=[oshape, oshape, oshape], compiler_params=_cparams(("parallel", "parallel")),
        name="s5_expand",
    )(wst, vout, kt)


def _s5_weights(w):
    dt = jnp.exp(w["s5_log_dt"])[..., None]
    lr, li = w["s5_lam_re"], w["s5_lam_im"]
    mag = jnp.exp(lr * dt)
    a_re = mag * jnp.cos(li * dt)
    a_im = mag * jnp.sin(li * dt)
    den = lr * lr + li * li
    f_re = ((a_re - 1.0) * lr + a_im * li) / den
    f_im = (a_im * lr - (a_re - 1.0) * li) / den
    bb_re = f_re[..., None] * w["s5_b_re"] - f_im[..., None] * w["s5_b_im"]
    bb_im = f_re[..., None] * w["s5_b_im"] + f_im[..., None] * w["s5_b_re"]
    pw_re = [jnp.ones_like(a_re)]
    pw_im = [jnp.zeros_like(a_im)]
    for _ in range(S5_L):
        pr, pi = pw_re[-1], pw_im[-1]
        pw_re.append(pr * a_re - pi * a_im)
        pw_im.append(pr * a_im + pi * a_re)
    pw_re = jnp.stack(pw_re, axis=1)
    pw_im = jnp.stack(pw_im, axis=1)
    dd = lr.shape[0]
    ncb = S5_G // S5_CB
    grp = lambda a: a.reshape(a.shape[0], a.shape[1], ncb, S5_CB, *a.shape[3:])
    rev_re = jnp.stack([pw_re[:, S5_L - 1 - j] for j in range(S5_L)], axis=1)
    rev_im = jnp.stack([pw_im[:, S5_L - 1 - j] for j in range(S5_L)], axis=1)
    st_re = rev_re[..., None] * bb_re[:, None] - rev_im[..., None] * bb_im[:, None]
    st_im = rev_re[..., None] * bb_im[:, None] + rev_im[..., None] * bb_re[:, None]

    def state_w(a):
        return grp(a).transpose(0, 2, 1, 3, 5, 4).reshape(dd, ncb, S5_L * LANES, S5_P)

    wst = jnp.concatenate([state_w(st_re), state_w(st_im)], axis=-1)
    c_re, c_im = w["s5_c_re"], w["s5_c_im"]
    nx_re, nx_im = pw_re[:, 1:], pw_im[:, 1:]
    ca_re = c_re[:, None] * nx_re[:, :, :, None] - c_im[:, None] * nx_im[:, :, :, None]
    ca_im = c_re[:, None] * nx_im[:, :, :, None] + c_im[:, None] * nx_re[:, :, :, None]

    def out_w(a):
        return grp(a).transpose(0, 2, 3, 5, 1, 4).reshape(dd, ncb, S5_SW, S5_L * S5_GS)

    vout = jnp.concatenate([out_w(ca_re), out_w(-ca_im)], axis=2)
    cat_re = c_re[:, None] * pw_re[:, :S5_L, :, None] - c_im[:, None] * pw_im[:, :S5_L, :, None]
    cat_im = c_re[:, None] * pw_im[:, :S5_L, :, None] + c_im[:, None] * pw_re[:, :S5_L, :, None]
    kk = (jnp.einsum("dlgcp,dgpe->dlgce", cat_re, bb_re, precision=HI)
          - jnp.einsum("dlgcp,dgpe->dlgce", cat_im, bb_im, precision=HI))
    zero = jnp.zeros_like(kk[:, 0])
    kt = jnp.stack([jnp.stack([kk[:, j - i] if j >= i else zero for j in range(S5_L)], axis=1)
                    for i in range(S5_L)], axis=1)
    kt = kt.reshape(dd, S5_L, S5_L, ncb, S5_CB, S5_GS, S5_GS)
    kt = kt.transpose(0, 3, 1, 4, 6, 2, 5).reshape(dd, ncb, S5_L * LANES, S5_L * S5_GS)
    wst, vout, kt = _s5_expand(wst, vout, kt)
    sw = lambda a: a.reshape(dd, ncb, 1, S5_SW)
    return dict(s5_wst=wst, s5_vout=vout, s5_kt=kt, s5_alre=sw(pw_re[:, S5_L]), s5_alim=sw(pw_im[:, S5_L]),
                s5_d=w["s5_d"].reshape(dd, W_BR // LANES, 1, LANES), s5_glu_w=w["s5_glu_w"].astype(BF16))


def _unit_lower_solve(ms, rhs, c):
    def split(a):
        hi = a.astype(BF16)
        return hi, (a - hi.astype(F32)).astype(BF16)

    def dot3(a, b):
        return _dot(a[0], b[0]) + _dot(a[1], b[0]) + _dot(a[0], b[1])

    n = range(len(ms))
    ms = [split(m) for m in ms]
    rs = [split(r) for r in rhs]
    xs = [rhs[i] - dot3(ms[i], rs[i]) for i in n]
    k = 2
    while k < c:
        if k <= 2 ** GDN_SPLIT_LEVELS:
            ms = [split(dot3(ms[i], ms[i])) for i in n]
            rs = [split(x) for x in xs]
            xs = [xs[i] + dot3(ms[i], rs[i]) for i in n]
        else:
            ms = [(_dot(ms[i][0], ms[i][0]).astype(BF16), None) for i in n]
            xs = [xs[i] + _dot(ms[i][0], xs[i].astype(BF16)) for i in n]
        k *= 2
    return xs


def _gdn_kernel(l_ref, q_ref, k_ref, v_ref, sm_ref, gg_ref, bq_ref, bk_ref, bv_ref, cwq_ref, cwk_ref, cwv_ref,
                alog_ref, dtb_ref, nw_ref, s0_ref, y_ref, bufo_ref, so_ref,
                xp_scr, qkv_scr, gx_scr, bx_scr, gc_scr, s_scr, *, tt, nt, c):
    ti = pl.program_id(1)
    log2c = c.bit_length() - 1

    @pl.when(ti == 0)
    def _():
        for i, b_ref in enumerate((bq_ref, bk_ref, bv_ref)):
            xp_scr[i, 5:8, :] = b_ref[...]
        s_scr[...] = s0_ref[...]

    for i, (x_ref, cw_ref) in enumerate(((q_ref, cwq_ref), (k_ref, cwk_ref), (v_ref, cwv_ref))):
        x = x_ref[...]
        xp_scr[i, 8:8 + tt, :] = x
        cw = cw_ref[...]
        cv = (cw[3:4] * x + cw[2:3] * xp_scr[i, 7:7 + tt, :]
              + cw[1:2] * xp_scr[i, 6:6 + tt, :] + cw[0:1] * xp_scr[i, 5:5 + tt, :])
        tail = x[tt - 3:tt, :]
        xp_scr[i, 5:8, :] = tail
        bufo_ref[:, i * W_BR:(i + 1) * W_BR] = tail
        cv = _silu(cv)
        if i < 2:
            scale = GDN_HD ** -0.5 if i == 0 else 1.0
            for h in range(GDN_H):
                sl = slice(h * GDN_HD, (h + 1) * GDN_HD)
                xh = cv[:, sl]
                qkv_scr[i, :, sl] = xh * (lax.rsqrt(jnp.sum(xh * xh, axis=-1, keepdims=True) + 1e-6) * scale)
        else:
            qkv_scr[i] = cv

    sm = sm_ref[...]
    gsm = -jnp.exp(alog_ref[...]) * _softplus(sm + dtb_ref[...])
    bsm = _sigmoid(sm)
    src = _iota((LANES, W_BR), 0)
    head = _iota((LANES, W_BR), 1) >> 7
    gx_scr[...] = _dot_hi(gsm, jnp.where(src - SM_A == head, 1.0, 0.0))
    bx_scr[...] = _dot_hi(bsm, jnp.where(src - SM_B == head, 1.0, 0.0))
    headc = _iota((LANES, GDN_H * c), 1) >> log2c
    gc_scr[...] = _dot_hi(gsm, jnp.where(_iota((LANES, GDN_H * c), 0) - SM_A == headc, 1.0, 0.0))

    rowi = _iota((c, c), 0)
    coli = _iota((c, c), 1)
    incl = coli <= rowi
    strict = coli < rowi
    ltri = jnp.where(incl, 1.0, 0.0)
    upper = jnp.where(_iota((c, GDN_H * c), 0) > (_iota((c, GDN_H * c), 1) & (c - 1)), 1.0, 0.0)
    nw = nw_ref[...]

    def chunk(ci, carry):
        r0 = pl.multiple_of(ci * c, c)
        gcb_all = _dot_hi(ltri, gx_scr[pl.ds(r0, c), :])
        diffs = _dot_hi(ltri, gc_scr[pl.ds(r0, c), :] * upper)
        beta_all = bx_scr[pl.ds(r0, c), :]
        hs = range(GDN_H)
        sls = [slice(h * GDN_HD, (h + 1) * GDN_HD) for h in hs]
        q = [qkv_scr[0, pl.ds(r0, c), sl] for sl in sls]
        k = [qkv_scr[1, pl.ds(r0, c), sl] for sl in sls]
        gcb = [gcb_all[:, sl] for sl in sls]
        decay = [jnp.where(incl, jnp.exp(diffs[:, h * c:(h + 1) * c]), 0.0) for h in hs]
        kb = [k[h] * beta_all[:, sls[h]] for h in hs]
        kbf = [x.astype(BF16) for x in k]
        kq = [_dot_nt(jnp.concatenate([kb[h], q[h]], axis=0).astype(BF16), kbf[h]) for h in hs]
        m = [jnp.where(strict, kq[h][:c] * decay[h], 0.0) for h in hs]
        qk = [(kq[h][c:] * decay[h]).astype(BF16) for h in hs]
        rhs = [jnp.concatenate([qkv_scr[2, pl.ds(r0, c), sls[h]] * beta_all[:, sls[h]], kb[h] * jnp.exp(gcb[h])],
                               axis=1) for h in hs]
        sol = _unit_lower_solve(m, rhs, c)
        s = [s_scr[h] for h in hs]
        sb = [x.astype(BF16) for x in s]
        ws = [_dot(jnp.concatenate([sol[h][:, GDN_HD:], q[h] * jnp.exp(gcb[h])], axis=0).astype(BF16), sb[h])
              for h in hs]
        vnb = [(sol[h][:, :GDN_HD] - ws[h][:c]).astype(BF16) for h in hs]
        o = [ws[h][c:] + _dot(qk[h], vnb[h]) for h in hs]
        for h in hs:
            glast = gcb[h][c - 1:c, :]
            s_scr[h] = s[h] * jnp.exp(glast) + _dot_tn((k[h] * jnp.exp(glast - gcb[h])).astype(BF16), vnb[h])
        for h in hs:
            on = o[h] * lax.rsqrt(jnp.mean(o[h] * o[h], axis=-1, keepdims=True) + 1e-6) * nw
            y_ref[pl.ds(r0, c), sls[h]] = (on * _silu(gg_ref[pl.ds(r0, c), sls[h]])).astype(y_ref.dtype)
        return carry

    lax.fori_loop(0, tt // c, chunk, 0)

    @pl.when(ti == nt - 1)
    def _():
        so_ref[...] = s_scr[...]


def _gdn(lidx, z3, buf, s0, p, ydtype, chunk):
    s, t, _ = z3.shape
    tt = min(t, 256)
    nt = t // tt
    c = min(chunk, tt)
    zspec = lambda col: pl.BlockSpec((None, tt, W_BR), lambda b, i, l: (b, i, col // W_BR))
    bspec = lambda j: pl.BlockSpec((None, 3, W_BR), lambda b, i, l: (b, 0, j))
    cspec = lambda j: pl.BlockSpec((None, CONV_K, W_BR), lambda b, i, l: (l[0], 0, j))
    rspec = pl.BlockSpec((None, 1, LANES), lambda b, i, l: (l[0], 0, 0))
    sspec = pl.BlockSpec((None, GDN_H, GDN_HD, GDN_HD), lambda b, i, l: (b, 0, 0, 0))
    return pl.pallas_call(
        functools.partial(_gdn_kernel, tt=tt, nt=nt, c=c),
        grid_spec=pltpu.PrefetchScalarGridSpec(
            num_scalar_prefetch=1, grid=(s, nt),
            in_specs=[zspec(C_QKV), zspec(C_QKV + W_BR), zspec(C_QKV + 2 * W_BR),
                      pl.BlockSpec((None, tt, LANES), lambda b, i, l: (b, i, C_SMALL_B // LANES)),
                      zspec(C_GGDN), bspec(0), bspec(1), bspec(2), cspec(0), cspec(1), cspec(2),
                      rspec, rspec, rspec, sspec],
            out_specs=[pl.BlockSpec((None, tt, W_BR), lambda b, i, l: (b, i, 0)),
                       pl.BlockSpec((None, 3, 3 * W_BR), lambda b, i, l: (b, 0, 0)),
                       sspec],
            scratch_shapes=[pltpu.VMEM((3, tt + 8, W_BR), F32), pltpu.VMEM((3, tt, W_BR), F32),
                            pltpu.VMEM((tt, W_BR), F32), pltpu.VMEM((tt, W_BR), F32),
                            pltpu.VMEM((tt, GDN_H * c), F32), pltpu.VMEM((GDN_H, GDN_HD, GDN_HD), F32)]),
        out_shape=[jax.ShapeDtypeStruct((s, t, W_BR), ydtype),
                   jax.ShapeDtypeStruct((s, 3, 3 * W_BR), F32),
                   jax.ShapeDtypeStruct((s, GDN_H, GDN_HD, GDN_HD), F32)],
        compiler_params=_cparams(("parallel", "arbitrary")),
        name="gdn",
    )(lidx, z3, z3, z3, z3, z3, buf, buf, buf, p["gdn_conv_w"], p["gdn_conv_w"], p["gdn_conv_w"],
      p["gdn_a_log"], p["gdn_dt_bias"], p["gdn_norm_w"], s0)


PAGES_PER_STEP = 4


def _gather_kernel(l_ref, pt_ref, p0_ref, p1_ref, p2_ref, p3_ref, zn_ref, cmp_ref, sel_ref, *, n_past):
    i = pl.program_id(1)
    half = 2 * NSA_KVH * NSA_HD

    @pl.when(i < n_past)
    def _():
        pages = (p0_ref, p1_ref, p2_ref, p3_ref)
        per_tok = 4 * NSA_KVH
        for k, p_ref in enumerate(pages):
            rows = slice(k * PAGE_SIZE, (k + 1) * PAGE_SIZE)
            for r in (2, 3):
                for g in range(NSA_KVH):
                    x = p_ref[pl.ds(r * NSA_KVH + g, PAGE_SIZE, stride=per_tok), :]
                    cols = slice(((r - 2) * NSA_KVH + g) * NSA_HD, ((r - 2) * NSA_KVH + g + 1) * NSA_HD)
                    sel_ref[rows, cols] = x.astype(BF16)
        halves = PAGE_SIZE // CMP_STRIDE
        for r in (0, 1):
            for g in range(NSA_KVH):
                flat = [jnp.concatenate([p_ref[pl.ds(s * per_tok + r * NSA_KVH + g, halves,
                                                     stride=CMP_STRIDE * per_tok), :] for p_ref in pages], axis=0)
                        for s in range(CMP_STRIDE)]
                cmp_ref[r, g] = jnp.concatenate(flat, axis=1).astype(BF16)

    @pl.when(i == n_past)
    def _():
        sel_ref[...] = jnp.zeros(sel_ref.shape, BF16)
        new = zn_ref[:, half:2 * half]
        sel_ref[0:2 * SUBLANES, :] = jnp.concatenate([new, jnp.zeros_like(new)], axis=0).astype(BF16)


def _gather_ctx(lidx, pt_flat, cache, z3):
    b, t_new, _ = z3.shape
    assert t_new == SUBLANES
    n_pages = pt_flat.shape[0] // b
    n_past = n_pages // PAGES_PER_STEP
    rows = PAGES_PER_STEP * PAGE_SIZE
    width = 4 * NSA_KVH * NSA_HD
    half = width // 2
    n_pool = cache.shape[1]
    cache = cache.reshape(cache.shape[0] * n_pool, PAGE_SIZE * 4 * NSA_KVH, NSA_HD)

    def pspec(k):
        def imap(bi, i, l, pt):
            page = jnp.minimum(i * PAGES_PER_STEP + k, n_pages - 1)
            return (l[0] * n_pool + pt[bi * n_pages + page], 0, 0)
        return pl.BlockSpec((None, PAGE_SIZE * 4 * NSA_KVH, NSA_HD), imap)

    return pl.pallas_call(
        functools.partial(_gather_kernel, n_past=n_past),
        grid_spec=pltpu.PrefetchScalarGridSpec(
            num_scalar_prefetch=2, grid=(b, n_past + 1),
            in_specs=[pspec(k) for k in range(PAGES_PER_STEP)]
            + [pl.BlockSpec((None, t_new, width), lambda bi, i, l, pt: (bi, 0, C_KV // width))],
            out_specs=[pl.BlockSpec((None, 2, NSA_KVH, rows // CMP_STRIDE, CMP_STRIDE * NSA_HD),
                                    lambda bi, i, l, pt: (bi, 0, 0, jnp.minimum(i, n_past - 1), 0)),
                       pl.BlockSpec((None, rows, half), lambda bi, i, l, pt: (bi, i, 0))]),
        out_shape=[jax.ShapeDtypeStruct((b, 2, NSA_KVH, n_past * rows // CMP_STRIDE, CMP_STRIDE * NSA_HD), BF16),
                   jax.ShapeDtypeStruct((b, (n_past + 1) * rows, half), BF16)],
        compiler_params=_cparams(("parallel", "arbitrary")),
        name="nsa_gather",
    )(lidx, pt_flat, cache, cache, cache, cache, z3)


def _cmp_bias_kernel(pe_ref, w1_ref, o_ref):
    o_ref[...] = _dot_hi(pe_ref[...], w1_ref[...].astype(F32))


def _cmp_bias(pe8, w1):
    dd, _, flat, wide = w1.shape
    return pl.pallas_call(
        _cmp_bias_kernel, grid=(dd, 2),
        in_specs=[pl.BlockSpec((None, SUBLANES, flat), lambda d, c: (d, 0, 0)),
                  pl.BlockSpec((None, None, flat, wide), lambda d, c: (d, c, 0, 0))],
        out_specs=pl.BlockSpec((None, None, SUBLANES, wide), lambda d, c: (d, c, 0, 0)),
        out_shape=jax.ShapeDtypeStruct((dd, 2, SUBLANES, wide), F32),
        compiler_params=_cparams(("parallel", "parallel")),
        name="nsa_cmp_bias",
    )(pe8, w1)


def _cmp_kernel(l_ref, x0_ref, x1_ref, w1_ref, bias_ref, w2_ref, o_ref, carry_scr, *, nh, flat):
    @pl.when(pl.program_id(2) == 0)
    def _():
        carry_scr[...] = jnp.zeros(carry_scr.shape, F32)

    w1 = w1_ref[...]
    bias = bias_ref[...]
    last = _iota((nh, NSA_HD), 0) == nh - 1
    for g, x_ref in enumerate((x0_ref, x1_ref)):
        if flat:
            ucat = x_ref[...]
        else:
            ucat = jnp.concatenate([x_ref[pl.ds(s, nh, stride=CMP_STRIDE), :] for s in range(CMP_STRIDE)], axis=1)
        hh = _dot(ucat.astype(BF16), w1)
        hf = hh[:, :NSA_HD] + bias[0:1, :NSA_HD]
        hs = hh[:, NSA_HD:] + bias[1:2, NSA_HD:]
        hs_next = jnp.where(last, carry_scr[g], pltpu.roll(hs, nh - 1, axis=0))
        carry_scr[g] = hs[0:1, :]
        hid = _silu(hf + hs_next)
        o_ref[g] = _dot(hid.astype(BF16), w2_ref[...]).astype(BF16)


def _compress(lidx, src, col0, n_rows, p):
    b = src.shape[0]
    tr = min(n_rows, 4096)
    nt = n_rows // tr
    nh = tr // CMP_STRIDE
    flat = col0 is None
    if flat:
        xspec = lambda g: pl.BlockSpec((None, None, None, nh, CMP_STRIDE * NSA_HD),
                                       lambda bi, c, i, l: (bi, c, g, nt - 1 - i, 0))
    else:
        xspec = lambda g: pl.BlockSpec((None, tr, NSA_HD),
                                       lambda bi, c, i, l: (bi, nt - 1 - i, col0 + NSA_KVH * c + g))
    return pl.pallas_call(
        functools.partial(_cmp_kernel, nh=nh, flat=flat),
        grid_spec=pltpu.PrefetchScalarGridSpec(
            num_scalar_prefetch=1, grid=(b, 2, nt),
            in_specs=[xspec(0), xspec(1),
                      pl.BlockSpec((None, None, CMP_STRIDE * NSA_HD, 2 * NSA_HD), lambda bi, c, i, l: (l[0], c, 0, 0)),
                      pl.BlockSpec((None, None, SUBLANES, 2 * NSA_HD), lambda bi, c, i, l: (l[0], c, 0, 0)),
                      pl.BlockSpec((None, None, NSA_HD, NSA_HD), lambda bi, c, i, l: (l[0], c, 0, 0))],
            out_specs=pl.BlockSpec((None, None, NSA_KVH, nh, NSA_HD), lambda bi, c, i, l: (bi, c, 0, nt - 1 - i, 0)),
            scratch_shapes=[pltpu.VMEM((NSA_KVH, 1, NSA_HD), F32)]),
        out_shape=jax.ShapeDtypeStruct((b, 2, NSA_KVH, n_rows // CMP_STRIDE, NSA_HD), BF16),
        compiler_params=_cparams(("parallel", "parallel", "arbitrary")),
        name="nsa_compress",
    )(lidx, src, src, p["nsa_w1"], p["nsa_bias"], p["nsa_w2"])


def _masked_softmax(s, valid):
    s = jnp.where(valid, s, NEG)
    m = jnp.max(s, axis=-1, keepdims=True)
    e = jnp.where(valid, jnp.exp2(s - m), 0.0)
    den = jnp.sum(e, axis=-1, keepdims=True)
    return e / jnp.where(den > 0.0, den, 1.0)


def _attn_kernel(l_ref, q_ref, sm_ref, gn_ref, kc_ref, vc_ref, ks_ref, vs_ref, kw_ref, vw_ref, y_ref, *,
                 qb, pos0, ncp, ns, nsp, kt, wn, wpos0, tw):
    g = pl.program_id(1)
    q0 = pl.program_id(2) * qb
    qpos0 = pos0 + q0
    hd = NSA_HD
    q = q_ref[...] * (hd ** -0.5 * LOG2E)
    qr = jnp.concatenate([q[:, j * hd:(j + 1) * hd] for j in range(NSA_GQ)], axis=0).astype(BF16)
    slope_g = jnp.where(g == 0, LOG2E, LOG2E * 2.0 ** -NSA_GQ)
    slopes = [slope_g * 2.0 ** -(j + 1) for j in range(NSA_GQ)]
    heads = lambda a: [a[j * qb:(j + 1) * qb] for j in range(NSA_GQ)]

    dist = (qpos0 + _iota((qb, ncp), 0)) - (_iota((qb, ncp), 1) * CMP_STRIDE + (CMP_BLOCK - 1))
    valid = dist >= 0
    distf = dist.astype(F32)
    sc = heads(_dot_nt(qr, kc_ref[...]))
    pc = [_masked_softmax(sc[j] - slopes[j] * distf, valid) for j in range(NSA_GQ)]
    o_c = heads(_dot(jnp.concatenate(pc, axis=0).astype(BF16), vc_ref[...]))
    imp = pc[0] + pc[1] + pc[2] + pc[3]
    pool = jnp.where((_iota((ncp, nsp), 0) >> 2) == _iota((ncp, nsp), 1), 1.0, 0.0)
    imp = _dot_hi(imp, pool)

    blk = _iota((qb, nsp), 1)
    qp = qpos0 + _iota((qb, nsp), 0)
    val = jnp.where(blk * SEL_BLOCK > qp, -FORCE, imp)
    val = jnp.where(blk == (qp >> 6), FORCE, jnp.where(blk == 0, FORCE, val))
    val = jnp.where(blk >= ns, -3.0 * FORCE, val)
    top = float(min(TOP_N, ns))
    if qb == LANES and nsp == LANES:
        nsr = -(-ns // SUBLANES) * SUBLANES
        val_t = val.T[:nsr]
        blk_t = _iota((nsr, qb), 0)
        rank = jnp.zeros((nsr, qb), F32)
        for bidx in range(ns):
            cand = val_t[bidx:bidx + 1, :]
            rank = rank + jnp.where(cand > val_t, 1.0, jnp.where(cand == val_t, jnp.where(blk_t > bidx, 1.0, 0.0), 0.0))
        sel_t = jnp.where(rank < top, 1.0, 0.0)
        selb = jnp.concatenate([sel_t, jnp.zeros((nsp - nsr, qb), F32)], axis=0).T.astype(BF16)
    else:
        rank = jnp.zeros((qb, nsp), F32)
        for bidx in range(ns):
            cand = val[:, bidx:bidx + 1]
            rank = rank + jnp.where(cand > val, 1.0, jnp.where(cand == val, jnp.where(blk > bidx, 1.0, 0.0), 0.0))
        selb = jnp.where(rank < top, 1.0, 0.0).astype(BF16)

    n_tiles = (qpos0 + qb + kt - 1) // kt
    qh = heads(qr)
    spread = jnp.where(_iota((LANES, kt), 0) == (_iota((LANES, kt), 1) >> 6), 1.0, 0.0).astype(BF16)

    def tile_bias(t):
        k0 = t * kt
        d = (qpos0 + _iota((qb, kt), 0)) - (k0 + _iota((qb, kt), 1))
        pick_blk = jnp.where(_iota((nsp, LANES), 0) == (k0 >> 6) + _iota((nsp, LANES), 1), 1.0, 0.0).astype(BF16)
        sel_tile = _dot(selb, pick_blk).astype(BF16)
        ok = jnp.where(d >= 0, _dot(sel_tile, spread), 0.0) > 0.5
        return jnp.where(ok, d.astype(F32), -NEG)

    def tile(t, carry):
        k0 = pl.multiple_of(t * kt, kt)
        kk = ks_ref[pl.ds(k0, kt), :].astype(BF16)
        vv = vs_ref[pl.ds(k0, kt), :].astype(BF16)
        per_head = qb == Q_BLOCK
        s = [_dot_nt(qh[j], kk) for j in range(NSA_GQ)] if per_head else heads(_dot_nt(qr, kk))
        bias = carry[NSA_GQ]
        bias_next = tile_bias(t + 1)
        stats, prs = [], []
        for j in range(NSA_GQ):
            m, lsum, _ = carry[j]
            sm = s[j] - slopes[j] * bias
            m_new = jnp.maximum(m, jnp.max(sm, axis=-1, keepdims=True))
            pr = jnp.exp2(sm - m_new)
            alpha = jnp.exp2(m - m_new)
            stats.append((m_new, alpha * lsum + jnp.sum(pr, axis=-1, keepdims=True), alpha))
            prs.append(pr.astype(BF16))
        pv = [_dot(pr, vv) for pr in prs] if per_head else heads(_dot(jnp.concatenate(prs, axis=0), vv))
        new = [(stats[j][0], stats[j][1], stats[j][2] * carry[j][2] + pv[j]) for j in range(NSA_GQ)]
        return tuple(new) + (bias_next,)

    init = (jnp.full((qb, 1), NEG, F32), jnp.zeros((qb, 1), F32), jnp.zeros((qb, hd), F32))
    fin = lax.fori_loop(0, n_tiles, tile, (init,) * NSA_GQ + (tile_bias(0),))
    o_s = [acc / lsum for _, lsum, acc in fin[:NSA_GQ]]

    k0w = pl.multiple_of(jnp.clip(q0 - WINDOW, 0, tw - wn), SUBLANES)
    dw = (qpos0 + _iota((qb, wn), 0)) - (wpos0 + k0w + _iota((qb, wn), 1))
    okw = jnp.abs(2 * dw - (WINDOW - 1)) < WINDOW
    dwf = dw.astype(F32)
    sw = heads(_dot_nt(qr, kw_ref[pl.ds(k0w, wn), :].astype(BF16)))
    pw = [_masked_softmax(sw[j] - slopes[j] * dwf, okw) for j in range(NSA_GQ)]
    o_w = heads(_dot(jnp.concatenate(pw, axis=0).astype(BF16), vw_ref[pl.ds(k0w, wn), :].astype(BF16)))

    gate = _sigmoid(sm_ref[...])
    lane = _iota((qb, LANES), 1)
    pick = lambda idx: jnp.sum(jnp.where(lane == idx, gate, 0.0), axis=-1, keepdims=True)
    for j in range(NSA_GQ):
        base = SM_GL + 3 * (g * NSA_GQ + j)
        o = pick(base) * o_c[j] + pick(base + 1) * o_s[j] + pick(base + 2) * o_w[j]
        y_ref[:, j * hd:(j + 1) * hd] = (o * _silu(gn_ref[:, j * hd:(j + 1) * hd])).astype(y_ref.dtype)


def _attention(lidx, z3, cmp_kv, ks_src, ks_col, vs_col, kw_src, kw_col, vw_col, pos0, wpos0, ydtype):
    b, t, _ = z3.shape
    qb = min(Q_BLOCK, t)
    ncp = cmp_kv.shape[3]
    tk = ks_src.shape[1]
    tw = kw_src.shape[1]
    if qb == Q_BLOCK:
        kt = 512
    else:
        kt = max(k for k in range(512, min(tk, LANES * SEL_BLOCK) + 1, 512) if tk % k == 0)
    ns = -(-(pos0 + t) // SEL_BLOCK)
    nsp = -(-ns // LANES) * LANES
    wn = WINDOW + Q_BLOCK
    assert tk % kt == 0 and tk >= pos0 + t and tw >= wn and ncp // CMP_PER_SEL <= nsp
    hw = NSA_GQ * NSA_HD
    kern = functools.partial(_attn_kernel, qb=qb, pos0=pos0, ncp=ncp, ns=ns, nsp=nsp, kt=kt, wn=wn, wpos0=wpos0, tw=tw)
    kvspec = lambda rows, col: pl.BlockSpec((None, rows, NSA_HD), lambda bi, g, i, l: (bi, 0, col + g))
    cspec = lambda c: pl.BlockSpec((None, None, None, ncp, NSA_HD), lambda bi, g, i, l: (bi, c, g, 0, 0))
    return pl.pallas_call(
        kern,
        grid_spec=pltpu.PrefetchScalarGridSpec(
            num_scalar_prefetch=1, grid=(b, NSA_KVH, t // qb),
            in_specs=[pl.BlockSpec((None, qb, hw), lambda bi, g, i, l: (bi, i, C_QNSA // hw + g)),
                      pl.BlockSpec((None, qb, LANES), lambda bi, g, i, l: (bi, i, C_SMALL_A // LANES)),
                      pl.BlockSpec((None, qb, hw), lambda bi, g, i, l: (bi, i, C_GNSA // hw + g)),
                      cspec(0), cspec(1),
                      kvspec(tk, ks_col), kvspec(tk, vs_col), kvspec(tw, kw_col), kvspec(tw, vw_col)],
            out_specs=pl.BlockSpec((None, qb, hw), lambda bi, g, i, l: (bi, i, g))),
        out_shape=jax.ShapeDtypeStruct((b, t, W_BR), ydtype),
        compiler_params=_cparams(("parallel", "parallel", "arbitrary")),
        name="nsa_attention",
    )(lidx, z3, z3, z3, cmp_kv, cmp_kv, ks_src, ks_src, kw_src, kw_src)


def _merge_kernel(l_ref, y0_ref, y1_ref, y2_ref, y3_ref, g0_ref, g1_ref, g2_ref, g3_ref, w_ref, o_ref):
    acc = None
    for m, (y_ref, g_ref) in enumerate(((y0_ref, g0_ref), (y1_ref, g1_ref), (y2_ref, g2_ref), (y3_ref, g3_ref))):
        term = _sigmoid(g_ref[...]) * _dot(y_ref[...].astype(BF16), w_ref[m])
        acc = term if acc is None else acc + term
    o_ref[...] = acc.astype(o_ref.dtype)


def _merge(lidx, ys, z2d, p):
    n = z2d.shape[0]
    tm = min(n, 256)
    yspec = pl.BlockSpec((tm, W_BR), lambda i, l: (i, 0))
    gspec = lambda m: pl.BlockSpec((tm, D_MODEL), lambda i, l: (i, C_MERGE // D_MODEL + m))
    wspec = pl.BlockSpec((None, N_BRANCH, W_BR, D_MODEL), lambda i, l: (l[0], 0, 0, 0), pipeline_mode=pl.Buffered(1))
    return pl.pallas_call(
        _merge_kernel,
        grid_spec=pltpu.PrefetchScalarGridSpec(
            num_scalar_prefetch=1, grid=(n // tm,),
            in_specs=[yspec] * N_BRANCH + [gspec(m) for m in range(N_BRANCH)] + [wspec],
            out_specs=pl.BlockSpec((tm, D_MODEL), lambda i, l: (i, 0))),
        out_shape=jax.ShapeDtypeStruct((n, D_MODEL), BF16),
        compiler_params=_cparams(("parallel",)),
        name="merge",
    )(lidx, *ys, z2d, z2d, z2d, z2d, p["w_branch"])


def _outproj_kernel(l_ref, m_ref, x_ref, w_ref, g_ref, b_ref, o_ref, *, alpha):
    v = alpha * x_ref[...] + _dot(m_ref[...], w_ref[...])
    mu = jnp.mean(v, axis=-1, keepdims=True)
    c = v - mu
    var = jnp.mean(c * c, axis=-1, keepdims=True)
    o_ref[...] = c * lax.rsqrt(var + 1e-5) * g_ref[...] + b_ref[...]


def _outproj(lidx, merged, x2d, p, alpha):
    n = x2d.shape[0]
    tm = min(n, 256)
    return pl.pallas_call(
        functools.partial(_outproj_kernel, alpha=alpha),
        grid_spec=pltpu.PrefetchScalarGridSpec(
            num_scalar_prefetch=1, grid=(n // tm,),
            in_specs=[pl.BlockSpec((tm, D_MODEL), lambda i, l: (i, 0)),
                      pl.BlockSpec((tm, D_MODEL), lambda i, l: (i, 0)),
                      pl.BlockSpec((None, D_MODEL, D_MODEL), lambda i, l: (l[0], 0, 0)),
                      pl.BlockSpec((None, 1, D_MODEL), lambda i, l: (l[0], 0, 0)),
                      pl.BlockSpec((None, 1, D_MODEL), lambda i, l: (l[0], 0, 0))],
            out_specs=pl.BlockSpec((tm, D_MODEL), lambda i, l: (i, 0))),
        out_shape=jax.ShapeDtypeStruct((n, D_MODEL), F32),
        compiler_params=_cparams(("parallel",)),
        name="outproj_ln",
    )(lidx, merged, x2d, p["w_out"], p["ln_g"], p["ln_b"])


def _prep_params(w):
    p = {}
    row = lambda a: a[:, None, :]
    p["lru_conv_w"] = w["lru_conv_w"]
    p["lru_conv_b"] = row(w["lru_conv_b"])
    p["lru_wa"] = w["lru_wa"].astype(BF16)
    p["lru_ba"] = row(w["lru_ba"])
    p["lru_wx"] = w["lru_wx"].astype(BF16)
    p["lru_bx"] = row(w["lru_bx"])
    p["lru_lambda"] = row(w["lru_lambda"])
    p.update(_s5_weights(w))
    p["w_branch"] = w["w_branch"].astype(BF16)
    p["w_out"] = w["w_out"].astype(BF16)
    p["ln_g"] = row(w["ln_g"])
    p["ln_b"] = row(w["ln_b"])
    dd = w["gdn_a_log"].shape[0]
    lane_row = lambda a, off: jnp.zeros((dd, 1, LANES), F32).at[:, 0, off:off + a.shape[-1]].set(a)
    p["gdn_conv_w"] = w["gdn_conv_w"]
    p["gdn_a_log"] = lane_row(w["gdn_a_log"], SM_A)
    p["gdn_dt_bias"] = lane_row(w["gdn_dt_bias"], SM_A)
    p["gdn_norm_w"] = row(w["gdn_norm_w"])
    flat = CMP_STRIDE * NSA_HD
    w1 = w["nsa_cmp_w1"].reshape(dd, 2, 2, flat, NSA_HD).transpose(0, 1, 3, 2, 4)
    p["nsa_w1"] = w1.reshape(dd, 2, flat, 2 * NSA_HD).astype(BF16)
    pe = w["nsa_cmp_pos"].reshape(dd, 2, flat)
    p["nsa_bias"] = _cmp_bias(jnp.concatenate([pe, jnp.zeros((dd, SUBLANES - 2, flat), F32)], axis=1), p["nsa_w1"])
    p["nsa_w2"] = w["nsa_cmp_w2"].astype(BF16)
    return p


def _mixer_layer(lidx, x2d, s, t, state, nsa_branch, p, win, alpha, ydtype):
    lru_buf, lru_h, gdn_buf, gdn_s, s5_re, s5_im = state
    z = _inproj(lidx, win[0], x2d, win[1])
    z3 = z.reshape(s, t, NP)
    y_lru, lru_buf, lru_h = _lru(lidx, z3, lru_buf, lru_h, p, ydtype)
    y_nsa = nsa_branch(z3)
    y_gdn, gdn_buf, gdn_s = _gdn(lidx, z3, gdn_buf, gdn_s, p, ydtype, GDN_CHUNK)
    y_s5, s5_re, s5_im = _s5_scan(lidx, z3, s5_re, s5_im, p, s5_re.shape[1])
    y_s5 = _s5_glu(lidx, y_s5, z, p, ydtype)
    flat = lambda y: y.reshape(s * t, W_BR)
    merged = _merge(lidx, (flat(y_lru), flat(y_nsa), flat(y_gdn), y_s5), z, p)
    x_new = _outproj(lidx, merged, x2d, p, alpha)
    return x_new, z3, (lru_buf, lru_h, gdn_buf, gdn_s, s5_re, s5_im)


def kernel(x_prompt, x_sample, state_lru_h, state_lru_conv, cache_nsa_kv, cache_win_kv, state_gdn_s, state_gdn_conv, state_s5_re, state_s5_im, page_table, w_in, lru_conv_w, lru_conv_b, lru_wa, lru_ba, lru_wx, lru_bx, lru_lambda, nsa_cmp_pos, nsa_cmp_w1, nsa_cmp_w2, gdn_conv_w, gdn_a_log, gdn_dt_bias, gdn_norm_w, s5_lam_re, s5_lam_im, s5_log_dt, s5_b_re, s5_b_im, s5_c_re, s5_c_im, s5_d, s5_glu_w, w_branch, w_out, ln_g, ln_b):
    depth = w_in.shape[0]
    bp, tp, _ = x_prompt.shape
    db, ts, _ = x_sample.shape
    n_pages = page_table.shape[1]
    past = n_pages * PAGE_SIZE
    wbuf = cache_win_kv.shape[2]
    alpha = (2.0 * depth) ** 0.25
    kvw = 4 * NSA_KVH * NSA_HD
    winw = 2 * NSA_KVH * NSA_HD
    p = _prep_params(dict(
        lru_conv_w=lru_conv_w, lru_conv_b=lru_conv_b, lru_wa=lru_wa, lru_ba=lru_ba, lru_wx=lru_wx, lru_bx=lru_bx,
        lru_lambda=lru_lambda, nsa_cmp_pos=nsa_cmp_pos, nsa_cmp_w1=nsa_cmp_w1, nsa_cmp_w2=nsa_cmp_w2,
        gdn_conv_w=gdn_conv_w, gdn_a_log=gdn_a_log, gdn_dt_bias=gdn_dt_bias, gdn_norm_w=gdn_norm_w,
        s5_lam_re=s5_lam_re, s5_lam_im=s5_lam_im, s5_log_dt=s5_log_dt, s5_b_re=s5_b_re, s5_b_im=s5_b_im,
        s5_c_re=s5_c_re, s5_c_im=s5_c_im, s5_d=s5_d, s5_glu_w=s5_glu_w, w_branch=w_branch, w_out=w_out,
        ln_g=ln_g, ln_b=ln_b))
    assert w_in.shape[2] == sum(s[1] for s in IN_SEGMENTS)
    wp = (jnp.asarray(_tile_sources(), jnp.int32), jnp.swapaxes(w_in, 1, 2))
    p = lax.optimization_barrier(p)
    pt_flat = page_table.reshape(-1).astype(jnp.int32)
    sb = db if ts == S5_L else 1
    c0 = C_KV // LANES
    zero_state = (jnp.zeros((bp, CONV_K - 1, W_BR), F32), jnp.zeros((bp, 1, W_BR), F32),
                  jnp.zeros((bp, CONV_K - 1, 3 * W_BR), F32), jnp.zeros((bp, GDN_H, GDN_HD, GDN_HD), F32),
                  jnp.zeros((bp, 1, S5_G * S5_P), F32), jnp.zeros((bp, 1, S5_G * S5_P), F32))

    def layer(carry, l):
        xp, xs = carry
        lidx = l.reshape(1)
        at = lambda a: lax.dynamic_index_in_dim(a, l, 0, keepdims=False)

        def nsa_prompt(z3):
            ckv = _compress(lidx, z3, c0, tp, p)
            return _attention(lidx, z3, ckv, z3, c0 + 4, c0 + 6, z3, c0 + 8, c0 + 10, 0, 0, BF16)

        xp, zp3, st_p = _mixer_layer(lidx, xp, bp, tp, zero_state, nsa_prompt, p, wp, alpha, BF16)

        win_state = {}

        def nsa_sample(z3):
            cmp_rows, sel_rows = _gather_ctx(lidx, pt_flat, cache_nsa_kv, z3)
            ckv = _compress(lidx, cmp_rows, None, past, p)
            win = jnp.concatenate([at(cache_win_kv).reshape(db, wbuf, winw),
                                   z3[:, :, C_KV + kvw:C_KV + kvw + winw]], axis=1)
            win_state["win"] = win
            pad = max(0, WINDOW + Q_BLOCK - (wbuf + ts))
            win_pad = jnp.concatenate([win, jnp.zeros((db, pad, winw), F32)], axis=1)
            return _attention(lidx, z3, ckv, sel_rows, 0, 2, win_pad, 0, 2, past, past - wbuf, F32)

        st_in = (at(state_lru_conv), at(state_lru_h)[:, None, :], at(state_gdn_conv), at(state_gdn_s),
                 at(state_s5_re).reshape(db // sb, sb, S5_G * S5_P), at(state_s5_im).reshape(db // sb, sb, S5_G * S5_P))
        xs, zs3, st_s = _mixer_layer(lidx, xs, db, ts, st_in, nsa_sample, p, wp, alpha, F32)

        def outs(st, z3, s, t, win):
            lru_buf, lru_h, gdn_buf, gdn_s, s5_re, s5_im = st
            return (lru_h.reshape(s, W_BR), lru_buf,
                    z3[:, :, C_KV:C_KV + kvw].reshape(s, t, 4, NSA_KVH, NSA_HD),
                    win.reshape(s, win.shape[1], 2, NSA_KVH, NSA_HD),
                    gdn_s, gdn_buf, s5_re.reshape(s, S5_G, S5_P), s5_im.reshape(s, S5_G, S5_P))

        win_p = zp3[:, tp - min(WINDOW, tp):, C_KV + kvw:C_KV + kvw + winw]
        win_s = win_state["win"][:, wbuf + ts - min(WINDOW, past + ts):]
        return (xp, xs), (outs(st_p, zp3, bp, tp, win_p), outs(st_s, zs3, db, ts, win_s))

    (xp, xs), (op, os_) = lax.scan(layer, (x_prompt.reshape(bp * tp, D_MODEL), x_sample.reshape(db * ts, D_MODEL)),
                                   jnp.arange(depth, dtype=jnp.int32))
    return (xp.reshape(bp, tp, D_MODEL), xs.reshape(db, ts, D_MODEL)) + tuple(op) + tuple(os_)
```

```python
import functools

import jax
import jax.numpy as jnp
from jax import lax
from jax.experimental import pallas as pl
from jax.experimental.pallas import tpu as pltpu

F32 = jnp.float32
BF16 = jnp.bfloat16
HI = lax.Precision.HIGHEST

D_MODEL = 2048
W_BR = D_MODEL // 2
N_BRANCH = 4
CONV_K = 4
LRU_BLOCKS = 8
LRU_BS = W_BR // LRU_BLOCKS
LRU_C = 8.0
NSA_HD = 128
NSA_H = 8
NSA_KVH = 2
NSA_GQ = NSA_H // NSA_KVH
CMP_STRIDE = 16
CMP_BLOCK = 32
SEL_BLOCK = 64
CMP_PER_SEL = SEL_BLOCK // CMP_STRIDE
TOP_N = 16
WINDOW = 512
Q_BLOCK = 128
PAGE_SIZE = 128
GDN_HD = 128
GDN_H = 8
GDN_CHUNK = 64
GDN_SPLIT_LEVELS = 2
S5_GS = 16
S5_G = W_BR // S5_GS
S5_P = 64
S5_L = 8
NEG = -1e30
FORCE = 1e9
LOG2E = 1.4426950408889634

LANES = 128
SUBLANES = 8
VMEM_LIMIT = 56 * 1024 * 1024

IN_TILE = 512
C_MERGE = 0
C_ULRU = 8192
C_GLRU = 9216
C_QNSA = 10240
C_GNSA = 11264
C_GGDN = 12288
C_US5 = 13312
C_GS5 = 14336
C_QKV = 15360
C_KV = 18432
C_SMALL_A = 19968
C_SMALL_B = 20480
NP = C_SMALL_B + IN_TILE
IN_SEGMENTS = (("u_lru", W_BR, C_ULRU), ("g_lru", W_BR, C_GLRU), ("q_nsa", W_BR, C_QNSA),
               ("kv", 6 * NSA_KVH * NSA_HD, C_KV), ("gl", 3 * NSA_H, None), ("g_nsa", W_BR, C_GNSA),
               ("qkv", 3 * W_BR, C_QKV), ("a", GDN_H, None), ("b", GDN_H, None),
               ("g_gdn", W_BR, C_GGDN), ("u_s5", W_BR, C_US5), ("g_s5", W_BR, C_GS5),
               ("merge", N_BRANCH * D_MODEL, C_MERGE))


def _segment_start(name):
    return sum(w for n, w, _ in IN_SEGMENTS[:[s[0] for s in IN_SEGMENTS].index(name)])


SRC_SMALL_A = _segment_start("gl") // LANES * LANES
SRC_SMALL_B = _segment_start("a") // LANES * LANES
SM_GL = _segment_start("gl") - SRC_SMALL_A
SM_A = _segment_start("a") - SRC_SMALL_B
SM_B = _segment_start("b") - SRC_SMALL_B


def _tile_sources():
    src = [None] * (NP // IN_TILE)
    at = 0
    for _, width, dst in IN_SEGMENTS:
        if dst is not None:
            for c in range(0, width, IN_TILE):
                src[(dst + c) // IN_TILE] = at + c
        at += width
    src[C_SMALL_A // IN_TILE] = SRC_SMALL_A
    src[C_SMALL_B // IN_TILE] = SRC_SMALL_B
    assert all(s is not None and s % SUBLANES == 0 and s + IN_TILE <= at for s in src)
    return src


def _cparams(sem):
    return pltpu.CompilerParams(dimension_semantics=sem, vmem_limit_bytes=VMEM_LIMIT)


def _sigmoid(x):
    return 0.5 + 0.5 * jnp.tanh(0.5 * x)


def _silu(x):
    return x * _sigmoid(x)


def _softplus(x):
    return jnp.maximum(x, 0.0) + jnp.log1p(jnp.exp(-jnp.abs(x)))


def _dot(a, b):
    return jnp.dot(a, b, preferred_element_type=F32)


def _dot_hi(a, b):
    return jnp.dot(a, b, preferred_element_type=F32, precision=HI)


def _dot_nt(a, b):
    return lax.dot_general(a, b, (((1,), (1,)), ((), ())), preferred_element_type=F32)


def _dot_tn(a, b):
    return lax.dot_general(a, b, (((0,), (0,)), ((), ())), preferred_element_type=F32)


def _iota(shape, axis):
    return lax.broadcasted_iota(jnp.int32, shape, axis)


def _log2(n):
    assert n & (n - 1) == 0
    return n.bit_length() - 1


def _inproj_kernel(l_ref, src_ref, x_ref, w_ref, o_ref, xb_ref):
    @pl.when(pl.program_id(1) == 0)
    def _():
        xb_ref[...] = x_ref[...].astype(BF16)

    o_ref[...] = _dot_nt(xb_ref[...], w_ref[...].astype(BF16))


def _inproj(lidx, srcs, x2d, wt):
    n = x2d.shape[0]
    tm = min(n, 2048)
    wspec = pl.BlockSpec((pl.Squeezed(), pl.Element(IN_TILE), pl.Element(D_MODEL)),
                         lambda i, j, l, src: (l[0], pl.multiple_of(src[j], SUBLANES), 0))
    xspec = pl.BlockSpec((tm, D_MODEL), lambda i, j, l, src: (i, 0), pipeline_mode=pl.Buffered(1))
    return pl.pallas_call(
        _inproj_kernel,
        grid_spec=pltpu.PrefetchScalarGridSpec(
            num_scalar_prefetch=2, grid=(n // tm, NP // IN_TILE),
            in_specs=[xspec, wspec],
            out_specs=pl.BlockSpec((tm, IN_TILE), lambda i, j, l, src: (i, j)),
            scratch_shapes=[pltpu.VMEM((tm, D_MODEL), BF16)]),
        out_shape=jax.ShapeDtypeStruct((n, NP), F32),
        compiler_params=_cparams(("parallel", "arbitrary")),
        name="inproj",
    )(lidx, srcs, x2d, wt)


def _lru_kernel(l_ref, u_ref, g_ref, buf_ref, h0_ref, cw_ref, cb_ref, wa_ref, ba_ref, wx_ref, bx_ref, lam_ref,
                y_ref, bufo_ref, ho_ref, xp_scr, a_scr, b_scr, h_scr, *, tt, nt):
    ti = pl.program_id(1)

    @pl.when(ti == 0)
    def _():
        xp_scr[5:8, :] = buf_ref[...]
        h_scr[...] = h0_ref[...]

    u = u_ref[...]
    xp_scr[8:8 + tt, :] = u
    cw = cw_ref[...]
    xc = (cb_ref[...] + cw[3:4] * u + cw[2:3] * xp_scr[7:7 + tt, :]
          + cw[1:2] * xp_scr[6:6 + tt, :] + cw[0:1] * xp_scr[5:5 + tt, :])
    tail = u[tt - 3:tt, :]
    xp_scr[5:8, :] = tail
    sp = _softplus(-lam_ref[...])
    for n in range(LRU_BLOCKS):
        sl = slice(n * LRU_BS, (n + 1) * LRU_BS)
        xn = xc[:, sl]
        xb = xn.astype(BF16)
        r = _sigmoid(_dot(xb, wa_ref[n]) + ba_ref[:, sl])
        i = _sigmoid(_dot(xb, wx_ref[n]) + bx_ref[:, sl])
        a = jnp.exp(-LRU_C * r * sp[:, sl])
        a_scr[:, sl] = a
        b_scr[:, sl] = jnp.sqrt(1.0 - a * a) * (i * xn)

    def body(i, h):
        for k in range(SUBLANES):
            t = i * SUBLANES + k
            h = a_scr[pl.ds(t, 1), :] * h + b_scr[pl.ds(t, 1), :]
            b_scr[pl.ds(t, 1), :] = h
        return h

    h = lax.fori_loop(0, tt // SUBLANES, body, h_scr[...])
    h_scr[...] = h
    y_ref[...] = (b_scr[...] * _silu(g_ref[...])).astype(y_ref.dtype)

    @pl.when(ti == nt - 1)
    def _():
        bufo_ref[...] = tail
        ho_ref[...] = h


def _lru(lidx, z3, buf, h0, p, ydtype):
    s, t, _ = z3.shape
    tt = min(t, 512)
    nt = t // tt
    wspec = lambda shape: pl.BlockSpec((None,) + shape, lambda b, i, l: (l[0],) + (0,) * len(shape))
    return pl.pallas_call(
        functools.partial(_lru_kernel, tt=tt, nt=nt),
        grid_spec=pltpu.PrefetchScalarGridSpec(
            num_scalar_prefetch=1, grid=(s, nt),
            in_specs=[pl.BlockSpec((None, tt, W_BR), lambda b, i, l: (b, i, C_ULRU // W_BR)),
                      pl.BlockSpec((None, tt, W_BR), lambda b, i, l: (b, i, C_GLRU // W_BR)),
                      pl.BlockSpec((None, 3, W_BR), lambda b, i, l: (b, 0, 0)),
                      pl.BlockSpec((None, 1, W_BR), lambda b, i, l: (b, 0, 0)),
                      wspec((CONV_K, W_BR)), wspec((1, W_BR)),
                      wspec((LRU_BLOCKS, LRU_BS, LRU_BS)), wspec((1, W_BR)),
                      wspec((LRU_BLOCKS, LRU_BS, LRU_BS)), wspec((1, W_BR)), wspec((1, W_BR))],
            out_specs=[pl.BlockSpec((None, tt, W_BR), lambda b, i, l: (b, i, 0)),
                       pl.BlockSpec((None, 3, W_BR), lambda b, i, l: (b, 0, 0)),
                       pl.BlockSpec((None, 1, W_BR), lambda b, i, l: (b, 0, 0))],
            scratch_shapes=[pltpu.VMEM((tt + 8, W_BR), F32), pltpu.VMEM((tt, W_BR), F32),
                            pltpu.VMEM((tt, W_BR), F32), pltpu.VMEM((1, W_BR), F32)]),
        out_shape=[jax.ShapeDtypeStruct((s, t, W_BR), ydtype),
                   jax.ShapeDtypeStruct((s, 3, W_BR), F32),
                   jax.ShapeDtypeStruct((s, 1, W_BR), F32)],
        compiler_params=_cparams(("parallel", "arbitrary")),
        name="rglru",
    )(lidx, z3, z3, buf, h0, p["lru_conv_w"], p["lru_conv_b"], p["lru_wa"], p["lru_ba"],
      p["lru_wx"], p["lru_bx"], p["lru_lambda"])


S5_CB = LANES // S5_GS
S5_SW = S5_CB * S5_P


def _gelu_tanh(x):
    return 0.5 * x * (1.0 + jnp.tanh(0.7978845608028654 * (x + 0.044715 * (x * x * x))))


def _s5_kernel(l_ref, u_ref, h0re_ref, h0im_ref, wst_ref, vout_ref, kt_ref, alre_ref, alim_ref, d_ref,
               y_ref, hre_ref, him_ref, hin_scr, s_scr, *, n, sb):
    rows = sb * n
    us = [u_ref[pl.ds(j, rows, stride=S5_L), :] for j in range(S5_L)]
    ub = jnp.concatenate(us, axis=1).astype(BF16)
    s = _dot(ub, wst_ref[...])
    alre = alre_ref[...]
    alim = alim_ref[...]
    h0re = h0re_ref[...]
    h0im = h0im_ref[...]
    if n == 1:
        hin_scr[:, :S5_SW] = h0re
        hin_scr[:, S5_SW:] = h0im
        hre = alre * h0re - alim * h0im + s[:, :S5_SW]
        him = alre * h0im + alim * h0re + s[:, S5_SW:]
    else:
        s_scr[...] = s

        def body(c, carry):
            hre, him = carry
            hin_scr[pl.ds(c, 1), :S5_SW] = hre
            hin_scr[pl.ds(c, 1), S5_SW:] = him
            srow = s_scr[pl.ds(c, 1), :]
            return (alre * hre - alim * him + srow[:, :S5_SW],
                    alre * him + alim * hre + srow[:, S5_SW:])

        hre, him = lax.fori_loop(0, n, body, (h0re, h0im))
    hre_ref[...] = hre
    him_ref[...] = him
    ycat = _dot(hin_scr[...].astype(BF16), vout_ref[...]) + _dot(ub, kt_ref[...])
    d = d_ref[...]
    for j in range(S5_L):
        yj = ycat[:, j * LANES:(j + 1) * LANES] + d * us[j]
        y_ref[pl.ds(j, rows, stride=S5_L), :] = _gelu_tanh(yj)


def _s5_scan(lidx, z3, h0re, h0im, p, sb):
    s, t, _ = z3.shape
    n = t // S5_L
    assert sb == 1 or n == 1
    sg = s // sb
    zr = z3.reshape(sg, sb * t, NP)
    ncb = W_BR // LANES
    wspec = lambda shape: pl.BlockSpec((None, None) + shape, lambda b, c, l: (l[0], c) + (0,) * len(shape))
    hspec = pl.BlockSpec((None, sb, S5_SW), lambda b, c, l: (b, 0, c))
    y, hre, him = pl.pallas_call(
        functools.partial(_s5_kernel, n=n, sb=sb),
        grid_spec=pltpu.PrefetchScalarGridSpec(
            num_scalar_prefetch=1, grid=(sg, ncb),
            in_specs=[pl.BlockSpec((None, sb * t, LANES), lambda b, c, l: (b, 0, C_US5 // LANES + c)),
                      hspec, hspec,
                      wspec((S5_L * LANES, 2 * S5_SW)), wspec((2 * S5_SW, S5_L * LANES)),
                      wspec((S5_L * LANES, S5_L * LANES)), wspec((1, S5_SW)), wspec((1, S5_SW)),
                      wspec((1, LANES))],
            out_specs=[pl.BlockSpec((None, sb * t, LANES), lambda b, c, l: (b, 0, c)), hspec, hspec],
            scratch_shapes=[pltpu.VMEM((sb * n, 2 * S5_SW), F32), pltpu.VMEM((sb * n, 2 * S5_SW), F32)]),
        out_shape=[jax.ShapeDtypeStruct((sg, sb * t, W_BR), F32),
                   jax.ShapeDtypeStruct((sg, sb, S5_G * S5_P), F32),
                   jax.ShapeDtypeStruct((sg, sb, S5_G * S5_P), F32)],
        compiler_params=_cparams(("parallel", "arbitrary")),
        name="s5_scan",
    )(lidx, zr, h0re, h0im, p["s5_wst"], p["s5_vout"], p["s5_kt"], p["s5_alre"], p["s5_alim"], p["s5_d"])
    return y.reshape(s * t, W_BR), hre, him


def _s5_glu_kernel(l_ref, y_ref, g_ref, w_ref, o_ref):
    gl = _dot(y_ref[...].astype(BF16), w_ref[...])
    o_ref[...] = (gl[:, :W_BR] * _sigmoid(gl[:, W_BR:]) * _silu(g_ref[...])).astype(o_ref.dtype)


def _s5_glu(lidx, y2d, z2d, p, ydtype):
    n = y2d.shape[0]
    tm = min(n, 512)
    return pl.pallas_call(
        _s5_glu_kernel,
        grid_spec=pltpu.PrefetchScalarGridSpec(
            num_scalar_prefetch=1, grid=(n // tm,),
            in_specs=[pl.BlockSpec((tm, W_BR), lambda i, l: (i, 0)),
                      pl.BlockSpec((tm, W_BR), lambda i, l: (i, C_GS5 // W_BR)),
                      pl.BlockSpec((None, W_BR, 2 * W_BR), lambda i, l: (l[0], 0, 0))],
            out_specs=pl.BlockSpec((tm, W_BR), lambda i, l: (i, 0))),
        out_shape=jax.ShapeDtypeStruct((n, W_BR), ydtype),
        compiler_params=_cparams(("parallel",)),
        name="s5_glu",
    )(lidx, y2d, z2d, p["s5_glu_w"])


def _s5_expand_kernel(wst_ref, vout_ref, kt_ref, wst_o, vout_o, kt_o):
    n = S5_L * LANES
    lp = S5_P.bit_length() - 1
    lc = S5_GS.bit_length() - 1
    lg = S5_CB.bit_length() - 1
    row = _iota((n, n), 0)
    col = _iota((n, n), 1)
    src = _iota((LANES, n), 0)
    dst = _iota((LANES, n), 1)
    rep_state = jnp.where(src == (((dst >> (lp + lg)) << lp) | (dst & (S5_P - 1))), 1.0, 0.0).astype(BF16)
    rep_out = jnp.where(src == (((dst >> (lc + lg)) << lc) | (dst & (S5_GS - 1))), 1.0, 0.0).astype(BF16)
    gi_in_row = (row >> lc) & (S5_CB - 1)
    gi_state_row = (row >> lp) & (S5_CB - 1)
    gi_state_col = (col >> lp) & (S5_CB - 1)
    gi_out_col = (col >> lc) & (S5_CB - 1)
    wst_o[...] = jnp.where(gi_in_row == gi_state_col, _dot(wst_ref[...].astype(BF16), rep_state), 0.0).astype(BF16)
    vout_o[...] = jnp.where(gi_state_row == gi_out_col, _dot(vout_ref[...].astype(BF16), rep_out), 0.0).astype(BF16)
    kt_o[...] = jnp.where(gi_in_row == gi_out_col, _dot(kt_ref[...].astype(BF16), rep_out), 0.0).astype(BF16)


def _s5_expand(wst, vout, kt):
    dd, ncb, n, _ = wst.shape
    cspec = pl.BlockSpec((None, None, n, LANES), lambda d, c: (d, c, 0, 0))
    ospec = pl.BlockSpec((None, None, n, n), lambda d, c: (d, c, 0, 0))
    oshape = jax.ShapeDtypeStruct((dd, ncb, n, n), BF16)
    return pl.pallas_call(
        _s5_expand_kernel, grid=(dd, ncb), in_specs=[cspec, cspec, cspec], out_specs=[ospec, ospec, ospec],
        out_shape=[oshape, oshape, oshape], compiler_params=_cparams(("parallel", "parallel")),
        name="s5_expand",
    )(wst, vout, kt)


def _s5_weights(w):
    dt = jnp.exp(w["s5_log_dt"])[..., None]
    lr, li = w["s5_lam_re"], w["s5_lam_im"]
    mag = jnp.exp(lr * dt)
    a_re = mag * jnp.cos(li * dt)
    a_im = mag * jnp.sin(li * dt)
    den = lr * lr + li * li
    f_re = ((a_re - 1.0) * lr + a_im * li) / den
    f_im = (a_im * lr - (a_re - 1.0) * li) / den
    bb_re = f_re[..., None] * w["s5_b_re"] - f_im[..., None] * w["s5_b_im"]
    bb_im = f_re[..., None] * w["s5_b_im"] + f_im[..., None] * w["s5_b_re"]
    pw_re = [jnp.ones_like(a_re)]
    pw_im = [jnp.zeros_like(a_im)]
    for _ in range(S5_L):
        pr, pi = pw_re[-1], pw_im[-1]
        pw_re.append(pr * a_re - pi * a_im)
        pw_im.append(pr * a_im + pi * a_re)
    pw_re = jnp.stack(pw_re, axis=1)
    pw_im = jnp.stack(pw_im, axis=1)
    dd = lr.shape[0]
    ncb = S5_G // S5_CB
    grp = lambda a: a.reshape(a.shape[0], a.shape[1], ncb, S5_CB, *a.shape[3:])
    rev_re = jnp.stack([pw_re[:, S5_L - 1 - j] for j in range(S5_L)], axis=1)
    rev_im = jnp.stack([pw_im[:, S5_L - 1 - j] for j in range(S5_L)], axis=1)
    st_re = rev_re[..., None] * bb_re[:, None] - rev_im[..., None] * bb_im[:, None]
    st_im = rev_re[..., None] * bb_im[:, None] + rev_im[..., None] * bb_re[:, None]

    def state_w(a):
        return grp(a).transpose(0, 2, 1, 3, 5, 4).reshape(dd, ncb, S5_L * LANES, S5_P)

    wst = jnp.concatenate([state_w(st_re), state_w(st_im)], axis=-1)
    c_re, c_im = w["s5_c_re"], w["s5_c_im"]
    nx_re, nx_im = pw_re[:, 1:], pw_im[:, 1:]
    ca_re = c_re[:, None] * nx_re[:, :, :, None] - c_im[:, None] * nx_im[:, :, :, None]
    ca_im = c_re[:, None] * nx_im[:, :, :, None] + c_im[:, None] * nx_re[:, :, :, None]

    def out_w(a):
        return grp(a).transpose(0, 2, 3, 5, 1, 4).reshape(dd, ncb, S5_SW, S5_L * S5_GS)

    vout = jnp.concatenate([out_w(ca_re), out_w(-ca_im)], axis=2)
    cat_re = c_re[:, None] * pw_re[:, :S5_L, :, None] - c_im[:, None] * pw_im[:, :S5_L, :, None]
    cat_im = c_re[:, None] * pw_im[:, :S5_L, :, None] + c_im[:, None] * pw_re[:, :S5_L, :, None]
    kk = (jnp.einsum("dlgcp,dgpe->dlgce", cat_re, bb_re, precision=HI)
          - jnp.einsum("dlgcp,dgpe->dlgce", cat_im, bb_im, precision=HI))
    zero = jnp.zeros_like(kk[:, 0])
    kt = jnp.stack([jnp.stack([kk[:, j - i] if j >= i else zero for j in range(S5_L)], axis=1)
                    for i in range(S5_L)], axis=1)
    kt = kt.reshape(dd, S5_L, S5_L, ncb, S5_CB, S5_GS, S5_GS)
    kt = kt.transpose(0, 3, 1, 4, 6, 2, 5).reshape(dd, ncb, S5_L * LANES, S5_L * S5_GS)
    wst, vout, kt = _s5_expand(wst, vout, kt)
    sw = lambda a: a.reshape(dd, ncb, 1, S5_SW)
    return dict(s5_wst=wst, s5_vout=vout, s5_kt=kt, s5_alre=sw(pw_re[:, S5_L]), s5_alim=sw(pw_im[:, S5_L]),
                s5_d=w["s5_d"].reshape(dd, W_BR // LANES, 1, LANES), s5_glu_w=w["s5_glu_w"].astype(BF16))


def _unit_lower_solve(ms, rhs, c):
    def split(a):
        hi = a.astype(BF16)
        return hi, (a - hi.astype(F32)).astype(BF16)

    def dot3(a, b):
        return _dot(a[0], b[0]) + _dot(a[1], b[0]) + _dot(a[0], b[1])

    n = range(len(ms))
    ms = [split(m) for m in ms]
    rs = [split(r) for r in rhs]
    xs = [rhs[i] - dot3(ms[i], rs[i]) for i in n]
    k = 2
    while k < c:
        if k <= 2 ** GDN_SPLIT_LEVELS:
            ms = [split(dot3(ms[i], ms[i])) for i in n]
            rs = [split(x) for x in xs]
            xs = [xs[i] + dot3(ms[i], rs[i]) for i in n]
        else:
            ms = [(_dot(ms[i][0], ms[i][0]).astype(BF16), None) for i in n]
            xs = [xs[i] + _dot(ms[i][0], xs[i].astype(BF16)) for i in n]
        k *= 2
    return xs


def _gdn_kernel(l_ref, q_ref, k_ref, v_ref, sm_ref, gg_ref, bq_ref, bk_ref, bv_ref, cwq_ref, cwk_ref, cwv_ref,
                alog_ref, dtb_ref, nw_ref, s0_ref, y_ref, bufo_ref, so_ref,
                xp_scr, qkv_scr, gx_scr, bx_scr, gc_scr, s_scr, *, tt, nt, c):
    ti = pl.program_id(1)
    log2c = c.bit_length() - 1

    @pl.when(ti == 0)
    def _():
        for i, b_ref in enumerate((bq_ref, bk_ref, bv_ref)):
            xp_scr[i, 5:8, :] = b_ref[...]
        s_scr[...] = s0_ref[...]

    for i, (x_ref, cw_ref) in enumerate(((q_ref, cwq_ref), (k_ref, cwk_ref), (v_ref, cwv_ref))):
        x = x_ref[...]
        xp_scr[i, 8:8 + tt, :] = x
        cw = cw_ref[...]
        cv = (cw[3:4] * x + cw[2:3] * xp_scr[i, 7:7 + tt, :]
              + cw[1:2] * xp_scr[i, 6:6 + tt, :] + cw[0:1] * xp_scr[i, 5:5 + tt, :])
        tail = x[tt - 3:tt, :]
        xp_scr[i, 5:8, :] = tail
        bufo_ref[:, i * W_BR:(i + 1) * W_BR] = tail
        cv = _silu(cv)
        if i < 2:
            scale = GDN_HD ** -0.5 if i == 0 else 1.0
            for h in range(GDN_H):
                sl = slice(h * GDN_HD, (h + 1) * GDN_HD)
                xh = cv[:, sl]
                qkv_scr[i, :, sl] = xh * (lax.rsqrt(jnp.sum(xh * xh, axis=-1, keepdims=True) + 1e-6) * scale)
        else:
            qkv_scr[i] = cv

    sm = sm_ref[...]
    gsm = -jnp.exp(alog_ref[...]) * _softplus(sm + dtb_ref[...])
    bsm = _sigmoid(sm)
    src = _iota((LANES, W_BR), 0)
    head = _iota((LANES, W_BR), 1) >> _log2(GDN_HD)
    gx_scr[...] = _dot_hi(gsm, jnp.where(src - SM_A == head, 1.0, 0.0))
    bx_scr[...] = _dot_hi(bsm, jnp.where(src - SM_B == head, 1.0, 0.0))
    headc = _iota((LANES, GDN_H * c), 1) >> log2c
    gc_scr[...] = _dot_hi(gsm, jnp.where(_iota((LANES, GDN_H * c), 0) - SM_A == headc, 1.0, 0.0))

    rowi = _iota((c, c), 0)
    coli = _iota((c, c), 1)
    incl = coli <= rowi
    strict = coli < rowi
    ltri = jnp.where(incl, 1.0, 0.0)
    upper = jnp.where(_iota((c, GDN_H * c), 0) > (_iota((c, GDN_H * c), 1) & (c - 1)), 1.0, 0.0)
    nw = nw_ref[...]

    def chunk(ci, carry):
        r0 = pl.multiple_of(ci * c, c)
        gcb_all = _dot_hi(ltri, gx_scr[pl.ds(r0, c), :])
        diffs = _dot_hi(ltri, gc_scr[pl.ds(r0, c), :] * upper)
        beta_all = bx_scr[pl.ds(r0, c), :]
        hs = range(GDN_H)
        sls = [slice(h * GDN_HD, (h + 1) * GDN_HD) for h in hs]
        q = [qkv_scr[0, pl.ds(r0, c), sl] for sl in sls]
        k = [qkv_scr[1, pl.ds(r0, c), sl] for sl in sls]
        gcb = [gcb_all[:, sl] for sl in sls]
        decay = [jnp.where(incl, jnp.exp(diffs[:, h * c:(h + 1) * c]), 0.0) for h in hs]
        kb = [k[h] * beta_all[:, sls[h]] for h in hs]
        kbf = [x.astype(BF16) for x in k]
        kq = [_dot_nt(jnp.concatenate([kb[h], q[h]], axis=0).astype(BF16), kbf[h]) for h in hs]
        m = [jnp.where(strict, kq[h][:c] * decay[h], 0.0) for h in hs]
        qk = [(kq[h][c:] * decay[h]).astype(BF16) for h in hs]
        rhs = [jnp.concatenate([qkv_scr[2, pl.ds(r0, c), sls[h]] * beta_all[:, sls[h]], kb[h] * jnp.exp(gcb[h])],
                               axis=1) for h in hs]
        sol = _unit_lower_solve(m, rhs, c)
        s = [s_scr[h] for h in hs]
        sb = [x.astype(BF16) for x in s]
        ws = [_dot(jnp.concatenate([sol[h][:, GDN_HD:], q[h] * jnp.exp(gcb[h])], axis=0).astype(BF16), sb[h])
              for h in hs]
        vnb = [(sol[h][:, :GDN_HD] - ws[h][:c]).astype(BF16) for h in hs]
        o = [ws[h][c:] + _dot(qk[h], vnb[h]) for h in hs]
        for h in hs:
            glast = gcb[h][c - 1:c, :]
            s_scr[h] = s[h] * jnp.exp(glast) + _dot_tn((k[h] * jnp.exp(glast - gcb[h])).astype(BF16), vnb[h])
        for h in hs:
            on = o[h] * lax.rsqrt(jnp.mean(o[h] * o[h], axis=-1, keepdims=True) + 1e-6) * nw
            y_ref[pl.ds(r0, c), sls[h]] = (on * _silu(gg_ref[pl.ds(r0, c), sls[h]])).astype(y_ref.dtype)
        return carry

    lax.fori_loop(0, tt // c, chunk, 0)

    @pl.when(ti == nt - 1)
    def _():
        so_ref[...] = s_scr[...]


def _gdn(lidx, z3, buf, s0, p, ydtype, chunk):
    s, t, _ = z3.shape
    tt = min(t, 256)
    nt = t // tt
    c = min(chunk, tt)
    zspec = lambda col: pl.BlockSpec((None, tt, W_BR), lambda b, i, l: (b, i, col // W_BR))
    bspec = lambda j: pl.BlockSpec((None, 3, W_BR), lambda b, i, l: (b, 0, j))
    cspec = lambda j: pl.BlockSpec((None, CONV_K, W_BR), lambda b, i, l: (l[0], 0, j))
    rspec = pl.BlockSpec((None, 1, LANES), lambda b, i, l: (l[0], 0, 0))
    sspec = pl.BlockSpec((None, GDN_H, GDN_HD, GDN_HD), lambda b, i, l: (b, 0, 0, 0))
    return pl.pallas_call(
        functools.partial(_gdn_kernel, tt=tt, nt=nt, c=c),
        grid_spec=pltpu.PrefetchScalarGridSpec(
            num_scalar_prefetch=1, grid=(s, nt),
            in_specs=[zspec(C_QKV), zspec(C_QKV + W_BR), zspec(C_QKV + 2 * W_BR),
                      pl.BlockSpec((None, tt, LANES), lambda b, i, l: (b, i, C_SMALL_B // LANES)),
                      zspec(C_GGDN), bspec(0), bspec(1), bspec(2), cspec(0), cspec(1), cspec(2),
                      rspec, rspec, rspec, sspec],
            out_specs=[pl.BlockSpec((None, tt, W_BR), lambda b, i, l: (b, i, 0)),
                       pl.BlockSpec((None, 3, 3 * W_BR), lambda b, i, l: (b, 0, 0)),
                       sspec],
            scratch_shapes=[pltpu.VMEM((3, tt + 8, W_BR), F32), pltpu.VMEM((3, tt, W_BR), F32),
                            pltpu.VMEM((tt, W_BR), F32), pltpu.VMEM((tt, W_BR), F32),
                            pltpu.VMEM((tt, GDN_H * c), F32), pltpu.VMEM((GDN_H, GDN_HD, GDN_HD), F32)]),
        out_shape=[jax.ShapeDtypeStruct((s, t, W_BR), ydtype),
                   jax.ShapeDtypeStruct((s, 3, 3 * W_BR), F32),
                   jax.ShapeDtypeStruct((s, GDN_H, GDN_HD, GDN_HD), F32)],
        compiler_params=_cparams(("parallel", "arbitrary")),
        name="gdn",
    )(lidx, z3, z3, z3, z3, z3, buf, buf, buf, p["gdn_conv_w"], p["gdn_conv_w"], p["gdn_conv_w"],
      p["gdn_a_log"], p["gdn_dt_bias"], p["gdn_norm_w"], s0)


PAGES_PER_STEP = 8
MAX_TAIL_STEPS = 3


KEY_TILE = 512


def _key_tile(tk):
    return max(k for k in range(KEY_TILE, min(tk, LANES * SEL_BLOCK) + 1, KEY_TILE) if tk % k == 0)


def _gather_kernel(l_ref, pt_ref, *refs, n_past):
    pages = refs[:PAGES_PER_STEP]
    zn_ref, cmp_ref, sel_ref = refs[PAGES_PER_STEP:]
    i = pl.program_id(1)
    half = 2 * NSA_KVH * NSA_HD

    @pl.when(i < n_past)
    def _():
        per_tok = 4 * NSA_KVH
        for k, p_ref in enumerate(pages):
            rows = slice(k * PAGE_SIZE, (k + 1) * PAGE_SIZE)
            for r in (2, 3):
                for g in range(NSA_KVH):
                    x = p_ref[pl.ds(r * NSA_KVH + g, PAGE_SIZE, stride=per_tok), :]
                    cols = slice(((r - 2) * NSA_KVH + g) * NSA_HD, ((r - 2) * NSA_KVH + g + 1) * NSA_HD)
                    sel_ref[rows, cols] = x.astype(BF16)
        halves = PAGE_SIZE // CMP_STRIDE
        for r in (0, 1):
            for g in range(NSA_KVH):
                flat = [jnp.concatenate([p_ref[pl.ds(s * per_tok + r * NSA_KVH + g, halves,
                                                     stride=CMP_STRIDE * per_tok), :] for p_ref in pages], axis=0)
                        for s in range(CMP_STRIDE)]
                cmp_ref[r, g] = jnp.concatenate(flat, axis=1).astype(BF16)

    @pl.when(i >= n_past)
    def _():
        sel_ref[...] = jnp.zeros(sel_ref.shape, BF16)

    @pl.when(i == n_past)
    def _():
        new = zn_ref[:, half:2 * half]
        sel_ref[0:2 * SUBLANES, :] = jnp.concatenate([new, jnp.zeros_like(new)], axis=0).astype(BF16)


def _gather_ctx(lidx, pt_flat, cache, z3):
    b, t_new, _ = z3.shape
    assert t_new == SUBLANES
    n_pages = pt_flat.shape[0] // b
    assert n_pages % PAGES_PER_STEP == 0
    n_past = n_pages // PAGES_PER_STEP
    rows = PAGES_PER_STEP * PAGE_SIZE
    n_tail = min(range(1, MAX_TAIL_STEPS + 1), key=lambda k: ((n_past + k) * rows // _key_tile((n_past + k) * rows), k))
    width = 4 * NSA_KVH * NSA_HD
    half = width // 2
    n_pool = cache.shape[1]
    cache = cache.reshape(cache.shape[0] * n_pool, PAGE_SIZE * 4 * NSA_KVH, NSA_HD)

    def pspec(k):
        def imap(bi, i, l, pt):
            page = jnp.minimum(i * PAGES_PER_STEP + k, n_pages - 1)
            return (l[0] * n_pool + pt[bi * n_pages + page], 0, 0)
        return pl.BlockSpec((None, PAGE_SIZE * 4 * NSA_KVH, NSA_HD), imap)

    return pl.pallas_call(
        functools.partial(_gather_kernel, n_past=n_past),
        grid_spec=pltpu.PrefetchScalarGridSpec(
            num_scalar_prefetch=2, grid=(b, n_past + n_tail),
            in_specs=[pspec(k) for k in range(PAGES_PER_STEP)]
            + [pl.BlockSpec((None, t_new, width), lambda bi, i, l, pt: (bi, 0, C_KV // width))],
            out_specs=[pl.BlockSpec((None, 2, NSA_KVH, rows // CMP_STRIDE, CMP_STRIDE * NSA_HD),
                                    lambda bi, i, l, pt: (bi, 0, 0, jnp.minimum(i, n_past - 1), 0)),
                       pl.BlockSpec((None, rows, half), lambda bi, i, l, pt: (bi, i, 0))]),
        out_shape=[jax.ShapeDtypeStruct((b, 2, NSA_KVH, n_past * rows // CMP_STRIDE, CMP_STRIDE * NSA_HD), BF16),
                   jax.ShapeDtypeStruct((b, (n_past + n_tail) * rows, half), BF16)],
        compiler_params=_cparams(("parallel", "arbitrary")),
        name="nsa_gather",
    )(lidx, pt_flat, *([cache] * PAGES_PER_STEP), z3)


def _cmp_bias_kernel(pe_ref, w1_ref, o_ref):
    o_ref[...] = _dot_hi(pe_ref[...], w1_ref[...].astype(F32))


def _cmp_bias(pe8, w1):
    dd, _, flat, wide = w1.shape
    return pl.pallas_call(
        _cmp_bias_kernel, grid=(dd, 2),
        in_specs=[pl.BlockSpec((None, SUBLANES, flat), lambda d, c: (d, 0, 0)),
                  pl.BlockSpec((None, None, flat, wide), lambda d, c: (d, c, 0, 0))],
        out_specs=pl.BlockSpec((None, None, SUBLANES, wide), lambda d, c: (d, c, 0, 0)),
        out_shape=jax.ShapeDtypeStruct((dd, 2, SUBLANES, wide), F32),
        compiler_params=_cparams(("parallel", "parallel")),
        name="nsa_cmp_bias",
    )(pe8, w1)


def _cmp_kernel(l_ref, x0_ref, x1_ref, w1_ref, bias_ref, w2_ref, o_ref, carry_scr, *, nh, flat):
    @pl.when(pl.program_id(2) == 0)
    def _():
        carry_scr[...] = jnp.zeros(carry_scr.shape, F32)

    w1 = w1_ref[...]
    bias = bias_ref[...]
    last = _iota((nh, NSA_HD), 0) == nh - 1
    for g, x_ref in enumerate((x0_ref, x1_ref)):
        if flat:
            ucat = x_ref[...]
        else:
            ucat = jnp.concatenate([x_ref[pl.ds(s, nh, stride=CMP_STRIDE), :] for s in range(CMP_STRIDE)], axis=1)
        hh = _dot(ucat.astype(BF16), w1)
        hf = hh[:, :NSA_HD] + bias[0:1, :NSA_HD]
        hs = hh[:, NSA_HD:] + bias[1:2, NSA_HD:]
        hs_next = jnp.where(last, carry_scr[g], pltpu.roll(hs, nh - 1, axis=0))
        carry_scr[g] = hs[0:1, :]
        hid = _silu(hf + hs_next)
        o_ref[g] = _dot(hid.astype(BF16), w2_ref[...]).astype(BF16)


def _compress(lidx, src, col0, n_rows, p):
    b = src.shape[0]
    tr = min(n_rows, 4096)
    nt = n_rows // tr
    nh = tr // CMP_STRIDE
    flat = col0 is None
    if flat:
        xspec = lambda g: pl.BlockSpec((None, None, None, nh, CMP_STRIDE * NSA_HD),
                                       lambda bi, c, i, l: (bi, c, g, nt - 1 - i, 0))
    else:
        xspec = lambda g: pl.BlockSpec((None, tr, NSA_HD),
                                       lambda bi, c, i, l: (bi, nt - 1 - i, col0 + NSA_KVH * c + g))
    return pl.pallas_call(
        functools.partial(_cmp_kernel, nh=nh, flat=flat),
        grid_spec=pltpu.PrefetchScalarGridSpec(
            num_scalar_prefetch=1, grid=(b, 2, nt),
            in_specs=[xspec(0), xspec(1),
                      pl.BlockSpec((None, None, CMP_STRIDE * NSA_HD, 2 * NSA_HD), lambda bi, c, i, l: (l[0], c, 0, 0)),
                      pl.BlockSpec((None, None, SUBLANES, 2 * NSA_HD), lambda bi, c, i, l: (l[0], c, 0, 0)),
                      pl.BlockSpec((None, None, NSA_HD, NSA_HD), lambda bi, c, i, l: (l[0], c, 0, 0))],
            out_specs=pl.BlockSpec((None, None, NSA_KVH, nh, NSA_HD), lambda bi, c, i, l: (bi, c, 0, nt - 1 - i, 0)),
            scratch_shapes=[pltpu.VMEM((NSA_KVH, 1, NSA_HD), F32)]),
        out_shape=jax.ShapeDtypeStruct((b, 2, NSA_KVH, n_rows // CMP_STRIDE, NSA_HD), BF16),
        compiler_params=_cparams(("parallel", "parallel", "arbitrary")),
        name="nsa_compress",
    )(lidx, src, src, p["nsa_w1"], p["nsa_bias"], p["nsa_w2"])


def _masked_softmax(s, valid):
    s = jnp.where(valid, s, NEG)
    m = jnp.max(s, axis=-1, keepdims=True)
    e = jnp.where(valid, jnp.exp2(s - m), 0.0)
    den = jnp.sum(e, axis=-1, keepdims=True)
    return e / jnp.where(den > 0.0, den, 1.0)


def _attn_kernel(l_ref, q_ref, sm_ref, gn_ref, kc_ref, vc_ref, ks_ref, vs_ref, kw_ref, vw_ref, y_ref, *,
                 qb, pos0, ncp, ns, nsp, kt, wn, wpos0, tw):
    g = pl.program_id(1)
    q0 = pl.program_id(2) * qb
    qpos0 = pos0 + q0
    hd = NSA_HD
    q = q_ref[...] * (hd ** -0.5 * LOG2E)
    qr = jnp.concatenate([q[:, j * hd:(j + 1) * hd] for j in range(NSA_GQ)], axis=0).astype(BF16)
    slope_g = jnp.where(g == 0, LOG2E, LOG2E * 2.0 ** -NSA_GQ)
    slopes = [slope_g * 2.0 ** -(j + 1) for j in range(NSA_GQ)]
    heads = lambda a: [a[j * qb:(j + 1) * qb] for j in range(NSA_GQ)]

    dist = (qpos0 + _iota((qb, ncp), 0)) - (_iota((qb, ncp), 1) * CMP_STRIDE + (CMP_BLOCK - 1))
    valid = dist >= 0
    distf = dist.astype(F32)
    sc = heads(_dot_nt(qr, kc_ref[...]))
    pc = [_masked_softmax(sc[j] - slopes[j] * distf, valid) for j in range(NSA_GQ)]
    o_c = heads(_dot(jnp.concatenate(pc, axis=0).astype(BF16), vc_ref[...]))
    imp = pc[0] + pc[1] + pc[2] + pc[3]
    pool = jnp.where((_iota((ncp, nsp), 0) >> _log2(CMP_PER_SEL)) == _iota((ncp, nsp), 1), 1.0, 0.0)
    imp = _dot_hi(imp, pool)

    blk = _iota((qb, nsp), 1)
    qp = qpos0 + _iota((qb, nsp), 0)
    val = jnp.where(blk * SEL_BLOCK > qp, -FORCE, imp)
    val = jnp.where(blk == (qp >> _log2(SEL_BLOCK)), FORCE, jnp.where(blk == 0, FORCE, val))
    val = jnp.where(blk >= ns, -3.0 * FORCE, val)
    top = float(min(TOP_N, ns))
    if qb == LANES and nsp == LANES:
        nsr = -(-ns // SUBLANES) * SUBLANES
        val_t = val.T[:nsr]
        blk_t = _iota((nsr, qb), 0)
        rank = jnp.zeros((nsr, qb), F32)
        for bidx in range(ns):
            cand = val_t[bidx:bidx + 1, :]
            rank = rank + jnp.where(cand > val_t, 1.0, jnp.where(cand == val_t, jnp.where(blk_t > bidx, 1.0, 0.0), 0.0))
        sel_t = jnp.where(rank < top, 1.0, 0.0)
        selb = jnp.concatenate([sel_t, jnp.zeros((nsp - nsr, qb), F32)], axis=0).T.astype(BF16)
    else:
        rank = jnp.zeros((qb, nsp), F32)
        for bidx in range(ns):
            cand = val[:, bidx:bidx + 1]
            rank = rank + jnp.where(cand > val, 1.0, jnp.where(cand == val, jnp.where(blk > bidx, 1.0, 0.0), 0.0))
        selb = jnp.where(rank < top, 1.0, 0.0).astype(BF16)

    n_tiles = (qpos0 + qb + kt - 1) // kt
    qh = heads(qr)
    spread = jnp.where(_iota((LANES, kt), 0) == (_iota((LANES, kt), 1) >> _log2(SEL_BLOCK)), 1.0, 0.0).astype(BF16)

    def tile_bias(t):
        k0 = t * kt
        d = (qpos0 + _iota((qb, kt), 0)) - (k0 + _iota((qb, kt), 1))
        pick_blk = jnp.where(_iota((nsp, LANES), 0) == (k0 >> _log2(SEL_BLOCK)) + _iota((nsp, LANES), 1),
                             1.0, 0.0).astype(BF16)
        sel_tile = _dot(selb, pick_blk).astype(BF16)
        ok = jnp.where(d >= 0, _dot(sel_tile, spread), 0.0) > 0.5
        return jnp.where(ok, d.astype(F32), -NEG)

    def tile(t, carry):
        k0 = pl.multiple_of(t * kt, kt)
        kk = ks_ref[pl.ds(k0, kt), :].astype(BF16)
        vv = vs_ref[pl.ds(k0, kt), :].astype(BF16)
        per_head = qb == Q_BLOCK
        s = [_dot_nt(qh[j], kk) for j in range(NSA_GQ)] if per_head else heads(_dot_nt(qr, kk))
        bias = carry[NSA_GQ]
        bias_next = tile_bias(t + 1)
        stats, prs = [], []
        for j in range(NSA_GQ):
            m, lsum, _ = carry[j]
            sm = s[j] - slopes[j] * bias
            m_new = jnp.maximum(m, jnp.max(sm, axis=-1, keepdims=True))
            pr = jnp.exp2(sm - m_new)
            alpha = jnp.exp2(m - m_new)
            stats.append((m_new, alpha * lsum + jnp.sum(pr, axis=-1, keepdims=True), alpha))
            prs.append(pr.astype(BF16))
        pv = [_dot(pr, vv) for pr in prs] if per_head else heads(_dot(jnp.concatenate(prs, axis=0), vv))
        new = [(stats[j][0], stats[j][1], stats[j][2] * carry[j][2] + pv[j]) for j in range(NSA_GQ)]
        return tuple(new) + (bias_next,)

    init = (jnp.full((qb, 1), NEG, F32), jnp.zeros((qb, 1), F32), jnp.zeros((qb, hd), F32))
    fin = lax.fori_loop(0, n_tiles, tile, (init,) * NSA_GQ + (tile_bias(0),))
    o_s = [acc / lsum for _, lsum, acc in fin[:NSA_GQ]]

    k0w = pl.multiple_of(jnp.clip(q0 - WINDOW, 0, tw - wn), SUBLANES)
    dw = (qpos0 + _iota((qb, wn), 0)) - (wpos0 + k0w + _iota((qb, wn), 1))
    okw = jnp.abs(2 * dw - (WINDOW - 1)) < WINDOW
    dwf = dw.astype(F32)
    sw = heads(_dot_nt(qr, kw_ref[pl.ds(k0w, wn), :].astype(BF16)))
    pw = [_masked_softmax(sw[j] - slopes[j] * dwf, okw) for j in range(NSA_GQ)]
    o_w = heads(_dot(jnp.concatenate(pw, axis=0).astype(BF16), vw_ref[pl.ds(k0w, wn), :].astype(BF16)))

    gate = _sigmoid(sm_ref[...])
    lane = _iota((qb, LANES), 1)
    pick = lambda idx: jnp.sum(jnp.where(lane == idx, gate, 0.0), axis=-1, keepdims=True)
    for j in range(NSA_GQ):
        base = SM_GL + 3 * (g * NSA_GQ + j)
        o = pick(base) * o_c[j] + pick(base + 1) * o_s[j] + pick(base + 2) * o_w[j]
        y_ref[:, j * hd:(j + 1) * hd] = (o * _silu(gn_ref[:, j * hd:(j + 1) * hd])).astype(y_ref.dtype)


def _attention(lidx, z3, cmp_kv, ks_src, ks_col, vs_col, kw_src, kw_col, vw_col, pos0, wpos0, ydtype):
    b, t, _ = z3.shape
    qb = min(Q_BLOCK, t)
    ncp = cmp_kv.shape[3]
    tk = ks_src.shape[1]
    tw = kw_src.shape[1]
    if qb == Q_BLOCK:
        kt = KEY_TILE
    else:
        kt = _key_tile(tk)
    ns = -(-(pos0 + t) // SEL_BLOCK)
    nsp = -(-ns // LANES) * LANES
    wn = WINDOW + Q_BLOCK
    assert tk % kt == 0 and tk >= pos0 + t and tw >= wn and ncp // CMP_PER_SEL <= nsp
    hw = NSA_GQ * NSA_HD
    kern = functools.partial(_attn_kernel, qb=qb, pos0=pos0, ncp=ncp, ns=ns, nsp=nsp, kt=kt, wn=wn, wpos0=wpos0, tw=tw)
    kvspec = lambda rows, col: pl.BlockSpec((None, rows, NSA_HD), lambda bi, g, i, l: (bi, 0, col + g))
    cspec = lambda c: pl.BlockSpec((None, None, None, ncp, NSA_HD), lambda bi, g, i, l: (bi, c, g, 0, 0))
    return pl.pallas_call(
        kern,
        grid_spec=pltpu.PrefetchScalarGridSpec(
            num_scalar_prefetch=1, grid=(b, NSA_KVH, t // qb),
            in_specs=[pl.BlockSpec((None, qb, hw), lambda bi, g, i, l: (bi, i, C_QNSA // hw + g)),
                      pl.BlockSpec((None, qb, LANES), lambda bi, g, i, l: (bi, i, C_SMALL_A // LANES)),
                      pl.BlockSpec((None, qb, hw), lambda bi, g, i, l: (bi, i, C_GNSA // hw + g)),
                      cspec(0), cspec(1),
                      kvspec(tk, ks_col), kvspec(tk, vs_col), kvspec(tw, kw_col), kvspec(tw, vw_col)],
            out_specs=pl.BlockSpec((None, qb, hw), lambda bi, g, i, l: (bi, i, g))),
        out_shape=jax.ShapeDtypeStruct((b, t, W_BR), ydtype),
        compiler_params=_cparams(("parallel", "parallel", "arbitrary")),
        name="nsa_attention",
    )(lidx, z3, z3, z3, cmp_kv, cmp_kv, ks_src, ks_src, kw_src, kw_src)


def _merge_kernel(l_ref, y0_ref, y1_ref, y2_ref, y3_ref, g0_ref, g1_ref, g2_ref, g3_ref, w_ref, o_ref):
    acc = None
    for m, (y_ref, g_ref) in enumerate(((y0_ref, g0_ref), (y1_ref, g1_ref), (y2_ref, g2_ref), (y3_ref, g3_ref))):
        term = _sigmoid(g_ref[...]) * _dot(y_ref[...].astype(BF16), w_ref[m])
        acc = term if acc is None else acc + term
    o_ref[...] = acc.astype(o_ref.dtype)


def _merge(lidx, ys, z2d, p):
    n = z2d.shape[0]
    tm = min(n, 256)
    yspec = pl.BlockSpec((tm, W_BR), lambda i, l: (i, 0))
    gspec = lambda m: pl.BlockSpec((tm, D_MODEL), lambda i, l: (i, C_MERGE // D_MODEL + m))
    wspec = pl.BlockSpec((None, N_BRANCH, W_BR, D_MODEL), lambda i, l: (l[0], 0, 0, 0), pipeline_mode=pl.Buffered(1))
    return pl.pallas_call(
        _merge_kernel,
        grid_spec=pltpu.PrefetchScalarGridSpec(
            num_scalar_prefetch=1, grid=(n // tm,),
            in_specs=[yspec] * N_BRANCH + [gspec(m) for m in range(N_BRANCH)] + [wspec],
            out_specs=pl.BlockSpec((tm, D_MODEL), lambda i, l: (i, 0))),
        out_shape=jax.ShapeDtypeStruct((n, D_MODEL), BF16),
        compiler_params=_cparams(("parallel",)),
        name="merge",
    )(lidx, *ys, z2d, z2d, z2d, z2d, p["w_branch"])


def _outproj_kernel(l_ref, m_ref, x_ref, w_ref, g_ref, b_ref, o_ref, *, alpha):
    v = alpha * x_ref[...] + _dot(m_ref[...], w_ref[...])
    mu = jnp.mean(v, axis=-1, keepdims=True)
    c = v - mu
    var = jnp.mean(c * c, axis=-1, keepdims=True)
    o_ref[...] = c * lax.rsqrt(var + 1e-5) * g_ref[...] + b_ref[...]


def _outproj(lidx, merged, x2d, p, alpha):
    n = x2d.shape[0]
    tm = min(n, 256)
    return pl.pallas_call(
        functools.partial(_outproj_kernel, alpha=alpha),
        grid_spec=pltpu.PrefetchScalarGridSpec(
            num_scalar_prefetch=1, grid=(n // tm,),
            in_specs=[pl.BlockSpec((tm, D_MODEL), lambda i, l: (i, 0)),
                      pl.BlockSpec((tm, D_MODEL), lambda i, l: (i, 0)),
                      pl.BlockSpec((None, D_MODEL, D_MODEL), lambda i, l: (l[0], 0, 0)),
                      pl.BlockSpec((None, 1, D_MODEL), lambda i, l: (l[0], 0, 0)),
                      pl.BlockSpec((None, 1, D_MODEL), lambda i, l: (l[0], 0, 0))],
            out_specs=pl.BlockSpec((tm, D_MODEL), lambda i, l: (i, 0))),
        out_shape=jax.ShapeDtypeStruct((n, D_MODEL), F32),
        compiler_params=_cparams(("parallel",)),
        name="outproj_ln",
    )(lidx, merged, x2d, p["w_out"], p["ln_g"], p["ln_b"])


def _prep_params(w):
    p = {}
    row = lambda a: a[:, None, :]
    p["lru_conv_w"] = w["lru_conv_w"]
    p["lru_conv_b"] = row(w["lru_conv_b"])
    p["lru_wa"] = w["lru_wa"].astype(BF16)
    p["lru_ba"] = row(w["lru_ba"])
    p["lru_wx"] = w["lru_wx"].astype(BF16)
    p["lru_bx"] = row(w["lru_bx"])
    p["lru_lambda"] = row(w["lru_lambda"])
    p.update(_s5_weights(w))
    p["w_branch"] = w["w_branch"].astype(BF16)
    p["w_out"] = w["w_out"].astype(BF16)
    p["ln_g"] = row(w["ln_g"])
    p["ln_b"] = row(w["ln_b"])
    dd = w["gdn_a_log"].shape[0]
    lane_row = lambda a, off: jnp.zeros((dd, 1, LANES), F32).at[:, 0, off:off + a.shape[-1]].set(a)
    p["gdn_conv_w"] = w["gdn_conv_w"]
    p["gdn_a_log"] = lane_row(w["gdn_a_log"], SM_A)
    p["gdn_dt_bias"] = lane_row(w["gdn_dt_bias"], SM_A)
    p["gdn_norm_w"] = row(w["gdn_norm_w"])
    flat = CMP_STRIDE * NSA_HD
    w1 = w["nsa_cmp_w1"].reshape(dd, 2, 2, flat, NSA_HD).transpose(0, 1, 3, 2, 4)
    p["nsa_w1"] = w1.reshape(dd, 2, flat, 2 * NSA_HD).astype(BF16)
    pe = w["nsa_cmp_pos"].reshape(dd, 2, flat)
    p["nsa_bias"] = _cmp_bias(jnp.concatenate([pe, jnp.zeros((dd, SUBLANES - 2, flat), F32)], axis=1), p["nsa_w1"])
    p["nsa_w2"] = w["nsa_cmp_w2"].astype(BF16)
    return p


def _mixer_layer(lidx, x2d, s, t, state, nsa_branch, p, win, alpha, ydtype):
    lru_buf, lru_h, gdn_buf, gdn_s, s5_re, s5_im = state
    z = _inproj(lidx, win[0], x2d, win[1])
    z3 = z.reshape(s, t, NP)
    y_lru, lru_buf, lru_h = _lru(lidx, z3, lru_buf, lru_h, p, ydtype)
    y_nsa = nsa_branch(z3)
    y_gdn, gdn_buf, gdn_s = _gdn(lidx, z3, gdn_buf, gdn_s, p, ydtype, GDN_CHUNK)
    y_s5, s5_re, s5_im = _s5_scan(lidx, z3, s5_re, s5_im, p, s5_re.shape[1])
    y_s5 = _s5_glu(lidx, y_s5, z, p, ydtype)
    flat = lambda y: y.reshape(s * t, W_BR)
    merged = _merge(lidx, (flat(y_lru), flat(y_nsa), flat(y_gdn), y_s5), z, p)
    x_new = _outproj(lidx, merged, x2d, p, alpha)
    return x_new, z3, (lru_buf, lru_h, gdn_buf, gdn_s, s5_re, s5_im)


def kernel(x_prompt, x_sample, state_lru_h, state_lru_conv, cache_nsa_kv, cache_win_kv, state_gdn_s, state_gdn_conv, state_s5_re, state_s5_im, page_table, w_in, lru_conv_w, lru_conv_b, lru_wa, lru_ba, lru_wx, lru_bx, lru_lambda, nsa_cmp_pos, nsa_cmp_w1, nsa_cmp_w2, gdn_conv_w, gdn_a_log, gdn_dt_bias, gdn_norm_w, s5_lam_re, s5_lam_im, s5_log_dt, s5_b_re, s5_b_im, s5_c_re, s5_c_im, s5_d, s5_glu_w, w_branch, w_out, ln_g, ln_b):
    depth = w_in.shape[0]
    bp, tp, _ = x_prompt.shape
    db, ts, _ = x_sample.shape
    n_pages = page_table.shape[1]
    past = n_pages * PAGE_SIZE
    wbuf = cache_win_kv.shape[2]
    alpha = (2.0 * depth) ** 0.25
    kvw = 4 * NSA_KVH * NSA_HD
    winw = 2 * NSA_KVH * NSA_HD
    p = _prep_params(dict(
        lru_conv_w=lru_conv_w, lru_conv_b=lru_conv_b, lru_wa=lru_wa, lru_ba=lru_ba, lru_wx=lru_wx, lru_bx=lru_bx,
        lru_lambda=lru_lambda, nsa_cmp_pos=nsa_cmp_pos, nsa_cmp_w1=nsa_cmp_w1, nsa_cmp_w2=nsa_cmp_w2,
        gdn_conv_w=gdn_conv_w, gdn_a_log=gdn_a_log, gdn_dt_bias=gdn_dt_bias, gdn_norm_w=gdn_norm_w,
        s5_lam_re=s5_lam_re, s5_lam_im=s5_lam_im, s5_log_dt=s5_log_dt, s5_b_re=s5_b_re, s5_b_im=s5_b_im,
        s5_c_re=s5_c_re, s5_c_im=s5_c_im, s5_d=s5_d, s5_glu_w=s5_glu_w, w_branch=w_branch, w_out=w_out,
        ln_g=ln_g, ln_b=ln_b))
    assert w_in.shape[2] == sum(s[1] for s in IN_SEGMENTS)
    wp = (jnp.asarray(_tile_sources(), jnp.int32), jnp.swapaxes(w_in, 1, 2))
    p = lax.optimization_barrier(p)
    pt_flat = page_table.reshape(-1).astype(jnp.int32)
    sb = db if ts == S5_L else 1
    c0 = C_KV // LANES
    zero_state = (jnp.zeros((bp, CONV_K - 1, W_BR), F32), jnp.zeros((bp, 1, W_BR), F32),
                  jnp.zeros((bp, CONV_K - 1, 3 * W_BR), F32), jnp.zeros((bp, GDN_H, GDN_HD, GDN_HD), F32),
                  jnp.zeros((bp, 1, S5_G * S5_P), F32), jnp.zeros((bp, 1, S5_G * S5_P), F32))

    def layer(carry, l):
        xp, xs = carry
        lidx = l.reshape(1)
        at = lambda a: lax.dynamic_index_in_dim(a, l, 0, keepdims=False)

        def nsa_prompt(z3):
            ckv = _compress(lidx, z3, c0, tp, p)
            return _attention(lidx, z3, ckv, z3, c0 + 4, c0 + 6, z3, c0 + 8, c0 + 10, 0, 0, BF16)

        xp, zp3, st_p = _mixer_layer(lidx, xp, bp, tp, zero_state, nsa_prompt, p, wp, alpha, BF16)

        win_state = {}

        def nsa_sample(z3):
            cmp_rows, sel_rows = _gather_ctx(lidx, pt_flat, cache_nsa_kv, z3)
            ckv = _compress(lidx, cmp_rows, None, past, p)
            win = jnp.concatenate([at(cache_win_kv).reshape(db, wbuf, winw),
                                   z3[:, :, C_KV + kvw:C_KV + kvw + winw]], axis=1)
            win_state["win"] = win
            pad = max(0, WINDOW + Q_BLOCK - (wbuf + ts))
            win_pad = jnp.concatenate([win, jnp.zeros((db, pad, winw), F32)], axis=1)
            return _attention(lidx, z3, ckv, sel_rows, 0, 2, win_pad, 0, 2, past, past - wbuf, F32)

        st_in = (at(state_lru_conv), at(state_lru_h)[:, None, :], at(state_gdn_conv), at(state_gdn_s),
                 at(state_s5_re).reshape(db // sb, sb, S5_G * S5_P), at(state_s5_im).reshape(db // sb, sb, S5_G * S5_P))
        xs, zs3, st_s = _mixer_layer(lidx, xs, db, ts, st_in, nsa_sample, p, wp, alpha, F32)

        def outs(st, z3, s, t, win):
            lru_buf, lru_h, gdn_buf, gdn_s, s5_re, s5_im = st
            return (lru_h.reshape(s, W_BR), lru_buf,
                    z3[:, :, C_KV:C_KV + kvw].reshape(s, t, 4, NSA_KVH, NSA_HD),
                    win.reshape(s, win.shape[1], 2, NSA_KVH, NSA_HD),
                    gdn_s, gdn_buf, s5_re.reshape(s, S5_G, S5_P), s5_im.reshape(s, S5_G, S5_P))

        win_p = zp3[:, tp - min(WINDOW, tp):, C_KV + kvw:C_KV + kvw + winw]
        win_s = win_state["win"][:, wbuf + ts - min(WINDOW, past + ts):]
        return (xp, xs), (outs(st_p, zp3, bp, tp, win_p), outs(st_s, zs3, db, ts, win_s))

    (xp, xs), (op, os_) = lax.scan(layer, (x_prompt.reshape(bp * tp, D_MODEL), x_sample.reshape(db * ts, D_MODEL)),
                                   jnp.arange(depth, dtype=jnp.int32))
    return (xp.reshape(bp, tp, D_MODEL), xs.reshape(db, ts, D_MODEL)) + tuple(op) + tuple(os_)
```

```python
import functools

import jax
import jax.numpy as jnp
from jax import lax
from jax.experimental import pallas as pl
from jax.experimental.pallas import tpu as pltpu

F32 = jnp.float32
BF16 = jnp.bfloat16
HI = lax.Precision.HIGHEST

D_MODEL = 2048
W_BR = D_MODEL // 2
N_BRANCH = 4
CONV_K = 4
LRU_BLOCKS = 8
LRU_BS = W_BR // LRU_BLOCKS
LRU_C = 8.0
NSA_HD = 128
NSA_H = 8
NSA_KVH = 2
NSA_GQ = NSA_H // NSA_KVH
CMP_STRIDE = 16
CMP_BLOCK = 32
SEL_BLOCK = 64
CMP_PER_SEL = SEL_BLOCK // CMP_STRIDE
TOP_N = 16
WINDOW = 512
Q_BLOCK = 128
PAGE_SIZE = 128
GDN_HD = 128
GDN_H = 8
GDN_CHUNK = 64
GDN_SPLIT_LEVELS = 2
S5_GS = 16
S5_G = W_BR // S5_GS
S5_P = 64
S5_L = 8
NEG = -1e30
FORCE = 1e9
LOG2E = 1.4426950408889634

LANES = 128
SUBLANES = 8
VMEM_LIMIT = 56 * 1024 * 1024

IN_TILE = 512
C_MERGE = 0
C_ULRU = 8192
C_GLRU = 9216
C_QNSA = 10240
C_GNSA = 11264
C_GGDN = 12288
C_US5 = 13312
C_GS5 = 14336
C_QKV = 15360
C_KV = 18432
C_SMALL_A = 19968
C_SMALL_B = 20480
NP = C_SMALL_B + IN_TILE
IN_SEGMENTS = (("u_lru", W_BR, C_ULRU), ("g_lru", W_BR, C_GLRU), ("q_nsa", W_BR, C_QNSA),
               ("kv", 6 * NSA_KVH * NSA_HD, C_KV), ("gl", 3 * NSA_H, None), ("g_nsa", W_BR, C_GNSA),
               ("qkv", 3 * W_BR, C_QKV), ("a", GDN_H, None), ("b", GDN_H, None),
               ("g_gdn", W_BR, C_GGDN), ("u_s5", W_BR, C_US5), ("g_s5", W_BR, C_GS5),
               ("merge", N_BRANCH * D_MODEL, C_MERGE))


def _segment_start(name):
    return sum(w for n, w, _ in IN_SEGMENTS[:[s[0] for s in IN_SEGMENTS].index(name)])


SRC_SMALL_A = _segment_start("gl") // LANES * LANES
SRC_SMALL_B = _segment_start("a") // LANES * LANES
SM_GL = _segment_start("gl") - SRC_SMALL_A
SM_A = _segment_start("a") - SRC_SMALL_B
SM_B = _segment_start("b") - SRC_SMALL_B


def _tile_sources():
    src = [None] * (NP // IN_TILE)
    at = 0
    for _, width, dst in IN_SEGMENTS:
        if dst is not None:
            for c in range(0, width, IN_TILE):
                src[(dst + c) // IN_TILE] = at + c
        at += width
    src[C_SMALL_A // IN_TILE] = SRC_SMALL_A
    src[C_SMALL_B // IN_TILE] = SRC_SMALL_B
    assert all(s is not None and s % SUBLANES == 0 and s + IN_TILE <= at for s in src)
    return src


def _cparams(sem):
    return pltpu.CompilerParams(dimension_semantics=sem, vmem_limit_bytes=VMEM_LIMIT)


def _sigmoid(x):
    return 0.5 + 0.5 * jnp.tanh(0.5 * x)


def _silu(x):
    return x * _sigmoid(x)


def _softplus(x):
    return jnp.maximum(x, 0.0) + jnp.log1p(jnp.exp(-jnp.abs(x)))


def _dot(a, b):
    return jnp.dot(a, b, preferred_element_type=F32)


def _dot_hi(a, b):
    return jnp.dot(a, b, preferred_element_type=F32, precision=HI)


def _dot_nt(a, b):
    return lax.dot_general(a, b, (((1,), (1,)), ((), ())), preferred_element_type=F32)


def _dot_tn(a, b):
    return lax.dot_general(a, b, (((0,), (0,)), ((), ())), preferred_element_type=F32)


def _iota(shape, axis):
    return lax.broadcasted_iota(jnp.int32, shape, axis)


def _log2(n):
    assert n & (n - 1) == 0
    return n.bit_length() - 1


def _inproj_kernel(l_ref, src_ref, x_ref, w_ref, o_ref, xb_ref):
    @pl.when(pl.program_id(1) == 0)
    def _():
        xb_ref[...] = x_ref[...].astype(BF16)

    o_ref[...] = _dot_nt(xb_ref[...], w_ref[...].astype(BF16))


def _inproj(lidx, srcs, x2d, wt):
    n = x2d.shape[0]
    tm = min(n, 2048)
    wspec = pl.BlockSpec((pl.Squeezed(), pl.Element(IN_TILE), pl.Element(D_MODEL)),
                         lambda i, j, l, src: (l[0], pl.multiple_of(src[j], SUBLANES), 0))
    xspec = pl.BlockSpec((tm, D_MODEL), lambda i, j, l, src: (i, 0), pipeline_mode=pl.Buffered(1))
    return pl.pallas_call(
        _inproj_kernel,
        grid_spec=pltpu.PrefetchScalarGridSpec(
            num_scalar_prefetch=2, grid=(n // tm, NP // IN_TILE),
            in_specs=[xspec, wspec],
            out_specs=pl.BlockSpec((tm, IN_TILE), lambda i, j, l, src: (i, j)),
            scratch_shapes=[pltpu.VMEM((tm, D_MODEL), BF16)]),
        out_shape=jax.ShapeDtypeStruct((n, NP), F32),
        compiler_params=_cparams(("parallel", "arbitrary")),
        name="inproj",
    )(lidx, srcs, x2d, wt)


def _lru_kernel(l_ref, u_ref, g_ref, buf_ref, h0_ref, cw_ref, cb_ref, wa_ref, ba_ref, wx_ref, bx_ref, lam_ref,
                y_ref, bufo_ref, ho_ref, xp_scr, a_scr, b_scr, h_scr, *, tt, nt):
    ti = pl.program_id(1)

    @pl.when(ti == 0)
    def _():
        xp_scr[5:8, :] = buf_ref[...]
        h_scr[...] = h0_ref[...]

    u = u_ref[...]
    xp_scr[8:8 + tt, :] = u
    cw = cw_ref[...]
    xc = (cb_ref[...] + cw[3:4] * u + cw[2:3] * xp_scr[7:7 + tt, :]
          + cw[1:2] * xp_scr[6:6 + tt, :] + cw[0:1] * xp_scr[5:5 + tt, :])
    tail = u[tt - 3:tt, :]
    xp_scr[5:8, :] = tail
    sp = _softplus(-lam_ref[...])
    for n in range(LRU_BLOCKS):
        sl = slice(n * LRU_BS, (n + 1) * LRU_BS)
        xn = xc[:, sl]
        xb = xn.astype(BF16)
        r = _sigmoid(_dot(xb, wa_ref[n]) + ba_ref[:, sl])
        i = _sigmoid(_dot(xb, wx_ref[n]) + bx_ref[:, sl])
        a = jnp.exp(-LRU_C * r * sp[:, sl])
        a_scr[:, sl] = a
        b_scr[:, sl] = jnp.sqrt(1.0 - a * a) * (i * xn)

    def body(i, h):
        for k in range(SUBLANES):
            t = i * SUBLANES + k
            h = a_scr[pl.ds(t, 1), :] * h + b_scr[pl.ds(t, 1), :]
            b_scr[pl.ds(t, 1), :] = h
        return h

    h = lax.fori_loop(0, tt // SUBLANES, body, h_scr[...])
    h_scr[...] = h
    y_ref[...] = (b_scr[...] * _silu(g_ref[...])).astype(y_ref.dtype)

    @pl.when(ti == nt - 1)
    def _():
        bufo_ref[...] = tail
        ho_ref[...] = h


def _lru(lidx, z3, buf, h0, p, ydtype):
    s, t, _ = z3.shape
    tt = min(t, 512)
    nt = t // tt
    wspec = lambda shape: pl.BlockSpec((None,) + shape, lambda b, i, l: (l[0],) + (0,) * len(shape))
    return pl.pallas_call(
        functools.partial(_lru_kernel, tt=tt, nt=nt),
        grid_spec=pltpu.PrefetchScalarGridSpec(
            num_scalar_prefetch=1, grid=(s, nt),
            in_specs=[pl.BlockSpec((None, tt, W_BR), lambda b, i, l: (b, i, C_ULRU // W_BR)),
                      pl.BlockSpec((None, tt, W_BR), lambda b, i, l: (b, i, C_GLRU // W_BR)),
                      pl.BlockSpec((None, 3, W_BR), lambda b, i, l: (b, 0, 0)),
                      pl.BlockSpec((None, 1, W_BR), lambda b, i, l: (b, 0, 0)),
                      wspec((CONV_K, W_BR)), wspec((1, W_BR)),
                      wspec((LRU_BLOCKS, LRU_BS, LRU_BS)), wspec((1, W_BR)),
                      wspec((LRU_BLOCKS, LRU_BS, LRU_BS)), wspec((1, W_BR)), wspec((1, W_BR))],
            out_specs=[pl.BlockSpec((None, tt, W_BR), lambda b, i, l: (b, i, 0)),
                       pl.BlockSpec((None, 3, W_BR), lambda b, i, l: (b, 0, 0)),
                       pl.BlockSpec((None, 1, W_BR), lambda b, i, l: (b, 0, 0))],
            scratch_shapes=[pltpu.VMEM((tt + 8, W_BR), F32), pltpu.VMEM((tt, W_BR), F32),
                            pltpu.VMEM((tt, W_BR), F32), pltpu.VMEM((1, W_BR), F32)]),
        out_shape=[jax.ShapeDtypeStruct((s, t, W_BR), ydtype),
                   jax.ShapeDtypeStruct((s, 3, W_BR), F32),
                   jax.ShapeDtypeStruct((s, 1, W_BR), F32)],
        compiler_params=_cparams(("parallel", "arbitrary")),
        name="rglru",
    )(lidx, z3, z3, buf, h0, p["lru_conv_w"], p["lru_conv_b"], p["lru_wa"], p["lru_ba"],
      p["lru_wx"], p["lru_bx"], p["lru_lambda"])


S5_CB = LANES // S5_GS
S5_SW = S5_CB * S5_P


def _gelu_tanh(x):
    return 0.5 * x * (1.0 + jnp.tanh(0.7978845608028654 * (x + 0.044715 * (x * x * x))))


def _s5_kernel(l_ref, u_ref, h0re_ref, h0im_ref, wst_ref, vout_ref, kt_ref, alre_ref, alim_ref, d_ref,
               y_ref, hre_ref, him_ref, hin_scr, s_scr, *, n, sb):
    rows = sb * n
    us = [u_ref[pl.ds(j, rows, stride=S5_L), :] for j in range(S5_L)]
    ub = jnp.concatenate(us, axis=1).astype(BF16)
    s = _dot(ub, wst_ref[...])
    alre = alre_ref[...]
    alim = alim_ref[...]
    h0re = h0re_ref[...]
    h0im = h0im_ref[...]
    if n == 1:
        hin_scr[:, :S5_SW] = h0re
        hin_scr[:, S5_SW:] = h0im
        hre = alre * h0re - alim * h0im + s[:, :S5_SW]
        him = alre * h0im + alim * h0re + s[:, S5_SW:]
    else:
        s_scr[...] = s

        def body(c, carry):
            hre, him = carry
            hin_scr[pl.ds(c, 1), :S5_SW] = hre
            hin_scr[pl.ds(c, 1), S5_SW:] = him
            srow = s_scr[pl.ds(c, 1), :]
            return (alre * hre - alim * him + srow[:, :S5_SW],
                    alre * him + alim * hre + srow[:, S5_SW:])

        hre, him = lax.fori_loop(0, n, body, (h0re, h0im))
    hre_ref[...] = hre
    him_ref[...] = him
    ycat = _dot(hin_scr[...].astype(BF16), vout_ref[...]) + _dot(ub, kt_ref[...])
    d = d_ref[...]
    for j in range(S5_L):
        yj = ycat[:, j * LANES:(j + 1) * LANES] + d * us[j]
        y_ref[pl.ds(j, rows, stride=S5_L), :] = _gelu_tanh(yj)


def _s5_scan(lidx, z3, h0re, h0im, p, sb):
    s, t, _ = z3.shape
    n = t // S5_L
    assert sb == 1 or n == 1
    sg = s // sb
    zr = z3.reshape(sg, sb * t, NP)
    ncb = W_BR // LANES
    wspec = lambda shape: pl.BlockSpec((None, None) + shape, lambda b, c, l: (l[0], c) + (0,) * len(shape))
    hspec = pl.BlockSpec((None, sb, S5_SW), lambda b, c, l: (b, 0, c))
    y, hre, him = pl.pallas_call(
        functools.partial(_s5_kernel, n=n, sb=sb),
        grid_spec=pltpu.PrefetchScalarGridSpec(
            num_scalar_prefetch=1, grid=(sg, ncb),
            in_specs=[pl.BlockSpec((None, sb * t, LANES), lambda b, c, l: (b, 0, C_US5 // LANES + c)),
                      hspec, hspec,
                      wspec((S5_L * LANES, 2 * S5_SW)), wspec((2 * S5_SW, S5_L * LANES)),
                      wspec((S5_L * LANES, S5_L * LANES)), wspec((1, S5_SW)), wspec((1, S5_SW)),
                      wspec((1, LANES))],
            out_specs=[pl.BlockSpec((None, sb * t, LANES), lambda b, c, l: (b, 0, c)), hspec, hspec],
            scratch_shapes=[pltpu.VMEM((sb * n, 2 * S5_SW), F32), pltpu.VMEM((sb * n, 2 * S5_SW), F32)]),
        out_shape=[jax.ShapeDtypeStruct((sg, sb * t, W_BR), F32),
                   jax.ShapeDtypeStruct((sg, sb, S5_G * S5_P), F32),
                   jax.ShapeDtypeStruct((sg, sb, S5_G * S5_P), F32)],
        compiler_params=_cparams(("parallel", "arbitrary")),
        name="s5_scan",
    )(lidx, zr, h0re, h0im, p["s5_wst"], p["s5_vout"], p["s5_kt"], p["s5_alre"], p["s5_alim"], p["s5_d"])
    return y.reshape(s * t, W_BR), hre, him


def _s5_glu_kernel(l_ref, y_ref, g_ref, w_ref, o_ref):
    gl = _dot(y_ref[...].astype(BF16), w_ref[...])
    o_ref[...] = (gl[:, :W_BR] * _sigmoid(gl[:, W_BR:]) * _silu(g_ref[...])).astype(o_ref.dtype)


def _s5_glu(lidx, y2d, z2d, p, ydtype):
    n = y2d.shape[0]
    tm = min(n, 512)
    return pl.pallas_call(
        _s5_glu_kernel,
        grid_spec=pltpu.PrefetchScalarGridSpec(
            num_scalar_prefetch=1, grid=(n // tm,),
            in_specs=[pl.BlockSpec((tm, W_BR), lambda i, l: (i, 0)),
                      pl.BlockSpec((tm, W_BR), lambda i, l: (i, C_GS5 // W_BR)),
                      pl.BlockSpec((None, W_BR, 2 * W_BR), lambda i, l: (l[0], 0, 0))],
            out_specs=pl.BlockSpec((tm, W_BR), lambda i, l: (i, 0))),
        out_shape=jax.ShapeDtypeStruct((n, W_BR), ydtype),
        compiler_params=_cparams(("parallel",)),
        name="s5_glu",
    )(lidx, y2d, z2d, p["s5_glu_w"])


def _s5_expand_kernel(wst_ref, vout_ref, kt_ref, wst_o, vout_o, kt_o):
    n = S5_L * LANES
    lp = S5_P.bit_length() - 1
    lc = S5_GS.bit_length() - 1
    lg = S5_CB.bit_length() - 1
    row = _iota((n, n), 0)
    col = _iota((n, n), 1)
    src = _iota((LANES, n), 0)
    dst = _iota((LANES, n), 1)
    rep_state = jnp.where(src == (((dst >> (lp + lg)) << lp) | (dst & (S5_P - 1))), 1.0, 0.0).astype(BF16)
    rep_out = jnp.where(src == (((dst >> (lc + lg)) << lc) | (dst & (S5_GS - 1))), 1.0, 0.0).astype(BF16)
    gi_in_row = (row >> lc) & (S5_CB - 1)
    gi_state_row = (row >> lp) & (S5_CB - 1)
    gi_state_col = (col >> lp) & (S5_CB - 1)
    gi_out_col = (col >> lc) & (S5_CB - 1)
    wst_o[...] = jnp.where(gi_in_row == gi_state_col, _dot(wst_ref[...].astype(BF16), rep_state), 0.0).astype(BF16)
    vout_o[...] = jnp.where(gi_state_row == gi_out_col, _dot(vout_ref[...].astype(BF16), rep_out), 0.0).astype(BF16)
    kt_o[...] = jnp.where(gi_in_row == gi_out_col, _dot(kt_ref[...].astype(BF16), rep_out), 0.0).astype(BF16)


def _s5_expand(wst, vout, kt):
    dd, ncb, n, _ = wst.shape
    cspec = pl.BlockSpec((None, None, n, LANES), lambda d, c: (d, c, 0, 0))
    ospec = pl.BlockSpec((None, None, n, n), lambda d, c: (d, c, 0, 0))
    oshape = jax.ShapeDtypeStruct((dd, ncb, n, n), BF16)
    return pl.pallas_call(
        _s5_expand_kernel, grid=(dd, ncb), in_specs=[cspec, cspec, cspec], out_specs=[ospec, ospec, ospec],
        out_shape=[oshape, oshape, oshape], compiler_params=_cparams(("parallel", "parallel")),
        name="s5_expand",
    )(wst, vout, kt)


def _s5_weights(w):
    dt = jnp.exp(w["s5_log_dt"])[..., None]
    lr, li = w["s5_lam_re"], w["s5_lam_im"]
    mag = jnp.exp(lr * dt)
    a_re = mag * jnp.cos(li * dt)
    a_im = mag * jnp.sin(li * dt)
    den = lr * lr + li * li
    f_re = ((a_re - 1.0) * lr + a_im * li) / den
    f_im = (a_im * lr - (a_re - 1.0) * li) / den
    bb_re = f_re[..., None] * w["s5_b_re"] - f_im[..., None] * w["s5_b_im"]
    bb_im = f_re[..., None] * w["s5_b_im"] + f_im[..., None] * w["s5_b_re"]
    pw_re = [jnp.ones_like(a_re)]
    pw_im = [jnp.zeros_like(a_im)]
    for _ in range(S5_L):
        pr, pi = pw_re[-1], pw_im[-1]
        pw_re.append(pr * a_re - pi * a_im)
        pw_im.append(pr * a_im + pi * a_re)
    pw_re = jnp.stack(pw_re, axis=1)
    pw_im = jnp.stack(pw_im, axis=1)
    dd = lr.shape[0]
    ncb = S5_G // S5_CB
    grp = lambda a: a.reshape(a.shape[0], a.shape[1], ncb, S5_CB, *a.shape[3:])
    rev_re = jnp.stack([pw_re[:, S5_L - 1 - j] for j in range(S5_L)], axis=1)
    rev_im = jnp.stack([pw_im[:, S5_L - 1 - j] for j in range(S5_L)], axis=1)
    st_re = rev_re[..., None] * bb_re[:, None] - rev_im[..., None] * bb_im[:, None]
    st_im = rev_re[..., None] * bb_im[:, None] + rev_im[..., None] * bb_re[:, None]

    def state_w(a):
        return grp(a).transpose(0, 2, 1, 3, 5, 4).reshape(dd, ncb, S5_L * LANES, S5_P)

    wst = jnp.concatenate([state_w(st_re), state_w(st_im)], axis=-1)
    c_re, c_im = w["s5_c_re"], w["s5_c_im"]
    nx_re, nx_im = pw_re[:, 1:], pw_im[:, 1:]
    ca_re = c_re[:, None] * nx_re[:, :, :, None] - c_im[:, None] * nx_im[:, :, :, None]
    ca_im = c_re[:, None] * nx_im[:, :, :, None] + c_im[:, None] * nx_re[:, :, :, None]

    def out_w(a):
        return grp(a).transpose(0, 2, 3, 5, 1, 4).reshape(dd, ncb, S5_SW, S5_L * S5_GS)

    vout = jnp.concatenate([out_w(ca_re), out_w(-ca_im)], axis=2)
    cat_re = c_re[:, None] * pw_re[:, :S5_L, :, None] - c_im[:, None] * pw_im[:, :S5_L, :, None]
    cat_im = c_re[:, None] * pw_im[:, :S5_L, :, None] + c_im[:, None] * pw_re[:, :S5_L, :, None]
    kk = jnp.einsum("dlgcp,dgpe->dlgce", jnp.concatenate([cat_re, -cat_im], axis=-1),
                    jnp.concatenate([bb_re, bb_im], axis=2), precision=HI)
    zero = jnp.zeros_like(kk[:, 0])
    kt = jnp.stack([jnp.stack([kk[:, j - i] if j >= i else zero for j in range(S5_L)], axis=1)
                    for i in range(S5_L)], axis=1)
    kt = kt.reshape(dd, S5_L, S5_L, ncb, S5_CB, S5_GS, S5_GS)
    kt = kt.transpose(0, 3, 1, 4, 6, 2, 5).reshape(dd, ncb, S5_L * LANES, S5_L * S5_GS)
    wst, vout, kt = _s5_expand(wst, vout, kt)
    sw = lambda a: a.reshape(dd, ncb, 1, S5_SW)
    return dict(s5_wst=wst, s5_vout=vout, s5_kt=kt, s5_alre=sw(pw_re[:, S5_L]), s5_alim=sw(pw_im[:, S5_L]),
                s5_d=w["s5_d"].reshape(dd, W_BR // LANES, 1, LANES), s5_glu_w=w["s5_glu_w"].astype(BF16))


def _unit_lower_solve(ms, rhs, c):
    def split(a):
        hi = a.astype(BF16)
        return hi, (a - hi.astype(F32)).astype(BF16)

    def dot3(a, b):
        return _dot(a[0], b[0]) + _dot(a[1], b[0]) + _dot(a[0], b[1])

    n = range(len(ms))
    ms = [split(m) for m in ms]
    rs = [split(r) for r in rhs]
    xs = [rhs[i] - dot3(ms[i], rs[i]) for i in n]
    k = 2
    while k < c:
        if k <= 2 ** GDN_SPLIT_LEVELS:
            ms = [split(dot3(ms[i], ms[i])) for i in n]
            rs = [split(x) for x in xs]
            xs = [xs[i] + dot3(ms[i], rs[i]) for i in n]
        else:
            ms = [(_dot(ms[i][0], ms[i][0]).astype(BF16), None) for i in n]
            xs = [xs[i] + _dot(ms[i][0], xs[i].astype(BF16)) for i in n]
        k *= 2
    return xs


def _gdn_kernel(l_ref, q_ref, k_ref, v_ref, sm_ref, gg_ref, bq_ref, bk_ref, bv_ref, cwq_ref, cwk_ref, cwv_ref,
                alog_ref, dtb_ref, nw_ref, s0_ref, y_ref, bufo_ref, so_ref,
                xp_scr, qkv_scr, gx_scr, bx_scr, gc_scr, s_scr, *, tt, nt, c):
    ti = pl.program_id(1)
    log2c = c.bit_length() - 1

    @pl.when(ti == 0)
    def _():
        for i, b_ref in enumerate((bq_ref, bk_ref, bv_ref)):
            xp_scr[i, 5:8, :] = b_ref[...]
        s_scr[...] = s0_ref[...]

    for i, (x_ref, cw_ref) in enumerate(((q_ref, cwq_ref), (k_ref, cwk_ref), (v_ref, cwv_ref))):
        x = x_ref[...]
        xp_scr[i, 8:8 + tt, :] = x
        cw = cw_ref[...]
        cv = (cw[3:4] * x + cw[2:3] * xp_scr[i, 7:7 + tt, :]
              + cw[1:2] * xp_scr[i, 6:6 + tt, :] + cw[0:1] * xp_scr[i, 5:5 + tt, :])
        tail = x[tt - 3:tt, :]
        xp_scr[i, 5:8, :] = tail
        bufo_ref[:, i * W_BR:(i + 1) * W_BR] = tail
        cv = _silu(cv)
        if i < 2:
            scale = GDN_HD ** -0.5 if i == 0 else 1.0
            for h in range(GDN_H):
                sl = slice(h * GDN_HD, (h + 1) * GDN_HD)
                xh = cv[:, sl]
                qkv_scr[i, :, sl] = xh * (lax.rsqrt(jnp.sum(xh * xh, axis=-1, keepdims=True) + 1e-6) * scale)
        else:
            qkv_scr[i] = cv

    sm = sm_ref[...]
    gsm = -jnp.exp(alog_ref[...]) * _softplus(sm + dtb_ref[...])
    bsm = _sigmoid(sm)
    src = _iota((LANES, W_BR), 0)
    head = _iota((LANES, W_BR), 1) >> _log2(GDN_HD)
    gx_scr[...] = _dot_hi(gsm, jnp.where(src - SM_A == head, 1.0, 0.0))
    bx_scr[...] = _dot_hi(bsm, jnp.where(src - SM_B == head, 1.0, 0.0))
    headc = _iota((LANES, GDN_H * c), 1) >> log2c
    gc_scr[...] = _dot_hi(gsm, jnp.where(_iota((LANES, GDN_H * c), 0) - SM_A == headc, 1.0, 0.0))

    rowi = _iota((c, c), 0)
    coli = _iota((c, c), 1)
    incl = coli <= rowi
    strict = coli < rowi
    ltri = jnp.where(incl, 1.0, 0.0)
    upper = jnp.where(_iota((c, GDN_H * c), 0) > (_iota((c, GDN_H * c), 1) & (c - 1)), 1.0, 0.0)
    nw = nw_ref[...]

    def chunk(ci, carry):
        r0 = pl.multiple_of(ci * c, c)
        gcb_all = _dot_hi(ltri, gx_scr[pl.ds(r0, c), :])
        diffs = _dot_hi(ltri, gc_scr[pl.ds(r0, c), :] * upper)
        beta_all = bx_scr[pl.ds(r0, c), :]
        hs = range(GDN_H)
        sls = [slice(h * GDN_HD, (h + 1) * GDN_HD) for h in hs]
        q = [qkv_scr[0, pl.ds(r0, c), sl] for sl in sls]
        k = [qkv_scr[1, pl.ds(r0, c), sl] for sl in sls]
        gcb = [gcb_all[:, sl] for sl in sls]
        decay = [jnp.where(incl, jnp.exp(diffs[:, h * c:(h + 1) * c]), 0.0) for h in hs]
        kb = [k[h] * beta_all[:, sls[h]] for h in hs]
        kbf = [x.astype(BF16) for x in k]
        kq = [_dot_nt(jnp.concatenate([kb[h], q[h]], axis=0).astype(BF16), kbf[h]) for h in hs]
        m = [jnp.where(strict, kq[h][:c] * decay[h], 0.0) for h in hs]
        qk = [(kq[h][c:] * decay[h]).astype(BF16) for h in hs]
        rhs = [jnp.concatenate([qkv_scr[2, pl.ds(r0, c), sls[h]] * beta_all[:, sls[h]], kb[h] * jnp.exp(gcb[h])],
                               axis=1) for h in hs]
        sol = _unit_lower_solve(m, rhs, c)
        s = [s_scr[h] for h in hs]
        sb = [x.astype(BF16) for x in s]
        ws = [_dot(jnp.concatenate([sol[h][:, GDN_HD:], q[h] * jnp.exp(gcb[h])], axis=0).astype(BF16), sb[h])
              for h in hs]
        vnb = [(sol[h][:, :GDN_HD] - ws[h][:c]).astype(BF16) for h in hs]
        o = [ws[h][c:] + _dot(qk[h], vnb[h]) for h in hs]
        for h in hs:
            glast = gcb[h][c - 1:c, :]
            s_scr[h] = s[h] * jnp.exp(glast) + _dot_tn((k[h] * jnp.exp(glast - gcb[h])).astype(BF16), vnb[h])
        for h in hs:
            on = o[h] * lax.rsqrt(jnp.mean(o[h] * o[h], axis=-1, keepdims=True) + 1e-6) * nw
            y_ref[pl.ds(r0, c), sls[h]] = (on * _silu(gg_ref[pl.ds(r0, c), sls[h]])).astype(y_ref.dtype)
        return carry

    lax.fori_loop(0, tt // c, chunk, 0)

    @pl.when(ti == nt - 1)
    def _():
        so_ref[...] = s_scr[...]


def _gdn(lidx, z3, buf, s0, p, ydtype, chunk):
    s, t, _ = z3.shape
    tt = min(t, 256)
    nt = t // tt
    c = min(chunk, tt)
    zspec = lambda col: pl.BlockSpec((None, tt, W_BR), lambda b, i, l: (b, i, col // W_BR))
    bspec = lambda j: pl.BlockSpec((None, 3, W_BR), lambda b, i, l: (b, 0, j))
    cspec = lambda j: pl.BlockSpec((None, CONV_K, W_BR), lambda b, i, l: (l[0], 0, j))
    rspec = pl.BlockSpec((None, 1, LANES), lambda b, i, l: (l[0], 0, 0))
    sspec = pl.BlockSpec((None, GDN_H, GDN_HD, GDN_HD), lambda b, i, l: (b, 0, 0, 0))
    return pl.pallas_call(
        functools.partial(_gdn_kernel, tt=tt, nt=nt, c=c),
        grid_spec=pltpu.PrefetchScalarGridSpec(
            num_scalar_prefetch=1, grid=(s, nt),
            in_specs=[zspec(C_QKV), zspec(C_QKV + W_BR), zspec(C_QKV + 2 * W_BR),
                      pl.BlockSpec((None, tt, LANES), lambda b, i, l: (b, i, C_SMALL_B // LANES)),
                      zspec(C_GGDN), bspec(0), bspec(1), bspec(2), cspec(0), cspec(1), cspec(2),
                      rspec, rspec, rspec, sspec],
            out_specs=[pl.BlockSpec((None, tt, W_BR), lambda b, i, l: (b, i, 0)),
                       pl.BlockSpec((None, 3, 3 * W_BR), lambda b, i, l: (b, 0, 0)),
                       sspec],
            scratch_shapes=[pltpu.VMEM((3, tt + 8, W_BR), F32), pltpu.VMEM((3, tt, W_BR), F32),
                            pltpu.VMEM((tt, W_BR), F32), pltpu.VMEM((tt, W_BR), F32),
                            pltpu.VMEM((tt, GDN_H * c), F32), pltpu.VMEM((GDN_H, GDN_HD, GDN_HD), F32)]),
        out_shape=[jax.ShapeDtypeStruct((s, t, W_BR), ydtype),
                   jax.ShapeDtypeStruct((s, 3, 3 * W_BR), F32),
                   jax.ShapeDtypeStruct((s, GDN_H, GDN_HD, GDN_HD), F32)],
        compiler_params=_cparams(("parallel", "arbitrary")),
        name="gdn",
    )(lidx, z3, z3, z3, z3, z3, buf, buf, buf, p["gdn_conv_w"], p["gdn_conv_w"], p["gdn_conv_w"],
      p["gdn_a_log"], p["gdn_dt_bias"], p["gdn_norm_w"], s0)


PAGES_PER_STEP = 8
MAX_TAIL_STEPS = 3


KEY_TILE = 512


def _key_tile(tk):
    return max(k for k in range(KEY_TILE, min(tk, LANES * SEL_BLOCK) + 1, KEY_TILE) if tk % k == 0)


def _gather_kernel(l_ref, pt_ref, *refs, n_past):
    pages = refs[:PAGES_PER_STEP]
    zn_ref, cmp_ref, sel_ref = refs[PAGES_PER_STEP:]
    i = pl.program_id(1)
    half = 2 * NSA_KVH * NSA_HD

    @pl.when(i < n_past)
    def _():
        per_tok = 4 * NSA_KVH
        for k, p_ref in enumerate(pages):
            rows = slice(k * PAGE_SIZE, (k + 1) * PAGE_SIZE)
            for r in (2, 3):
                for g in range(NSA_KVH):
                    x = p_ref[pl.ds(r * NSA_KVH + g, PAGE_SIZE, stride=per_tok), :]
                    cols = slice(((r - 2) * NSA_KVH + g) * NSA_HD, ((r - 2) * NSA_KVH + g + 1) * NSA_HD)
                    sel_ref[rows, cols] = x.astype(BF16)
        halves = PAGE_SIZE // CMP_STRIDE
        for r in (0, 1):
            for g in range(NSA_KVH):
                flat = [jnp.concatenate([p_ref[pl.ds(s * per_tok + r * NSA_KVH + g, halves,
                                                     stride=CMP_STRIDE * per_tok), :] for p_ref in pages], axis=0)
                        for s in range(CMP_STRIDE)]
                cmp_ref[r, g] = jnp.concatenate(flat, axis=1).astype(BF16)

    @pl.when(i >= n_past)
    def _():
        sel_ref[...] = jnp.zeros(sel_ref.shape, BF16)

    @pl.when(i == n_past)
    def _():
        new = zn_ref[:, half:2 * half]
        sel_ref[0:2 * SUBLANES, :] = jnp.concatenate([new, jnp.zeros_like(new)], axis=0).astype(BF16)


def _gather_ctx(lidx, pt_flat, cache, z3):
    b, t_new, _ = z3.shape
    assert t_new == SUBLANES
    n_pages = pt_flat.shape[0] // b
    assert n_pages % PAGES_PER_STEP == 0
    n_past = n_pages // PAGES_PER_STEP
    rows = PAGES_PER_STEP * PAGE_SIZE
    n_tail = min(range(1, MAX_TAIL_STEPS + 1), key=lambda k: ((n_past + k) * rows // _key_tile((n_past + k) * rows), k))
    width = 4 * NSA_KVH * NSA_HD
    half = width // 2
    n_pool = cache.shape[1]
    cache = cache.reshape(cache.shape[0] * n_pool, PAGE_SIZE * 4 * NSA_KVH, NSA_HD)

    def pspec(k):
        def imap(bi, i, l, pt):
            page = jnp.minimum(i * PAGES_PER_STEP + k, n_pages - 1)
            return (l[0] * n_pool + pt[bi * n_pages + page], 0, 0)
        return pl.BlockSpec((None, PAGE_SIZE * 4 * NSA_KVH, NSA_HD), imap)

    return pl.pallas_call(
        functools.partial(_gather_kernel, n_past=n_past),
        grid_spec=pltpu.PrefetchScalarGridSpec(
            num_scalar_prefetch=2, grid=(b, n_past + n_tail),
            in_specs=[pspec(k) for k in range(PAGES_PER_STEP)]
            + [pl.BlockSpec((None, t_new, width), lambda bi, i, l, pt: (bi, 0, C_KV // width))],
            out_specs=[pl.BlockSpec((None, 2, NSA_KVH, rows // CMP_STRIDE, CMP_STRIDE * NSA_HD),
                                    lambda bi, i, l, pt: (bi, 0, 0, jnp.minimum(i, n_past - 1), 0)),
                       pl.BlockSpec((None, rows, half), lambda bi, i, l, pt: (bi, i, 0))]),
        out_shape=[jax.ShapeDtypeStruct((b, 2, NSA_KVH, n_past * rows // CMP_STRIDE, CMP_STRIDE * NSA_HD), BF16),
                   jax.ShapeDtypeStruct((b, (n_past + n_tail) * rows, half), BF16)],
        compiler_params=_cparams(("parallel", "arbitrary")),
        name="nsa_gather",
    )(lidx, pt_flat, *([cache] * PAGES_PER_STEP), z3)


def _cmp_bias_kernel(pe_ref, w1_ref, o_ref):
    o_ref[...] = _dot_hi(pe_ref[...], w1_ref[...].astype(F32))


def _cmp_bias(pe8, w1):
    dd, _, flat, wide = w1.shape
    return pl.pallas_call(
        _cmp_bias_kernel, grid=(dd, 2),
        in_specs=[pl.BlockSpec((None, SUBLANES, flat), lambda d, c: (d, 0, 0)),
                  pl.BlockSpec((None, None, flat, wide), lambda d, c: (d, c, 0, 0))],
        out_specs=pl.BlockSpec((None, None, SUBLANES, wide), lambda d, c: (d, c, 0, 0)),
        out_shape=jax.ShapeDtypeStruct((dd, 2, SUBLANES, wide), F32),
        compiler_params=_cparams(("parallel", "parallel")),
        name="nsa_cmp_bias",
    )(pe8, w1)


def _cmp_kernel(l_ref, x0_ref, x1_ref, w1_ref, bias_ref, w2_ref, o_ref, carry_scr, *, nh, flat):
    @pl.when(pl.program_id(2) == 0)
    def _():
        carry_scr[...] = jnp.zeros(carry_scr.shape, F32)

    w1 = w1_ref[...]
    bias = bias_ref[...]
    last = _iota((nh, NSA_HD), 0) == nh - 1
    for g, x_ref in enumerate((x0_ref, x1_ref)):
        if flat:
            ucat = x_ref[...]
        else:
            ucat = jnp.concatenate([x_ref[pl.ds(s, nh, stride=CMP_STRIDE), :] for s in range(CMP_STRIDE)], axis=1)
        hh = _dot(ucat.astype(BF16), w1)
        hf = hh[:, :NSA_HD] + bias[0:1, :NSA_HD]
        hs = hh[:, NSA_HD:] + bias[1:2, NSA_HD:]
        hs_next = jnp.where(last, carry_scr[g], pltpu.roll(hs, nh - 1, axis=0))
        carry_scr[g] = hs[0:1, :]
        hid = _silu(hf + hs_next)
        o_ref[g] = _dot(hid.astype(BF16), w2_ref[...]).astype(BF16)


def _compress(lidx, src, col0, n_rows, p):
    b = src.shape[0]
    tr = min(n_rows, 4096)
    nt = n_rows // tr
    nh = tr // CMP_STRIDE
    flat = col0 is None
    if flat:
        xspec = lambda g: pl.BlockSpec((None, None, None, nh, CMP_STRIDE * NSA_HD),
                                       lambda bi, c, i, l: (bi, c, g, nt - 1 - i, 0))
    else:
        xspec = lambda g: pl.BlockSpec((None, tr, NSA_HD),
                                       lambda bi, c, i, l: (bi, nt - 1 - i, col0 + NSA_KVH * c + g))
    return pl.pallas_call(
        functools.partial(_cmp_kernel, nh=nh, flat=flat),
        grid_spec=pltpu.PrefetchScalarGridSpec(
            num_scalar_prefetch=1, grid=(b, 2, nt),
            in_specs=[xspec(0), xspec(1),
                      pl.BlockSpec((None, None, CMP_STRIDE * NSA_HD, 2 * NSA_HD), lambda bi, c, i, l: (l[0], c, 0, 0)),
                      pl.BlockSpec((None, None, SUBLANES, 2 * NSA_HD), lambda bi, c, i, l: (l[0], c, 0, 0)),
                      pl.BlockSpec((None, None, NSA_HD, NSA_HD), lambda bi, c, i, l: (l[0], c, 0, 0))],
            out_specs=pl.BlockSpec((None, None, NSA_KVH, nh, NSA_HD), lambda bi, c, i, l: (bi, c, 0, nt - 1 - i, 0)),
            scratch_shapes=[pltpu.VMEM((NSA_KVH, 1, NSA_HD), F32)]),
        out_shape=jax.ShapeDtypeStruct((b, 2, NSA_KVH, n_rows // CMP_STRIDE, NSA_HD), BF16),
        compiler_params=_cparams(("parallel", "parallel", "arbitrary")),
        name="nsa_compress",
    )(lidx, src, src, p["nsa_w1"], p["nsa_bias"], p["nsa_w2"])


def _masked_softmax(s, valid):
    s = jnp.where(valid, s, NEG)
    m = jnp.max(s, axis=-1, keepdims=True)
    e = jnp.where(valid, jnp.exp2(s - m), 0.0)
    den = jnp.sum(e, axis=-1, keepdims=True)
    return e / jnp.where(den > 0.0, den, 1.0)


def _attn_kernel(l_ref, q_ref, sm_ref, gn_ref, kc_ref, vc_ref, ks_ref, vs_ref, kw_ref, vw_ref, y_ref, *,
                 qb, pos0, ncp, ns, nsp, kt, wn, wpos0, tw):
    g = pl.program_id(1)
    q0 = pl.program_id(2) * qb
    qpos0 = pos0 + q0
    hd = NSA_HD
    q = q_ref[...] * (hd ** -0.5 * LOG2E)
    qr = jnp.concatenate([q[:, j * hd:(j + 1) * hd] for j in range(NSA_GQ)], axis=0).astype(BF16)
    slope_g = jnp.where(g == 0, LOG2E, LOG2E * 2.0 ** -NSA_GQ)
    slopes = [slope_g * 2.0 ** -(j + 1) for j in range(NSA_GQ)]
    heads = lambda a: [a[j * qb:(j + 1) * qb] for j in range(NSA_GQ)]

    dist = (qpos0 + _iota((qb, ncp), 0)) - (_iota((qb, ncp), 1) * CMP_STRIDE + (CMP_BLOCK - 1))
    valid = dist >= 0
    distf = dist.astype(F32)
    sc = heads(_dot_nt(qr, kc_ref[...]))
    pc = [_masked_softmax(sc[j] - slopes[j] * distf, valid) for j in range(NSA_GQ)]
    o_c = heads(_dot(jnp.concatenate(pc, axis=0).astype(BF16), vc_ref[...]))
    imp = pc[0] + pc[1] + pc[2] + pc[3]
    pool = jnp.where((_iota((ncp, nsp), 0) >> _log2(CMP_PER_SEL)) == _iota((ncp, nsp), 1), 1.0, 0.0)
    imp = _dot_hi(imp, pool)

    blk = _iota((qb, nsp), 1)
    qp = qpos0 + _iota((qb, nsp), 0)
    val = jnp.where(blk * SEL_BLOCK > qp, -FORCE, imp)
    val = jnp.where(blk == (qp >> _log2(SEL_BLOCK)), FORCE, jnp.where(blk == 0, FORCE, val))
    val = jnp.where(blk >= ns, -3.0 * FORCE, val)
    top = float(min(TOP_N, ns))
    if qb == LANES and nsp == LANES:
        nsr = -(-ns // SUBLANES) * SUBLANES
        val_t = val.T[:nsr]
        blk_t = _iota((nsr, qb), 0)
        rank = jnp.zeros((nsr, qb), F32)
        for bidx in range(ns):
            cand = val_t[bidx:bidx + 1, :]
            rank = rank + jnp.where(cand > val_t, 1.0, jnp.where(cand == val_t, jnp.where(blk_t > bidx, 1.0, 0.0), 0.0))
        sel_t = jnp.where(rank < top, 1.0, 0.0)
        selb = jnp.concatenate([sel_t, jnp.zeros((nsp - nsr, qb), F32)], axis=0).T.astype(BF16)
    else:
        rank = jnp.zeros((qb, nsp), F32)
        for bidx in range(ns):
            cand = val[:, bidx:bidx + 1]
            rank = rank + jnp.where(cand > val, 1.0, jnp.where(cand == val, jnp.where(blk > bidx, 1.0, 0.0), 0.0))
        selb = jnp.where(rank < top, 1.0, 0.0).astype(BF16)

    n_tiles = (qpos0 + qb + kt - 1) // kt
    qh = heads(qr)
    spread = jnp.where(_iota((LANES, kt), 0) == (_iota((LANES, kt), 1) >> _log2(SEL_BLOCK)), 1.0, 0.0).astype(BF16)

    def tile_bias(t):
        k0 = t * kt
        d = (qpos0 + _iota((qb, kt), 0)) - (k0 + _iota((qb, kt), 1))
        pick_blk = jnp.where(_iota((nsp, LANES), 0) == (k0 >> _log2(SEL_BLOCK)) + _iota((nsp, LANES), 1),
                             1.0, 0.0).astype(BF16)
        sel_tile = _dot(selb, pick_blk).astype(BF16)
        ok = jnp.where(d >= 0, _dot(sel_tile, spread), 0.0) > 0.5
        return jnp.where(ok, d.astype(F32), -NEG)

    def tile(t, carry):
        k0 = pl.multiple_of(t * kt, kt)
        kk = ks_ref[pl.ds(k0, kt), :].astype(BF16)
        vv = vs_ref[pl.ds(k0, kt), :].astype(BF16)
        per_head = qb == Q_BLOCK
        s = [_dot_nt(qh[j], kk) for j in range(NSA_GQ)] if per_head else heads(_dot_nt(qr, kk))
        bias = carry[NSA_GQ]
        bias_next = tile_bias(t + 1)
        stats, prs = [], []
        for j in range(NSA_GQ):
            m, lsum, _ = carry[j]
            sm = s[j] - slopes[j] * bias
            m_new = jnp.maximum(m, jnp.max(sm, axis=-1, keepdims=True))
            pr = jnp.exp2(sm - m_new)
            alpha = jnp.exp2(m - m_new)
            stats.append((m_new, alpha * lsum + jnp.sum(pr, axis=-1, keepdims=True), alpha))
            prs.append(pr.astype(BF16))
        pv = [_dot(pr, vv) for pr in prs] if per_head else heads(_dot(jnp.concatenate(prs, axis=0), vv))
        new = [(stats[j][0], stats[j][1], stats[j][2] * carry[j][2] + pv[j]) for j in range(NSA_GQ)]
        return tuple(new) + (bias_next,)

    init = (jnp.full((qb, 1), NEG, F32), jnp.zeros((qb, 1), F32), jnp.zeros((qb, hd), F32))
    fin = lax.fori_loop(0, n_tiles, tile, (init,) * NSA_GQ + (tile_bias(0),))
    o_s = [acc / lsum for _, lsum, acc in fin[:NSA_GQ]]

    k0w = pl.multiple_of(jnp.clip(q0 - WINDOW, 0, tw - wn), SUBLANES)
    dw = (qpos0 + _iota((qb, wn), 0)) - (wpos0 + k0w + _iota((qb, wn), 1))
    okw = jnp.abs(2 * dw - (WINDOW - 1)) < WINDOW
    dwf = dw.astype(F32)
    sw = heads(_dot_nt(qr, kw_ref[pl.ds(k0w, wn), :].astype(BF16)))
    pw = [_masked_softmax(sw[j] - slopes[j] * dwf, okw) for j in range(NSA_GQ)]
    o_w = heads(_dot(jnp.concatenate(pw, axis=0).astype(BF16), vw_ref[pl.ds(k0w, wn), :].astype(BF16)))

    gate = _sigmoid(sm_ref[...])
    lane = _iota((qb, LANES), 1)
    pick = lambda idx: jnp.sum(jnp.where(lane == idx, gate, 0.0), axis=-1, keepdims=True)
    for j in range(NSA_GQ):
        base = SM_GL + 3 * (g * NSA_GQ + j)
        o = pick(base) * o_c[j] + pick(base + 1) * o_s[j] + pick(base + 2) * o_w[j]
        y_ref[:, j * hd:(j + 1) * hd] = (o * _silu(gn_ref[:, j * hd:(j + 1) * hd])).astype(y_ref.dtype)


def _attention(lidx, z3, cmp_kv, ks_src, ks_col, vs_col, kw_src, kw_col, vw_col, pos0, wpos0, ydtype):
    b, t, _ = z3.shape
    qb = min(Q_BLOCK, t)
    ncp = cmp_kv.shape[3]
    tk = ks_src.shape[1]
    tw = kw_src.shape[1]
    if qb == Q_BLOCK:
        kt = KEY_TILE
    else:
        kt = _key_tile(tk)
    ns = -(-(pos0 + t) // SEL_BLOCK)
    nsp = -(-ns // LANES) * LANES
    wn = WINDOW + Q_BLOCK
    assert tk % kt == 0 and tk >= pos0 + t and tw >= wn and ncp // CMP_PER_SEL <= nsp
    hw = NSA_GQ * NSA_HD
    kern = functools.partial(_attn_kernel, qb=qb, pos0=pos0, ncp=ncp, ns=ns, nsp=nsp, kt=kt, wn=wn, wpos0=wpos0, tw=tw)
    kvspec = lambda rows, col: pl.BlockSpec((None, rows, NSA_HD), lambda bi, g, i, l: (bi, 0, col + g))
    cspec = lambda c: pl.BlockSpec((None, None, None, ncp, NSA_HD), lambda bi, g, i, l: (bi, c, g, 0, 0))
    return pl.pallas_call(
        kern,
        grid_spec=pltpu.PrefetchScalarGridSpec(
            num_scalar_prefetch=1, grid=(b, NSA_KVH, t // qb),
            in_specs=[pl.BlockSpec((None, qb, hw), lambda bi, g, i, l: (bi, i, C_QNSA // hw + g)),
                      pl.BlockSpec((None, qb, LANES), lambda bi, g, i, l: (bi, i, C_SMALL_A // LANES)),
                      pl.BlockSpec((None, qb, hw), lambda bi, g, i, l: (bi, i, C_GNSA // hw + g)),
                      cspec(0), cspec(1),
                      kvspec(tk, ks_col), kvspec(tk, vs_col), kvspec(tw, kw_col), kvspec(tw, vw_col)],
            out_specs=pl.BlockSpec((None, qb, hw), lambda bi, g, i, l: (bi, i, g))),
        out_shape=jax.ShapeDtypeStruct((b, t, W_BR), ydtype),
        compiler_params=_cparams(("parallel", "parallel", "arbitrary")),
        name="nsa_attention",
    )(lidx, z3, z3, z3, cmp_kv, cmp_kv, ks_src, ks_src, kw_src, kw_src)


def _merge_kernel(l_ref, y0_ref, y1_ref, y2_ref, y3_ref, g0_ref, g1_ref, g2_ref, g3_ref, w_ref, o_ref):
    acc = None
    for m, (y_ref, g_ref) in enumerate(((y0_ref, g0_ref), (y1_ref, g1_ref), (y2_ref, g2_ref), (y3_ref, g3_ref))):
        term = _sigmoid(g_ref[...]) * _dot(y_ref[...].astype(BF16), w_ref[m])
        acc = term if acc is None else acc + term
    o_ref[...] = acc.astype(o_ref.dtype)


def _merge(lidx, ys, z2d, p):
    n = z2d.shape[0]
    tm = min(n, 256)
    yspec = pl.BlockSpec((tm, W_BR), lambda i, l: (i, 0))
    gspec = lambda m: pl.BlockSpec((tm, D_MODEL), lambda i, l: (i, C_MERGE // D_MODEL + m))
    wspec = pl.BlockSpec((None, N_BRANCH, W_BR, D_MODEL), lambda i, l: (l[0], 0, 0, 0), pipeline_mode=pl.Buffered(1))
    return pl.pallas_call(
        _merge_kernel,
        grid_spec=pltpu.PrefetchScalarGridSpec(
            num_scalar_prefetch=1, grid=(n // tm,),
            in_specs=[yspec] * N_BRANCH + [gspec(m) for m in range(N_BRANCH)] + [wspec],
            out_specs=pl.BlockSpec((tm, D_MODEL), lambda i, l: (i, 0))),
        out_shape=jax.ShapeDtypeStruct((n, D_MODEL), BF16),
        compiler_params=_cparams(("parallel",)),
        name="merge",
    )(lidx, *ys, z2d, z2d, z2d, z2d, p["w_branch"])


def _outproj_kernel(l_ref, m_ref, x_ref, w_ref, g_ref, b_ref, o_ref, *, alpha):
    v = alpha * x_ref[...] + _dot(m_ref[...], w_ref[...])
    mu = jnp.mean(v, axis=-1, keepdims=True)
    c = v - mu
    var = jnp.mean(c * c, axis=-1, keepdims=True)
    o_ref[...] = c * lax.rsqrt(var + 1e-5) * g_ref[...] + b_ref[...]


def _outproj(lidx, merged, x2d, p, alpha):
    n = x2d.shape[0]
    tm = min(n, 256)
    return pl.pallas_call(
        functools.partial(_outproj_kernel, alpha=alpha),
        grid_spec=pltpu.PrefetchScalarGridSpec(
            num_scalar_prefetch=1, grid=(n // tm,),
            in_specs=[pl.BlockSpec((tm, D_MODEL), lambda i, l: (i, 0)),
                      pl.BlockSpec((tm, D_MODEL), lambda i, l: (i, 0)),
                      pl.BlockSpec((None, D_MODEL, D_MODEL), lambda i, l: (l[0], 0, 0)),
                      pl.BlockSpec((None, 1, D_MODEL), lambda i, l: (l[0], 0, 0)),
                      pl.BlockSpec((None, 1, D_MODEL), lambda i, l: (l[0], 0, 0))],
            out_specs=pl.BlockSpec((tm, D_MODEL), lambda i, l: (i, 0))),
        out_shape=jax.ShapeDtypeStruct((n, D_MODEL), F32),
        compiler_params=_cparams(("parallel",)),
        input_output_aliases={2: 0},
        name="outproj_ln",
    )(lidx, merged, x2d, p["w_out"], p["ln_g"], p["ln_b"])


def _prep_params(w):
    p = {}
    row = lambda a: a[:, None, :]
    p["lru_conv_w"] = w["lru_conv_w"]
    p["lru_conv_b"] = row(w["lru_conv_b"])
    p["lru_wa"] = w["lru_wa"].astype(BF16)
    p["lru_ba"] = row(w["lru_ba"])
    p["lru_wx"] = w["lru_wx"].astype(BF16)
    p["lru_bx"] = row(w["lru_bx"])
    p["lru_lambda"] = row(w["lru_lambda"])
    p.update(_s5_weights(w))
    p["w_branch"] = w["w_branch"].astype(BF16)
    p["w_out"] = w["w_out"].astype(BF16)
    p["ln_g"] = row(w["ln_g"])
    p["ln_b"] = row(w["ln_b"])
    dd = w["gdn_a_log"].shape[0]
    lane_row = lambda a, off: jnp.zeros((dd, 1, LANES), F32).at[:, 0, off:off + a.shape[-1]].set(a)
    p["gdn_conv_w"] = w["gdn_conv_w"]
    p["gdn_a_log"] = lane_row(w["gdn_a_log"], SM_A)
    p["gdn_dt_bias"] = lane_row(w["gdn_dt_bias"], SM_A)
    p["gdn_norm_w"] = row(w["gdn_norm_w"])
    flat = CMP_STRIDE * NSA_HD
    w1 = w["nsa_cmp_w1"].reshape(dd, 2, 2, flat, NSA_HD).transpose(0, 1, 3, 2, 4)
    p["nsa_w1"] = w1.reshape(dd, 2, flat, 2 * NSA_HD).astype(BF16)
    pe = w["nsa_cmp_pos"].reshape(dd, 2, flat)
    p["nsa_bias"] = _cmp_bias(jnp.concatenate([pe, jnp.zeros((dd, SUBLANES - 2, flat), F32)], axis=1), p["nsa_w1"])
    p["nsa_w2"] = w["nsa_cmp_w2"].astype(BF16)
    return p


def _mixer_layer(lidx, x2d, s, t, state, nsa_branch, p, win, alpha, ydtype):
    lru_buf, lru_h, gdn_buf, gdn_s, s5_re, s5_im = state
    z = _inproj(lidx, win[0], x2d, win[1])
    z3 = z.reshape(s, t, NP)
    y_lru, lru_buf, lru_h = _lru(lidx, z3, lru_buf, lru_h, p, ydtype)
    y_nsa = nsa_branch(z3)
    y_gdn, gdn_buf, gdn_s = _gdn(lidx, z3, gdn_buf, gdn_s, p, ydtype, GDN_CHUNK)
    y_s5, s5_re, s5_im = _s5_scan(lidx, z3, s5_re, s5_im, p, s5_re.shape[1])
    y_s5 = _s5_glu(lidx, y_s5, z, p, ydtype)
    flat = lambda y: y.reshape(s * t, W_BR)
    merged = _merge(lidx, (flat(y_lru), flat(y_nsa), flat(y_gdn), y_s5), z, p)
    x_new = _outproj(lidx, merged, x2d, p, alpha)
    return x_new, z3, (lru_buf, lru_h, gdn_buf, gdn_s, s5_re, s5_im)


def kernel(x_prompt, x_sample, state_lru_h, state_lru_conv, cache_nsa_kv, cache_win_kv, state_gdn_s, state_gdn_conv, state_s5_re, state_s5_im, page_table, w_in, lru_conv_w, lru_conv_b, lru_wa, lru_ba, lru_wx, lru_bx, lru_lambda, nsa_cmp_pos, nsa_cmp_w1, nsa_cmp_w2, gdn_conv_w, gdn_a_log, gdn_dt_bias, gdn_norm_w, s5_lam_re, s5_lam_im, s5_log_dt, s5_b_re, s5_b_im, s5_c_re, s5_c_im, s5_d, s5_glu_w, w_branch, w_out, ln_g, ln_b):
    depth = w_in.shape[0]
    bp, tp, _ = x_prompt.shape
    db, ts, _ = x_sample.shape
    n_pages = page_table.shape[1]
    past = n_pages * PAGE_SIZE
    wbuf = cache_win_kv.shape[2]
    alpha = (2.0 * depth) ** 0.25
    kvw = 4 * NSA_KVH * NSA_HD
    winw = 2 * NSA_KVH * NSA_HD
    p = _prep_params(dict(
        lru_conv_w=lru_conv_w, lru_conv_b=lru_conv_b, lru_wa=lru_wa, lru_ba=lru_ba, lru_wx=lru_wx, lru_bx=lru_bx,
        lru_lambda=lru_lambda, nsa_cmp_pos=nsa_cmp_pos, nsa_cmp_w1=nsa_cmp_w1, nsa_cmp_w2=nsa_cmp_w2,
        gdn_conv_w=gdn_conv_w, gdn_a_log=gdn_a_log, gdn_dt_bias=gdn_dt_bias, gdn_norm_w=gdn_norm_w,
        s5_lam_re=s5_lam_re, s5_lam_im=s5_lam_im, s5_log_dt=s5_log_dt, s5_b_re=s5_b_re, s5_b_im=s5_b_im,
        s5_c_re=s5_c_re, s5_c_im=s5_c_im, s5_d=s5_d, s5_glu_w=s5_glu_w, w_branch=w_branch, w_out=w_out,
        ln_g=ln_g, ln_b=ln_b))
    assert w_in.shape[2] == sum(s[1] for s in IN_SEGMENTS)
    wp = (jnp.asarray(_tile_sources(), jnp.int32), jnp.swapaxes(w_in, 1, 2))
    p = lax.optimization_barrier(p)
    pt_flat = page_table.reshape(-1).astype(jnp.int32)
    sb = db if ts == S5_L else 1
    c0 = C_KV // LANES
    zero_state = (jnp.zeros((bp, CONV_K - 1, W_BR), F32), jnp.zeros((bp, 1, W_BR), F32),
                  jnp.zeros((bp, CONV_K - 1, 3 * W_BR), F32), jnp.zeros((bp, GDN_H, GDN_HD, GDN_HD), F32),
                  jnp.zeros((bp, 1, S5_G * S5_P), F32), jnp.zeros((bp, 1, S5_G * S5_P), F32))

    def layer(carry, l):
        xp, xs = carry
        lidx = l.reshape(1)
        at = lambda a: lax.dynamic_index_in_dim(a, l, 0, keepdims=False)

        def nsa_prompt(z3):
            ckv = _compress(lidx, z3, c0, tp, p)
            return _attention(lidx, z3, ckv, z3, c0 + 4, c0 + 6, z3, c0 + 8, c0 + 10, 0, 0, BF16)

        xp, zp3, st_p = _mixer_layer(lidx, xp, bp, tp, zero_state, nsa_prompt, p, wp, alpha, BF16)

        win_state = {}

        def nsa_sample(z3):
            cmp_rows, sel_rows = _gather_ctx(lidx, pt_flat, cache_nsa_kv, z3)
            ckv = _compress(lidx, cmp_rows, None, past, p)
            win = jnp.concatenate([at(cache_win_kv).reshape(db, wbuf, winw),
                                   z3[:, :, C_KV + kvw:C_KV + kvw + winw]], axis=1)
            win_state["win"] = win
            pad = max(0, WINDOW + Q_BLOCK - (wbuf + ts))
            win_pad = jnp.concatenate([win, jnp.zeros((db, pad, winw), F32)], axis=1)
            return _attention(lidx, z3, ckv, sel_rows, 0, 2, win_pad, 0, 2, past, past - wbuf, F32)

        st_in = (at(state_lru_conv), at(state_lru_h)[:, None, :], at(state_gdn_conv), at(state_gdn_s),
                 at(state_s5_re).reshape(db // sb, sb, S5_G * S5_P), at(state_s5_im).reshape(db // sb, sb, S5_G * S5_P))
        xs, zs3, st_s = _mixer_layer(lidx, xs, db, ts, st_in, nsa_sample, p, wp, alpha, F32)

        def outs(st, z3, s, t, win):
            lru_buf, lru_h, gdn_buf, gdn_s, s5_re, s5_im = st
            return (lru_h.reshape(s, W_BR), lru_buf,
                    z3[:, :, C_KV:C_KV + kvw].reshape(s, t, 4, NSA_KVH, NSA_HD),
                    win.reshape(s, win.shape[1], 2, NSA_KVH, NSA_HD),
                    gdn_s, gdn_buf, s5_re.reshape(s, S5_G, S5_P), s5_im.reshape(s, S5_G, S5_P))

        win_p = zp3[:, tp - min(WINDOW, tp):, C_KV + kvw:C_KV + kvw + winw]
        win_s = win_state["win"][:, wbuf + ts - min(WINDOW, past + ts):]
        return (xp, xs), (outs(st_p, zp3, bp, tp, win_p), outs(st_s, zs3, db, ts, win_s))

    (xp, xs), (op, os_) = lax.scan(layer, (x_prompt.reshape(bp * tp, D_MODEL), x_sample.reshape(db * ts, D_MODEL)),
                                   jnp.arange(depth, dtype=jnp.int32))
    return (xp.reshape(bp, tp, D_MODEL), xs.reshape(db, ts, D_MODEL)) + tuple(op) + tuple(os_)
```

```python
import functools

import jax
import jax.numpy as jnp
from jax import lax
from jax.experimental import pallas as pl
from jax.experimental.pallas import tpu as pltpu

F32 = jnp.float32
BF16 = jnp.bfloat16
HI = lax.Precision.HIGHEST

D_MODEL = 2048
W_BR = D_MODEL // 2
N_BRANCH = 4
CONV_K = 4
LRU_BLOCKS = 8
LRU_BS = W_BR // LRU_BLOCKS
LRU_C = 8.0
NSA_HD = 128
NSA_H = 8
NSA_KVH = 2
NSA_GQ = NSA_H // NSA_KVH
CMP_STRIDE = 16
CMP_BLOCK = 32
SEL_BLOCK = 64
CMP_PER_SEL = SEL_BLOCK // CMP_STRIDE
TOP_N = 16
WINDOW = 512
Q_BLOCK = 128
PAGE_SIZE = 128
GDN_HD = 128
GDN_H = 8
GDN_CHUNK = 64
GDN_SPLIT_LEVELS = 2
S5_GS = 16
S5_G = W_BR // S5_GS
S5_P = 64
S5_L = 8
NEG = -1e30
FORCE = 1e9
LOG2E = 1.4426950408889634

LANES = 128
SUBLANES = 8
VMEM_LIMIT = 56 * 1024 * 1024

IN_TILE = 512
C_MERGE = 0
C_ULRU = 8192
C_GLRU = 9216
C_QNSA = 10240
C_GNSA = 11264
C_GGDN = 12288
C_US5 = 13312
C_GS5 = 14336
C_QKV = 15360
C_KV = 18432
C_SMALL_A = 19968
C_SMALL_B = 20480
NP = C_SMALL_B + IN_TILE
IN_SEGMENTS = (("u_lru", W_BR, C_ULRU), ("g_lru", W_BR, C_GLRU), ("q_nsa", W_BR, C_QNSA),
               ("kv", 6 * NSA_KVH * NSA_HD, C_KV), ("gl", 3 * NSA_H, None), ("g_nsa", W_BR, C_GNSA),
               ("qkv", 3 * W_BR, C_QKV), ("a", GDN_H, None), ("b", GDN_H, None),
               ("g_gdn", W_BR, C_GGDN), ("u_s5", W_BR, C_US5), ("g_s5", W_BR, C_GS5),
               ("merge", N_BRANCH * D_MODEL, C_MERGE))


def _segment_start(name):
    return sum(w for n, w, _ in IN_SEGMENTS[:[s[0] for s in IN_SEGMENTS].index(name)])


SRC_SMALL_A = _segment_start("gl") // LANES * LANES
SRC_SMALL_B = _segment_start("a") // LANES * LANES
SM_GL = _segment_start("gl") - SRC_SMALL_A
SM_A = _segment_start("a") - SRC_SMALL_B
SM_B = _segment_start("b") - SRC_SMALL_B


def _tile_sources():
    src = [None] * (NP // IN_TILE)
    at = 0
    for _, width, dst in IN_SEGMENTS:
        if dst is not None:
            for c in range(0, width, IN_TILE):
                src[(dst + c) // IN_TILE] = at + c
        at += width
    src[C_SMALL_A // IN_TILE] = SRC_SMALL_A
    src[C_SMALL_B // IN_TILE] = SRC_SMALL_B
    assert all(s is not None and s % SUBLANES == 0 and s + IN_TILE <= at for s in src)
    return src


def _cparams(sem):
    return pltpu.CompilerParams(dimension_semantics=sem, vmem_limit_bytes=VMEM_LIMIT)


def _sigmoid(x):
    return 0.5 + 0.5 * jnp.tanh(0.5 * x)


def _silu(x):
    return x * _sigmoid(x)


def _softplus(x):
    return jnp.maximum(x, 0.0) + jnp.log1p(jnp.exp(-jnp.abs(x)))


def _dot(a, b):
    return jnp.dot(a, b, preferred_element_type=F32)


def _dot_hi(a, b):
    return jnp.dot(a, b, preferred_element_type=F32, precision=HI)


def _dot_nt(a, b):
    return lax.dot_general(a, b, (((1,), (1,)), ((), ())), preferred_element_type=F32)


def _dot_tn(a, b):
    return lax.dot_general(a, b, (((0,), (0,)), ((), ())), preferred_element_type=F32)


def _iota(shape, axis):
    return lax.broadcasted_iota(jnp.int32, shape, axis)


def _log2(n):
    assert n & (n - 1) == 0
    return n.bit_length() - 1


def _inproj_kernel(l_ref, src_ref, x_ref, w_ref, o_ref, xb_ref):
    @pl.when(pl.program_id(1) == 0)
    def _():
        xb_ref[...] = x_ref[...].astype(BF16)

    o_ref[...] = _dot_nt(xb_ref[...], w_ref[...].astype(BF16))


def _inproj(lidx, srcs, x2d, wt):
    n = x2d.shape[0]
    tm = min(n, 2048)
    wspec = pl.BlockSpec((pl.Squeezed(), pl.Element(IN_TILE), pl.Element(D_MODEL)),
                         lambda i, j, l, src: (l[0], pl.multiple_of(src[j], SUBLANES), 0))
    xspec = pl.BlockSpec((tm, D_MODEL), lambda i, j, l, src: (i, 0), pipeline_mode=pl.Buffered(1))
    return pl.pallas_call(
        _inproj_kernel,
        grid_spec=pltpu.PrefetchScalarGridSpec(
            num_scalar_prefetch=2, grid=(n // tm, NP // IN_TILE),
            in_specs=[xspec, wspec],
            out_specs=pl.BlockSpec((tm, IN_TILE), lambda i, j, l, src: (i, j)),
            scratch_shapes=[pltpu.VMEM((tm, D_MODEL), BF16)]),
        out_shape=jax.ShapeDtypeStruct((n, NP), F32),
        compiler_params=_cparams(("parallel", "arbitrary")),
        name="inproj",
    )(lidx, srcs, x2d, wt)


def _lru_kernel(l_ref, u_ref, g_ref, buf_ref, h0_ref, cw_ref, cb_ref, wa_ref, ba_ref, wx_ref, bx_ref, lam_ref,
                y_ref, bufo_ref, ho_ref, xp_scr, a_scr, b_scr, h_scr, *, tt, nt):
    ti = pl.program_id(1)

    @pl.when(ti == 0)
    def _():
        xp_scr[5:8, :] = buf_ref[...]
        h_scr[...] = h0_ref[...]

    u = u_ref[...]
    xp_scr[8:8 + tt, :] = u
    cw = cw_ref[...]
    xc = (cb_ref[...] + cw[3:4] * u + cw[2:3] * xp_scr[7:7 + tt, :]
          + cw[1:2] * xp_scr[6:6 + tt, :] + cw[0:1] * xp_scr[5:5 + tt, :])
    tail = u[tt - 3:tt, :]
    xp_scr[5:8, :] = tail
    sp = _softplus(-lam_ref[...])
    for n in range(LRU_BLOCKS):
        sl = slice(n * LRU_BS, (n + 1) * LRU_BS)
        xn = xc[:, sl]
        xb = xn.astype(BF16)
        r = _sigmoid(_dot(xb, wa_ref[n]) + ba_ref[:, sl])
        i = _sigmoid(_dot(xb, wx_ref[n]) + bx_ref[:, sl])
        a = jnp.exp(-LRU_C * r * sp[:, sl])
        a_scr[:, sl] = a
        b_scr[:, sl] = jnp.sqrt(1.0 - a * a) * (i * xn)

    def body(i, h):
        for k in range(SUBLANES):
            t = i * SUBLANES + k
            h = a_scr[pl.ds(t, 1), :] * h + b_scr[pl.ds(t, 1), :]
            b_scr[pl.ds(t, 1), :] = h
        return h

    h = lax.fori_loop(0, tt // SUBLANES, body, h_scr[...])
    h_scr[...] = h
    y_ref[...] = (b_scr[...] * _silu(g_ref[...])).astype(y_ref.dtype)

    @pl.when(ti == nt - 1)
    def _():
        bufo_ref[...] = tail
        ho_ref[...] = h


def _lru(lidx, z3, buf, h0, p, ydtype):
    s, t, _ = z3.shape
    tt = min(t, 512)
    nt = t // tt
    wspec = lambda shape: pl.BlockSpec((None,) + shape, lambda b, i, l: (l[0],) + (0,) * len(shape))
    return pl.pallas_call(
        functools.partial(_lru_kernel, tt=tt, nt=nt),
        grid_spec=pltpu.PrefetchScalarGridSpec(
            num_scalar_prefetch=1, grid=(s, nt),
            in_specs=[pl.BlockSpec((None, tt, W_BR), lambda b, i, l: (b, i, C_ULRU // W_BR)),
                      pl.BlockSpec((None, tt, W_BR), lambda b, i, l: (b, i, C_GLRU // W_BR)),
                      pl.BlockSpec((None, 3, W_BR), lambda b, i, l: (b, 0, 0)),
                      pl.BlockSpec((None, 1, W_BR), lambda b, i, l: (b, 0, 0)),
                      wspec((CONV_K, W_BR)), wspec((1, W_BR)),
                      wspec((LRU_BLOCKS, LRU_BS, LRU_BS)), wspec((1, W_BR)),
                      wspec((LRU_BLOCKS, LRU_BS, LRU_BS)), wspec((1, W_BR)), wspec((1, W_BR))],
            out_specs=[pl.BlockSpec((None, tt, W_BR), lambda b, i, l: (b, i, 0)),
                       pl.BlockSpec((None, 3, W_BR), lambda b, i, l: (b, 0, 0)),
                       pl.BlockSpec((None, 1, W_BR), lambda b, i, l: (b, 0, 0))],
            scratch_shapes=[pltpu.VMEM((tt + 8, W_BR), F32), pltpu.VMEM((tt, W_BR), F32),
                            pltpu.VMEM((tt, W_BR), F32), pltpu.VMEM((1, W_BR), F32)]),
        out_shape=[jax.ShapeDtypeStruct((s, t, W_BR), ydtype),
                   jax.ShapeDtypeStruct((s, 3, W_BR), F32),
                   jax.ShapeDtypeStruct((s, 1, W_BR), F32)],
        compiler_params=_cparams(("parallel", "arbitrary")),
        name="rglru",
    )(lidx, z3, z3, buf, h0, p["lru_conv_w"], p["lru_conv_b"], p["lru_wa"], p["lru_ba"],
      p["lru_wx"], p["lru_bx"], p["lru_lambda"])


S5_CB = LANES // S5_GS
S5_SW = S5_CB * S5_P


def _gelu_tanh(x):
    return 0.5 * x * (1.0 + jnp.tanh(0.7978845608028654 * (x + 0.044715 * (x * x * x))))


def _s5_kernel(l_ref, u_ref, h0re_ref, h0im_ref, wst_ref, vout_ref, kt_ref, alre_ref, alim_ref, d_ref,
               y_ref, hre_ref, him_ref, hin_scr, s_scr, *, n, sb):
    rows = sb * n
    us = [u_ref[pl.ds(j, rows, stride=S5_L), :] for j in range(S5_L)]
    ub = jnp.concatenate(us, axis=1).astype(BF16)
    s = _dot(ub, wst_ref[...])
    alre = alre_ref[...]
    alim = alim_ref[...]
    h0re = h0re_ref[...]
    h0im = h0im_ref[...]
    if n == 1:
        hin_scr[:, :S5_SW] = h0re
        hin_scr[:, S5_SW:] = h0im
        hre = alre * h0re - alim * h0im + s[:, :S5_SW]
        him = alre * h0im + alim * h0re + s[:, S5_SW:]
    else:
        s_scr[...] = s

        def body(c, carry):
            hre, him = carry
            hin_scr[pl.ds(c, 1), :S5_SW] = hre
            hin_scr[pl.ds(c, 1), S5_SW:] = him
            srow = s_scr[pl.ds(c, 1), :]
            return (alre * hre - alim * him + srow[:, :S5_SW],
                    alre * him + alim * hre + srow[:, S5_SW:])

        hre, him = lax.fori_loop(0, n, body, (h0re, h0im))
    hre_ref[...] = hre
    him_ref[...] = him
    ycat = _dot(hin_scr[...].astype(BF16), vout_ref[...]) + _dot(ub, kt_ref[...])
    d = d_ref[...]
    for j in range(S5_L):
        yj = ycat[:, j * LANES:(j + 1) * LANES] + d * us[j]
        y_ref[pl.ds(j, rows, stride=S5_L), :] = _gelu_tanh(yj)


def _s5_scan(lidx, z3, h0re, h0im, p, sb):
    s, t, _ = z3.shape
    n = t // S5_L
    assert sb == 1 or n == 1
    sg = s // sb
    zr = z3.reshape(sg, sb * t, NP)
    ncb = W_BR // LANES
    wspec = lambda shape: pl.BlockSpec((None, None) + shape, lambda b, c, l: (l[0], c) + (0,) * len(shape))
    hspec = pl.BlockSpec((None, sb, S5_SW), lambda b, c, l: (b, 0, c))
    y, hre, him = pl.pallas_call(
        functools.partial(_s5_kernel, n=n, sb=sb),
        grid_spec=pltpu.PrefetchScalarGridSpec(
            num_scalar_prefetch=1, grid=(sg, ncb),
            in_specs=[pl.BlockSpec((None, sb * t, LANES), lambda b, c, l: (b, 0, C_US5 // LANES + c)),
                      hspec, hspec,
                      wspec((S5_L * LANES, 2 * S5_SW)), wspec((2 * S5_SW, S5_L * LANES)),
                      wspec((S5_L * LANES, S5_L * LANES)), wspec((1, S5_SW)), wspec((1, S5_SW)),
                      wspec((1, LANES))],
            out_specs=[pl.BlockSpec((None, sb * t, LANES), lambda b, c, l: (b, 0, c)), hspec, hspec],
            scratch_shapes=[pltpu.VMEM((sb * n, 2 * S5_SW), F32), pltpu.VMEM((sb * n, 2 * S5_SW), F32)]),
        out_shape=[jax.ShapeDtypeStruct((sg, sb * t, W_BR), F32),
                   jax.ShapeDtypeStruct((sg, sb, S5_G * S5_P), F32),
                   jax.ShapeDtypeStruct((sg, sb, S5_G * S5_P), F32)],
        compiler_params=_cparams(("parallel", "arbitrary")),
        name="s5_scan",
    )(lidx, zr, h0re, h0im, p["s5_wst"], p["s5_vout"], p["s5_kt"], p["s5_alre"], p["s5_alim"], p["s5_d"])
    return y.reshape(s * t, W_BR), hre, him


def _s5_glu_kernel(l_ref, y_ref, g_ref, w_ref, o_ref):
    gl = _dot(y_ref[...].astype(BF16), w_ref[...])
    o_ref[...] = (gl[:, :W_BR] * _sigmoid(gl[:, W_BR:]) * _silu(g_ref[...])).astype(o_ref.dtype)


def _s5_glu(lidx, y2d, z2d, p, ydtype):
    n = y2d.shape[0]
    tm = min(n, 512)
    return pl.pallas_call(
        _s5_glu_kernel,
        grid_spec=pltpu.PrefetchScalarGridSpec(
            num_scalar_prefetch=1, grid=(n // tm,),
            in_specs=[pl.BlockSpec((tm, W_BR), lambda i, l: (i, 0)),
                      pl.BlockSpec((tm, W_BR), lambda i, l: (i, C_GS5 // W_BR)),
                      pl.BlockSpec((None, W_BR, 2 * W_BR), lambda i, l: (l[0], 0, 0))],
            out_specs=pl.BlockSpec((tm, W_BR), lambda i, l: (i, 0))),
        out_shape=jax.ShapeDtypeStruct((n, W_BR), ydtype),
        compiler_params=_cparams(("parallel",)),
        name="s5_glu",
    )(lidx, y2d, z2d, p["s5_glu_w"])


def _s5_expand_kernel(wst_ref, vout_ref, kt_ref, wst_o, vout_o, kt_o):
    n = S5_L * LANES
    lp = S5_P.bit_length() - 1
    lc = S5_GS.bit_length() - 1
    lg = S5_CB.bit_length() - 1
    row = _iota((n, n), 0)
    col = _iota((n, n), 1)
    src = _iota((LANES, n), 0)
    dst = _iota((LANES, n), 1)
    rep_state = jnp.where(src == (((dst >> (lp + lg)) << lp) | (dst & (S5_P - 1))), 1.0, 0.0).astype(BF16)
    rep_out = jnp.where(src == (((dst >> (lc + lg)) << lc) | (dst & (S5_GS - 1))), 1.0, 0.0).astype(BF16)
    gi_in_row = (row >> lc) & (S5_CB - 1)
    gi_state_row = (row >> lp) & (S5_CB - 1)
    gi_state_col = (col >> lp) & (S5_CB - 1)
    gi_out_col = (col >> lc) & (S5_CB - 1)
    wst_o[...] = jnp.where(gi_in_row == gi_state_col, _dot(wst_ref[...].astype(BF16), rep_state), 0.0).astype(BF16)
    vout_o[...] = jnp.where(gi_state_row == gi_out_col, _dot(vout_ref[...].astype(BF16), rep_out), 0.0).astype(BF16)
    kt_o[...] = jnp.where(gi_in_row == gi_out_col, _dot(kt_ref[...].astype(BF16), rep_out), 0.0).astype(BF16)


def _s5_expand(wst, vout, kt):
    dd, ncb, n, _ = wst.shape
    cspec = pl.BlockSpec((None, None, n, LANES), lambda d, c: (d, c, 0, 0))
    ospec = pl.BlockSpec((None, None, n, n), lambda d, c: (d, c, 0, 0))
    oshape = jax.ShapeDtypeStruct((dd, ncb, n, n), BF16)
    return pl.pallas_call(
        _s5_expand_kernel, grid=(dd, ncb), in_specs=[cspec, cspec, cspec], out_specs=[ospec, ospec, ospec],
        out_shape=[oshape, oshape, oshape], compiler_params=_cparams(("parallel", "parallel")),
        name="s5_expand",
    )(wst, vout, kt)


def _s5_weights(w):
    dt = jnp.exp(w["s5_log_dt"])[..., None]
    lr, li = w["s5_lam_re"], w["s5_lam_im"]
    mag = jnp.exp(lr * dt)
    a_re = mag * jnp.cos(li * dt)
    a_im = mag * jnp.sin(li * dt)
    den = lr * lr + li * li
    f_re = ((a_re - 1.0) * lr + a_im * li) / den
    f_im = (a_im * lr - (a_re - 1.0) * li) / den
    bb_re = f_re[..., None] * w["s5_b_re"] - f_im[..., None] * w["s5_b_im"]
    bb_im = f_re[..., None] * w["s5_b_im"] + f_im[..., None] * w["s5_b_re"]
    pw_re = [jnp.ones_like(a_re)]
    pw_im = [jnp.zeros_like(a_im)]
    for _ in range(S5_L):
        pr, pi = pw_re[-1], pw_im[-1]
        pw_re.append(pr * a_re - pi * a_im)
        pw_im.append(pr * a_im + pi * a_re)
    pw_re = jnp.stack(pw_re, axis=1)
    pw_im = jnp.stack(pw_im, axis=1)
    dd = lr.shape[0]
    ncb = S5_G // S5_CB
    grp = lambda a: a.reshape(a.shape[0], a.shape[1], ncb, S5_CB, *a.shape[3:])
    rev_re = jnp.stack([pw_re[:, S5_L - 1 - j] for j in range(S5_L)], axis=1)
    rev_im = jnp.stack([pw_im[:, S5_L - 1 - j] for j in range(S5_L)], axis=1)
    st_re = rev_re[..., None] * bb_re[:, None] - rev_im[..., None] * bb_im[:, None]
    st_im = rev_re[..., None] * bb_im[:, None] + rev_im[..., None] * bb_re[:, None]

    def state_w(a):
        return grp(a).transpose(0, 2, 1, 3, 5, 4).reshape(dd, ncb, S5_L * LANES, S5_P)

    wst = jnp.concatenate([state_w(st_re), state_w(st_im)], axis=-1)
    c_re, c_im = w["s5_c_re"], w["s5_c_im"]
    nx_re, nx_im = pw_re[:, 1:], pw_im[:, 1:]
    ca_re = c_re[:, None] * nx_re[:, :, :, None] - c_im[:, None] * nx_im[:, :, :, None]
    ca_im = c_re[:, None] * nx_im[:, :, :, None] + c_im[:, None] * nx_re[:, :, :, None]

    def out_w(a):
        return grp(a).transpose(0, 2, 3, 5, 1, 4).reshape(dd, ncb, S5_SW, S5_L * S5_GS)

    vout = jnp.concatenate([out_w(ca_re), out_w(-ca_im)], axis=2)
    cat_re = c_re[:, None] * pw_re[:, :S5_L, :, None] - c_im[:, None] * pw_im[:, :S5_L, :, None]
    cat_im = c_re[:, None] * pw_im[:, :S5_L, :, None] + c_im[:, None] * pw_re[:, :S5_L, :, None]
    kk = jnp.einsum("dlgcp,dgpe->dlgce", jnp.concatenate([cat_re, -cat_im], axis=-1),
                    jnp.concatenate([bb_re, bb_im], axis=2), precision=HI)
    zero = jnp.zeros_like(kk[:, 0])
    kt = jnp.stack([jnp.stack([kk[:, j - i] if j >= i else zero for j in range(S5_L)], axis=1)
                    for i in range(S5_L)], axis=1)
    kt = kt.reshape(dd, S5_L, S5_L, ncb, S5_CB, S5_GS, S5_GS)
    kt = kt.transpose(0, 3, 1, 4, 6, 2, 5).reshape(dd, ncb, S5_L * LANES, S5_L * S5_GS)
    wst, vout, kt = _s5_expand(wst, vout, kt)
    sw = lambda a: a.reshape(dd, ncb, 1, S5_SW)
    return dict(s5_wst=wst, s5_vout=vout, s5_kt=kt, s5_alre=sw(pw_re[:, S5_L]), s5_alim=sw(pw_im[:, S5_L]),
                s5_d=w["s5_d"].reshape(dd, W_BR // LANES, 1, LANES), s5_glu_w=w["s5_glu_w"].astype(BF16))


def _unit_lower_solve(ms, rhs, c):
    def split(a):
        hi = a.astype(BF16)
        return hi, (a - hi.astype(F32)).astype(BF16)

    def dot3(a, b):
        return _dot(a[0], b[0]) + _dot(a[1], b[0]) + _dot(a[0], b[1])

    n = range(len(ms))
    ms = [split(m) for m in ms]
    rs = [split(r) for r in rhs]
    xs = [rhs[i] - dot3(ms[i], rs[i]) for i in n]
    k = 2
    while k < c:
        if k <= 2 ** GDN_SPLIT_LEVELS:
            ms = [split(dot3(ms[i], ms[i])) for i in n]
            rs = [split(x) for x in xs]
            xs = [xs[i] + dot3(ms[i], rs[i]) for i in n]
        else:
            ms = [(_dot(ms[i][0], ms[i][0]).astype(BF16), None) for i in n]
            xs = [xs[i] + _dot(ms[i][0], xs[i].astype(BF16)) for i in n]
        k *= 2
    return xs


def _gdn_kernel(l_ref, q_ref, k_ref, v_ref, sm_ref, gg_ref, bq_ref, bk_ref, bv_ref, cwq_ref, cwk_ref, cwv_ref,
                alog_ref, dtb_ref, nw_ref, s0_ref, y_ref, bufo_ref, so_ref,
                xp_scr, qkv_scr, gx_scr, bx_scr, gc_scr, s_scr, *, tt, nt, c):
    ti = pl.program_id(1)
    log2c = c.bit_length() - 1

    @pl.when(ti == 0)
    def _():
        for i, b_ref in enumerate((bq_ref, bk_ref, bv_ref)):
            xp_scr[i, 5:8, :] = b_ref[...]
        s_scr[...] = s0_ref[...]

    for i, (x_ref, cw_ref) in enumerate(((q_ref, cwq_ref), (k_ref, cwk_ref), (v_ref, cwv_ref))):
        x = x_ref[...]
        xp_scr[i, 8:8 + tt, :] = x
        cw = cw_ref[...]
        cv = (cw[3:4] * x + cw[2:3] * xp_scr[i, 7:7 + tt, :]
              + cw[1:2] * xp_scr[i, 6:6 + tt, :] + cw[0:1] * xp_scr[i, 5:5 + tt, :])
        tail = x[tt - 3:tt, :]
        xp_scr[i, 5:8, :] = tail
        bufo_ref[:, i * W_BR:(i + 1) * W_BR] = tail
        cv = _silu(cv)
        if i < 2:
            scale = GDN_HD ** -0.5 if i == 0 else 1.0
            for h in range(GDN_H):
                sl = slice(h * GDN_HD, (h + 1) * GDN_HD)
                xh = cv[:, sl]
                qkv_scr[i, :, sl] = xh * (lax.rsqrt(jnp.sum(xh * xh, axis=-1, keepdims=True) + 1e-6) * scale)
        else:
            qkv_scr[i] = cv

    sm = sm_ref[...]
    gsm = -jnp.exp(alog_ref[...]) * _softplus(sm + dtb_ref[...])
    bsm = _sigmoid(sm)
    src = _iota((LANES, W_BR), 0)
    head = _iota((LANES, W_BR), 1) >> _log2(GDN_HD)
    gx_scr[...] = _dot_hi(gsm, jnp.where(src - SM_A == head, 1.0, 0.0))
    bx_scr[...] = _dot_hi(bsm, jnp.where(src - SM_B == head, 1.0, 0.0))
    headc = _iota((LANES, GDN_H * c), 1) >> log2c
    gc_scr[...] = _dot_hi(gsm, jnp.where(_iota((LANES, GDN_H * c), 0) - SM_A == headc, 1.0, 0.0))

    rowi = _iota((c, c), 0)
    coli = _iota((c, c), 1)
    incl = coli <= rowi
    strict = coli < rowi
    ltri = jnp.where(incl, 1.0, 0.0)
    upper = jnp.where(_iota((c, GDN_H * c), 0) > (_iota((c, GDN_H * c), 1) & (c - 1)), 1.0, 0.0)
    nw = nw_ref[...]

    def chunk(ci, carry):
        r0 = pl.multiple_of(ci * c, c)
        gcb_all = _dot_hi(ltri, gx_scr[pl.ds(r0, c), :])
        diffs = _dot_hi(ltri, gc_scr[pl.ds(r0, c), :] * upper)
        beta_all = bx_scr[pl.ds(r0, c), :]
        hs = range(GDN_H)
        sls = [slice(h * GDN_HD, (h + 1) * GDN_HD) for h in hs]
        q = [qkv_scr[0, pl.ds(r0, c), sl] for sl in sls]
        k = [qkv_scr[1, pl.ds(r0, c), sl] for sl in sls]
        gcb = [gcb_all[:, sl] for sl in sls]
        decay = [jnp.where(incl, jnp.exp(diffs[:, h * c:(h + 1) * c]), 0.0) for h in hs]
        kb = [k[h] * beta_all[:, sls[h]] for h in hs]
        kbf = [x.astype(BF16) for x in k]
        kq = [_dot_nt(jnp.concatenate([kb[h], q[h]], axis=0).astype(BF16), kbf[h]) for h in hs]
        m = [jnp.where(strict, kq[h][:c] * decay[h], 0.0) for h in hs]
        qk = [(kq[h][c:] * decay[h]).astype(BF16) for h in hs]
        rhs = [jnp.concatenate([qkv_scr[2, pl.ds(r0, c), sls[h]] * beta_all[:, sls[h]], kb[h] * jnp.exp(gcb[h])],
                               axis=1) for h in hs]
        sol = _unit_lower_solve(m, rhs, c)
        s = [s_scr[h] for h in hs]
        sb = [x.astype(BF16) for x in s]
        ws = [_dot(jnp.concatenate([sol[h][:, GDN_HD:], q[h] * jnp.exp(gcb[h])], axis=0).astype(BF16), sb[h])
              for h in hs]
        vnb = [(sol[h][:, :GDN_HD] - ws[h][:c]).astype(BF16) for h in hs]
        o = [ws[h][c:] + _dot(qk[h], vnb[h]) for h in hs]
        for h in hs:
            glast = gcb[h][c - 1:c, :]
            s_scr[h] = s[h] * jnp.exp(glast) + _dot_tn((k[h] * jnp.exp(glast - gcb[h])).astype(BF16), vnb[h])
        for h in hs:
            on = o[h] * lax.rsqrt(jnp.mean(o[h] * o[h], axis=-1, keepdims=True) + 1e-6) * nw
            y_ref[pl.ds(r0, c), sls[h]] = (on * _silu(gg_ref[pl.ds(r0, c), sls[h]])).astype(y_ref.dtype)
        return carry

    lax.fori_loop(0, tt // c, chunk, 0)

    @pl.when(ti == nt - 1)
    def _():
        so_ref[...] = s_scr[...]


def _gdn(lidx, z3, buf, s0, p, ydtype, chunk):
    s, t, _ = z3.shape
    tt = min(t, 256)
    nt = t // tt
    c = min(chunk, tt)
    zspec = lambda col: pl.BlockSpec((None, tt, W_BR), lambda b, i, l: (b, i, col // W_BR))
    bspec = lambda j: pl.BlockSpec((None, 3, W_BR), lambda b, i, l: (b, 0, j))
    cspec = lambda j: pl.BlockSpec((None, CONV_K, W_BR), lambda b, i, l: (l[0], 0, j))
    rspec = pl.BlockSpec((None, 1, LANES), lambda b, i, l: (l[0], 0, 0))
    sspec = pl.BlockSpec((None, GDN_H, GDN_HD, GDN_HD), lambda b, i, l: (b, 0, 0, 0))
    return pl.pallas_call(
        functools.partial(_gdn_kernel, tt=tt, nt=nt, c=c),
        grid_spec=pltpu.PrefetchScalarGridSpec(
            num_scalar_prefetch=1, grid=(s, nt),
            in_specs=[zspec(C_QKV), zspec(C_QKV + W_BR), zspec(C_QKV + 2 * W_BR),
                      pl.BlockSpec((None, tt, LANES), lambda b, i, l: (b, i, C_SMALL_B // LANES)),
                      zspec(C_GGDN), bspec(0), bspec(1), bspec(2), cspec(0), cspec(1), cspec(2),
                      rspec, rspec, rspec, sspec],
            out_specs=[pl.BlockSpec((None, tt, W_BR), lambda b, i, l: (b, i, 0)),
                       pl.BlockSpec((None, 3, 3 * W_BR), lambda b, i, l: (b, 0, 0)),
                       sspec],
            scratch_shapes=[pltpu.VMEM((3, tt + 8, W_BR), F32), pltpu.VMEM((3, tt, W_BR), F32),
                            pltpu.VMEM((tt, W_BR), F32), pltpu.VMEM((tt, W_BR), F32),
                            pltpu.VMEM((tt, GDN_H * c), F32), pltpu.VMEM((GDN_H, GDN_HD, GDN_HD), F32)]),
        out_shape=[jax.ShapeDtypeStruct((s, t, W_BR), ydtype),
                   jax.ShapeDtypeStruct((s, 3, 3 * W_BR), F32),
                   jax.ShapeDtypeStruct((s, GDN_H, GDN_HD, GDN_HD), F32)],
        compiler_params=_cparams(("parallel", "arbitrary")),
        name="gdn",
    )(lidx, z3, z3, z3, z3, z3, buf, buf, buf, p["gdn_conv_w"], p["gdn_conv_w"], p["gdn_conv_w"],
      p["gdn_a_log"], p["gdn_dt_bias"], p["gdn_norm_w"], s0)


PAGES_PER_STEP = 8
MAX_TAIL_STEPS = 3


KEY_TILE = 512


def _key_tile(tk):
    return max(k for k in range(KEY_TILE, min(tk, LANES * SEL_BLOCK) + 1, KEY_TILE) if tk % k == 0)


def _gather_kernel(l_ref, pt_ref, *refs, n_past):
    pages = refs[:PAGES_PER_STEP]
    zn_ref, cmp_ref, sel_ref = refs[PAGES_PER_STEP:]
    i = pl.program_id(1)
    half = 2 * NSA_KVH * NSA_HD

    @pl.when(i < n_past)
    def _():
        per_tok = 4 * NSA_KVH
        for k, p_ref in enumerate(pages):
            rows = slice(k * PAGE_SIZE, (k + 1) * PAGE_SIZE)
            for r in (2, 3):
                for g in range(NSA_KVH):
                    x = p_ref[pl.ds(r * NSA_KVH + g, PAGE_SIZE, stride=per_tok), :]
                    cols = slice(((r - 2) * NSA_KVH + g) * NSA_HD, ((r - 2) * NSA_KVH + g + 1) * NSA_HD)
                    sel_ref[rows, cols] = x.astype(BF16)
        halves = PAGE_SIZE // CMP_STRIDE
        for r in (0, 1):
            for g in range(NSA_KVH):
                flat = [jnp.concatenate([p_ref[pl.ds(s * per_tok + r * NSA_KVH + g, halves,
                                                     stride=CMP_STRIDE * per_tok), :] for p_ref in pages], axis=0)
                        for s in range(CMP_STRIDE)]
                cmp_ref[r, g] = jnp.concatenate(flat, axis=1).astype(BF16)

    @pl.when(i >= n_past)
    def _():
        sel_ref[...] = jnp.zeros(sel_ref.shape, BF16)

    @pl.when(i == n_past)
    def _():
        new = zn_ref[:, half:2 * half]
        sel_ref[0:2 * SUBLANES, :] = jnp.concatenate([new, jnp.zeros_like(new)], axis=0).astype(BF16)


def _gather_ctx(lidx, pt_flat, cache, z3):
    b, t_new, _ = z3.shape
    assert t_new == SUBLANES
    n_pages = pt_flat.shape[0] // b
    assert n_pages % PAGES_PER_STEP == 0
    n_past = n_pages // PAGES_PER_STEP
    rows = PAGES_PER_STEP * PAGE_SIZE
    n_tail = min(range(1, MAX_TAIL_STEPS + 1), key=lambda k: ((n_past + k) * rows // _key_tile((n_past + k) * rows), k))
    width = 4 * NSA_KVH * NSA_HD
    half = width // 2
    n_pool = cache.shape[1]
    cache = cache.reshape(cache.shape[0] * n_pool, PAGE_SIZE * 4 * NSA_KVH, NSA_HD)

    def pspec(k):
        def imap(bi, i, l, pt):
            page = jnp.minimum(i * PAGES_PER_STEP + k, n_pages - 1)
            return (l[0] * n_pool + pt[bi * n_pages + page], 0, 0)
        return pl.BlockSpec((None, PAGE_SIZE * 4 * NSA_KVH, NSA_HD), imap)

    return pl.pallas_call(
        functools.partial(_gather_kernel, n_past=n_past),
        grid_spec=pltpu.PrefetchScalarGridSpec(
            num_scalar_prefetch=2, grid=(b, n_past + n_tail),
            in_specs=[pspec(k) for k in range(PAGES_PER_STEP)]
            + [pl.BlockSpec((None, t_new, width), lambda bi, i, l, pt: (bi, 0, C_KV // width))],
            out_specs=[pl.BlockSpec((None, 2, NSA_KVH, rows // CMP_STRIDE, CMP_STRIDE * NSA_HD),
                                    lambda bi, i, l, pt: (bi, 0, 0, jnp.minimum(i, n_past - 1), 0)),
                       pl.BlockSpec((None, rows, half), lambda bi, i, l, pt: (bi, i, 0))]),
        out_shape=[jax.ShapeDtypeStruct((b, 2, NSA_KVH, n_past * rows // CMP_STRIDE, CMP_STRIDE * NSA_HD), BF16),
                   jax.ShapeDtypeStruct((b, (n_past + n_tail) * rows, half), BF16)],
        compiler_params=_cparams(("parallel", "arbitrary")),
        name="nsa_gather",
    )(lidx, pt_flat, *([cache] * PAGES_PER_STEP), z3)


def _kv_rows_kernel(z_ref, o_ref, *, tt):
    per_tok = 4 * NSA_KVH
    for rg in range(per_tok):
        o_ref[pl.ds(rg, tt, stride=per_tok), :] = z_ref[:, rg * NSA_HD:(rg + 1) * NSA_HD]


def _kv_rows(z3):
    s, t, _ = z3.shape
    tt = min(t, 512)
    per_tok = 4 * NSA_KVH
    return pl.pallas_call(
        functools.partial(_kv_rows_kernel, tt=tt), grid=(s, t // tt),
        in_specs=[pl.BlockSpec((None, tt, per_tok * NSA_HD), lambda b, i: (b, i, C_KV // (per_tok * NSA_HD)))],
        out_specs=pl.BlockSpec((None, tt * per_tok, NSA_HD), lambda b, i: (b, i, 0)),
        out_shape=jax.ShapeDtypeStruct((s, t * per_tok, NSA_HD), F32),
        compiler_params=_cparams(("parallel", "parallel")),
        name="nsa_kv_rows",
    )(z3)


def _cmp_bias_kernel(pe_ref, w1_ref, o_ref):
    o_ref[...] = _dot_hi(pe_ref[...], w1_ref[...].astype(F32))


def _cmp_bias(pe8, w1):
    dd, _, flat, wide = w1.shape
    return pl.pallas_call(
        _cmp_bias_kernel, grid=(dd, 2),
        in_specs=[pl.BlockSpec((None, SUBLANES, flat), lambda d, c: (d, 0, 0)),
                  pl.BlockSpec((None, None, flat, wide), lambda d, c: (d, c, 0, 0))],
        out_specs=pl.BlockSpec((None, None, SUBLANES, wide), lambda d, c: (d, c, 0, 0)),
        out_shape=jax.ShapeDtypeStruct((dd, 2, SUBLANES, wide), F32),
        compiler_params=_cparams(("parallel", "parallel")),
        name="nsa_cmp_bias",
    )(pe8, w1)


def _cmp_kernel(l_ref, x0_ref, x1_ref, w1_ref, bias_ref, w2_ref, o_ref, carry_scr, *, nh, flat):
    @pl.when(pl.program_id(2) == 0)
    def _():
        carry_scr[...] = jnp.zeros(carry_scr.shape, F32)

    w1 = w1_ref[...]
    bias = bias_ref[...]
    last = _iota((nh, NSA_HD), 0) == nh - 1
    for g, x_ref in enumerate((x0_ref, x1_ref)):
        if flat:
            ucat = x_ref[...]
        else:
            ucat = jnp.concatenate([x_ref[pl.ds(s, nh, stride=CMP_STRIDE), :] for s in range(CMP_STRIDE)], axis=1)
        hh = _dot(ucat.astype(BF16), w1)
        hf = hh[:, :NSA_HD] + bias[0:1, :NSA_HD]
        hs = hh[:, NSA_HD:] + bias[1:2, NSA_HD:]
        hs_next = jnp.where(last, carry_scr[g], pltpu.roll(hs, nh - 1, axis=0))
        carry_scr[g] = hs[0:1, :]
        hid = _silu(hf + hs_next)
        o_ref[g] = _dot(hid.astype(BF16), w2_ref[...]).astype(BF16)


def _compress(lidx, src, col0, n_rows, p):
    b = src.shape[0]
    tr = min(n_rows, 4096)
    nt = n_rows // tr
    nh = tr // CMP_STRIDE
    flat = col0 is None
    if flat:
        xspec = lambda g: pl.BlockSpec((None, None, None, nh, CMP_STRIDE * NSA_HD),
                                       lambda bi, c, i, l: (bi, c, g, nt - 1 - i, 0))
    else:
        xspec = lambda g: pl.BlockSpec((None, tr, NSA_HD),
                                       lambda bi, c, i, l: (bi, nt - 1 - i, col0 + NSA_KVH * c + g))
    return pl.pallas_call(
        functools.partial(_cmp_kernel, nh=nh, flat=flat),
        grid_spec=pltpu.PrefetchScalarGridSpec(
            num_scalar_prefetch=1, grid=(b, 2, nt),
            in_specs=[xspec(0), xspec(1),
                      pl.BlockSpec((None, None, CMP_STRIDE * NSA_HD, 2 * NSA_HD), lambda bi, c, i, l: (l[0], c, 0, 0)),
                      pl.BlockSpec((None, None, SUBLANES, 2 * NSA_HD), lambda bi, c, i, l: (l[0], c, 0, 0)),
                      pl.BlockSpec((None, None, NSA_HD, NSA_HD), lambda bi, c, i, l: (l[0], c, 0, 0))],
            out_specs=pl.BlockSpec((None, None, NSA_KVH, nh, NSA_HD), lambda bi, c, i, l: (bi, c, 0, nt - 1 - i, 0)),
            scratch_shapes=[pltpu.VMEM((NSA_KVH, 1, NSA_HD), F32)]),
        out_shape=jax.ShapeDtypeStruct((b, 2, NSA_KVH, n_rows // CMP_STRIDE, NSA_HD), BF16),
        compiler_params=_cparams(("parallel", "parallel", "arbitrary")),
        name="nsa_compress",
    )(lidx, src, src, p["nsa_w1"], p["nsa_bias"], p["nsa_w2"])


def _masked_softmax(s, valid):
    s = jnp.where(valid, s, NEG)
    m = jnp.max(s, axis=-1, keepdims=True)
    e = jnp.where(valid, jnp.exp2(s - m), 0.0)
    den = jnp.sum(e, axis=-1, keepdims=True)
    return e / jnp.where(den > 0.0, den, 1.0)


def _attn_kernel(l_ref, q_ref, sm_ref, gn_ref, kc_ref, vc_ref, ks_ref, vs_ref, kw_ref, vw_ref, y_ref, *,
                 qb, pos0, ncp, ns, nsp, kt, wn, wpos0, tw):
    g = pl.program_id(1)
    q0 = pl.program_id(2) * qb
    qpos0 = pos0 + q0
    hd = NSA_HD
    q = q_ref[...] * (hd ** -0.5 * LOG2E)
    qr = jnp.concatenate([q[:, j * hd:(j + 1) * hd] for j in range(NSA_GQ)], axis=0).astype(BF16)
    slope_g = jnp.where(g == 0, LOG2E, LOG2E * 2.0 ** -NSA_GQ)
    slopes = [slope_g * 2.0 ** -(j + 1) for j in range(NSA_GQ)]
    heads = lambda a: [a[j * qb:(j + 1) * qb] for j in range(NSA_GQ)]

    dist = (qpos0 + _iota((qb, ncp), 0)) - (_iota((qb, ncp), 1) * CMP_STRIDE + (CMP_BLOCK - 1))
    valid = dist >= 0
    distf = dist.astype(F32)
    sc = heads(_dot_nt(qr, kc_ref[...]))
    pc = [_masked_softmax(sc[j] - slopes[j] * distf, valid) for j in range(NSA_GQ)]
    o_c = heads(_dot(jnp.concatenate(pc, axis=0).astype(BF16), vc_ref[...]))
    imp = pc[0] + pc[1] + pc[2] + pc[3]
    pool = jnp.where((_iota((ncp, nsp), 0) >> _log2(CMP_PER_SEL)) == _iota((ncp, nsp), 1), 1.0, 0.0)
    imp = _dot_hi(imp, pool)

    blk = _iota((qb, nsp), 1)
    qp = qpos0 + _iota((qb, nsp), 0)
    val = jnp.where(blk * SEL_BLOCK > qp, -FORCE, imp)
    val = jnp.where(blk == (qp >> _log2(SEL_BLOCK)), FORCE, jnp.where(blk == 0, FORCE, val))
    val = jnp.where(blk >= ns, -3.0 * FORCE, val)
    top = float(min(TOP_N, ns))
    if qb == LANES and nsp == LANES:
        nsr = -(-ns // SUBLANES) * SUBLANES
        val_t = val.T[:nsr]
        blk_t = _iota((nsr, qb), 0)
        rank = jnp.zeros((nsr, qb), F32)
        for bidx in range(ns):
            cand = val_t[bidx:bidx + 1, :]
            rank = rank + jnp.where(cand > val_t, 1.0, jnp.where(cand == val_t, jnp.where(blk_t > bidx, 1.0, 0.0), 0.0))
        sel_t = jnp.where(rank < top, 1.0, 0.0)
        selb = jnp.concatenate([sel_t, jnp.zeros((nsp - nsr, qb), F32)], axis=0).T.astype(BF16)
    else:
        rank = jnp.zeros((qb, nsp), F32)
        for bidx in range(ns):
            cand = val[:, bidx:bidx + 1]
            rank = rank + jnp.where(cand > val, 1.0, jnp.where(cand == val, jnp.where(blk > bidx, 1.0, 0.0), 0.0))
        selb = jnp.where(rank < top, 1.0, 0.0).astype(BF16)

    n_tiles = (qpos0 + qb + kt - 1) // kt
    qh = heads(qr)
    spread = jnp.where(_iota((LANES, kt), 0) == (_iota((LANES, kt), 1) >> _log2(SEL_BLOCK)), 1.0, 0.0).astype(BF16)

    def tile_bias(t):
        k0 = t * kt
        d = (qpos0 + _iota((qb, kt), 0)) - (k0 + _iota((qb, kt), 1))
        pick_blk = jnp.where(_iota((nsp, LANES), 0) == (k0 >> _log2(SEL_BLOCK)) + _iota((nsp, LANES), 1),
                             1.0, 0.0).astype(BF16)
        sel_tile = _dot(selb, pick_blk).astype(BF16)
        ok = jnp.where(d >= 0, _dot(sel_tile, spread), 0.0) > 0.5
        return jnp.where(ok, d.astype(F32), -NEG)

    def tile(t, carry):
        k0 = pl.multiple_of(t * kt, kt)
        kk = ks_ref[pl.ds(k0, kt), :].astype(BF16)
        vv = vs_ref[pl.ds(k0, kt), :].astype(BF16)
        per_head = qb == Q_BLOCK
        s = [_dot_nt(qh[j], kk) for j in range(NSA_GQ)] if per_head else heads(_dot_nt(qr, kk))
        bias = carry[NSA_GQ]
        bias_next = tile_bias(t + 1)
        stats, prs = [], []
        for j in range(NSA_GQ):
            m, lsum, _ = carry[j]
            sm = s[j] - slopes[j] * bias
            m_new = jnp.maximum(m, jnp.max(sm, axis=-1, keepdims=True))
            pr = jnp.exp2(sm - m_new)
            alpha = jnp.exp2(m - m_new)
            stats.append((m_new, alpha * lsum + jnp.sum(pr, axis=-1, keepdims=True), alpha))
            prs.append(pr.astype(BF16))
        pv = [_dot(pr, vv) for pr in prs] if per_head else heads(_dot(jnp.concatenate(prs, axis=0), vv))
        new = [(stats[j][0], stats[j][1], stats[j][2] * carry[j][2] + pv[j]) for j in range(NSA_GQ)]
        return tuple(new) + (bias_next,)

    init = (jnp.full((qb, 1), NEG, F32), jnp.zeros((qb, 1), F32), jnp.zeros((qb, hd), F32))
    fin = lax.fori_loop(0, n_tiles, tile, (init,) * NSA_GQ + (tile_bias(0),))
    o_s = [acc / lsum for _, lsum, acc in fin[:NSA_GQ]]

    k0w = pl.multiple_of(jnp.clip(q0 - WINDOW, 0, tw - wn), SUBLANES)
    dw = (qpos0 + _iota((qb, wn), 0)) - (wpos0 + k0w + _iota((qb, wn), 1))
    okw = jnp.abs(2 * dw - (WINDOW - 1)) < WINDOW
    dwf = dw.astype(F32)
    sw = heads(_dot_nt(qr, kw_ref[pl.ds(k0w, wn), :].astype(BF16)))
    pw = [_masked_softmax(sw[j] - slopes[j] * dwf, okw) for j in range(NSA_GQ)]
    o_w = heads(_dot(jnp.concatenate(pw, axis=0).astype(BF16), vw_ref[pl.ds(k0w, wn), :].astype(BF16)))

    gate = _sigmoid(sm_ref[...])
    lane = _iota((qb, LANES), 1)
    pick = lambda idx: jnp.sum(jnp.where(lane == idx, gate, 0.0), axis=-1, keepdims=True)
    for j in range(NSA_GQ):
        base = SM_GL + 3 * (g * NSA_GQ + j)
        o = pick(base) * o_c[j] + pick(base + 1) * o_s[j] + pick(base + 2) * o_w[j]
        y_ref[:, j * hd:(j + 1) * hd] = (o * _silu(gn_ref[:, j * hd:(j + 1) * hd])).astype(y_ref.dtype)


def _attention(lidx, z3, cmp_kv, ks_src, ks_col, vs_col, kw_src, kw_col, vw_col, pos0, wpos0, ydtype):
    b, t, _ = z3.shape
    qb = min(Q_BLOCK, t)
    ncp = cmp_kv.shape[3]
    tk = ks_src.shape[1]
    tw = kw_src.shape[1]
    if qb == Q_BLOCK:
        kt = KEY_TILE
    else:
        kt = _key_tile(tk)
    ns = -(-(pos0 + t) // SEL_BLOCK)
    nsp = -(-ns // LANES) * LANES
    wn = WINDOW + Q_BLOCK
    assert tk % kt == 0 and tk >= pos0 + t and tw >= wn and ncp // CMP_PER_SEL <= nsp
    hw = NSA_GQ * NSA_HD
    kern = functools.partial(_attn_kernel, qb=qb, pos0=pos0, ncp=ncp, ns=ns, nsp=nsp, kt=kt, wn=wn, wpos0=wpos0, tw=tw)
    kvspec = lambda rows, col: pl.BlockSpec((None, rows, NSA_HD), lambda bi, g, i, l: (bi, 0, col + g))
    cspec = lambda c: pl.BlockSpec((None, None, None, ncp, NSA_HD), lambda bi, g, i, l: (bi, c, g, 0, 0))
    return pl.pallas_call(
        kern,
        grid_spec=pltpu.PrefetchScalarGridSpec(
            num_scalar_prefetch=1, grid=(b, NSA_KVH, t // qb),
            in_specs=[pl.BlockSpec((None, qb, hw), lambda bi, g, i, l: (bi, i, C_QNSA // hw + g)),
                      pl.BlockSpec((None, qb, LANES), lambda bi, g, i, l: (bi, i, C_SMALL_A // LANES)),
                      pl.BlockSpec((None, qb, hw), lambda bi, g, i, l: (bi, i, C_GNSA // hw + g)),
                      cspec(0), cspec(1),
                      kvspec(tk, ks_col), kvspec(tk, vs_col), kvspec(tw, kw_col), kvspec(tw, vw_col)],
            out_specs=pl.BlockSpec((None, qb, hw), lambda bi, g, i, l: (bi, i, g))),
        out_shape=jax.ShapeDtypeStruct((b, t, W_BR), ydtype),
        compiler_params=_cparams(("parallel", "parallel", "arbitrary")),
        name="nsa_attention",
    )(lidx, z3, z3, z3, cmp_kv, cmp_kv, ks_src, ks_src, kw_src, kw_src)


def _merge_kernel(l_ref, y0_ref, y1_ref, y2_ref, y3_ref, g0_ref, g1_ref, g2_ref, g3_ref, w_ref, o_ref):
    acc = None
    for m, (y_ref, g_ref) in enumerate(((y0_ref, g0_ref), (y1_ref, g1_ref), (y2_ref, g2_ref), (y3_ref, g3_ref))):
        term = _sigmoid(g_ref[...]) * _dot(y_ref[...].astype(BF16), w_ref[m])
        acc = term if acc is None else acc + term
    o_ref[...] = acc.astype(o_ref.dtype)


def _merge(lidx, ys, z2d, p):
    n = z2d.shape[0]
    tm = min(n, 256)
    yspec = pl.BlockSpec((tm, W_BR), lambda i, l: (i, 0))
    gspec = lambda m: pl.BlockSpec((tm, D_MODEL), lambda i, l: (i, C_MERGE // D_MODEL + m))
    wspec = pl.BlockSpec((None, N_BRANCH, W_BR, D_MODEL), lambda i, l: (l[0], 0, 0, 0), pipeline_mode=pl.Buffered(1))
    return pl.pallas_call(
        _merge_kernel,
        grid_spec=pltpu.PrefetchScalarGridSpec(
            num_scalar_prefetch=1, grid=(n // tm,),
            in_specs=[yspec] * N_BRANCH + [gspec(m) for m in range(N_BRANCH)] + [wspec],
            out_specs=pl.BlockSpec((tm, D_MODEL), lambda i, l: (i, 0))),
        out_shape=jax.ShapeDtypeStruct((n, D_MODEL), BF16),
        compiler_params=_cparams(("parallel",)),
        name="merge",
    )(lidx, *ys, z2d, z2d, z2d, z2d, p["w_branch"])


def _outproj_kernel(l_ref, m_ref, x_ref, w_ref, g_ref, b_ref, o_ref, *, alpha):
    v = alpha * x_ref[...] + _dot(m_ref[...], w_ref[...])
    mu = jnp.mean(v, axis=-1, keepdims=True)
    c = v - mu
    var = jnp.mean(c * c, axis=-1, keepdims=True)
    o_ref[...] = c * lax.rsqrt(var + 1e-5) * g_ref[...] + b_ref[...]


def _outproj(lidx, merged, x2d, p, alpha):
    n = x2d.shape[0]
    tm = min(n, 256)
    return pl.pallas_call(
        functools.partial(_outproj_kernel, alpha=alpha),
        grid_spec=pltpu.PrefetchScalarGridSpec(
            num_scalar_prefetch=1, grid=(n // tm,),
            in_specs=[pl.BlockSpec((tm, D_MODEL), lambda i, l: (i, 0)),
                      pl.BlockSpec((tm, D_MODEL), lambda i, l: (i, 0)),
                      pl.BlockSpec((None, D_MODEL, D_MODEL), lambda i, l: (l[0], 0, 0)),
                      pl.BlockSpec((None, 1, D_MODEL), lambda i, l: (l[0], 0, 0)),
                      pl.BlockSpec((None, 1, D_MODEL), lambda i, l: (l[0], 0, 0))],
            out_specs=pl.BlockSpec((tm, D_MODEL), lambda i, l: (i, 0))),
        out_shape=jax.ShapeDtypeStruct((n, D_MODEL), F32),
        compiler_params=_cparams(("parallel",)),
        input_output_aliases={2: 0},
        name="outproj_ln",
    )(lidx, merged, x2d, p["w_out"], p["ln_g"], p["ln_b"])


def _prep_params(w):
    p = {}
    row = lambda a: a[:, None, :]
    p["lru_conv_w"] = w["lru_conv_w"]
    p["lru_conv_b"] = row(w["lru_conv_b"])
    p["lru_wa"] = w["lru_wa"].astype(BF16)
    p["lru_ba"] = row(w["lru_ba"])
    p["lru_wx"] = w["lru_wx"].astype(BF16)
    p["lru_bx"] = row(w["lru_bx"])
    p["lru_lambda"] = row(w["lru_lambda"])
    p.update(_s5_weights(w))
    p["w_branch"] = w["w_branch"].astype(BF16)
    p["w_out"] = w["w_out"].astype(BF16)
    p["ln_g"] = row(w["ln_g"])
    p["ln_b"] = row(w["ln_b"])
    dd = w["gdn_a_log"].shape[0]
    lane_row = lambda a, off: jnp.zeros((dd, 1, LANES), F32).at[:, 0, off:off + a.shape[-1]].set(a)
    p["gdn_conv_w"] = w["gdn_conv_w"]
    p["gdn_a_log"] = lane_row(w["gdn_a_log"], SM_A)
    p["gdn_dt_bias"] = lane_row(w["gdn_dt_bias"], SM_A)
    p["gdn_norm_w"] = row(w["gdn_norm_w"])
    flat = CMP_STRIDE * NSA_HD
    w1 = w["nsa_cmp_w1"].reshape(dd, 2, 2, flat, NSA_HD).transpose(0, 1, 3, 2, 4)
    p["nsa_w1"] = w1.reshape(dd, 2, flat, 2 * NSA_HD).astype(BF16)
    pe = w["nsa_cmp_pos"].reshape(dd, 2, flat)
    p["nsa_bias"] = _cmp_bias(jnp.concatenate([pe, jnp.zeros((dd, SUBLANES - 2, flat), F32)], axis=1), p["nsa_w1"])
    p["nsa_w2"] = w["nsa_cmp_w2"].astype(BF16)
    return p


def _mixer_layer(lidx, x2d, s, t, state, nsa_branch, p, win, alpha, ydtype):
    lru_buf, lru_h, gdn_buf, gdn_s, s5_re, s5_im = state
    z = _inproj(lidx, win[0], x2d, win[1])
    z3 = z.reshape(s, t, NP)
    y_lru, lru_buf, lru_h = _lru(lidx, z3, lru_buf, lru_h, p, ydtype)
    y_nsa = nsa_branch(z3)
    y_gdn, gdn_buf, gdn_s = _gdn(lidx, z3, gdn_buf, gdn_s, p, ydtype, GDN_CHUNK)
    y_s5, s5_re, s5_im = _s5_scan(lidx, z3, s5_re, s5_im, p, s5_re.shape[1])
    y_s5 = _s5_glu(lidx, y_s5, z, p, ydtype)
    flat = lambda y: y.reshape(s * t, W_BR)
    merged = _merge(lidx, (flat(y_lru), flat(y_nsa), flat(y_gdn), y_s5), z, p)
    x_new = _outproj(lidx, merged, x2d, p, alpha)
    return x_new, z3, (lru_buf, lru_h, gdn_buf, gdn_s, s5_re, s5_im)


def kernel(x_prompt, x_sample, state_lru_h, state_lru_conv, cache_nsa_kv, cache_win_kv, state_gdn_s, state_gdn_conv, state_s5_re, state_s5_im, page_table, w_in, lru_conv_w, lru_conv_b, lru_wa, lru_ba, lru_wx, lru_bx, lru_lambda, nsa_cmp_pos, nsa_cmp_w1, nsa_cmp_w2, gdn_conv_w, gdn_a_log, gdn_dt_bias, gdn_norm_w, s5_lam_re, s5_lam_im, s5_log_dt, s5_b_re, s5_b_im, s5_c_re, s5_c_im, s5_d, s5_glu_w, w_branch, w_out, ln_g, ln_b):
    depth = w_in.shape[0]
    bp, tp, _ = x_prompt.shape
    db, ts, _ = x_sample.shape
    n_pages = page_table.shape[1]
    past = n_pages * PAGE_SIZE
    wbuf = cache_win_kv.shape[2]
    alpha = (2.0 * depth) ** 0.25
    kvw = 4 * NSA_KVH * NSA_HD
    winw = 2 * NSA_KVH * NSA_HD
    p = _prep_params(dict(
        lru_conv_w=lru_conv_w, lru_conv_b=lru_conv_b, lru_wa=lru_wa, lru_ba=lru_ba, lru_wx=lru_wx, lru_bx=lru_bx,
        lru_lambda=lru_lambda, nsa_cmp_pos=nsa_cmp_pos, nsa_cmp_w1=nsa_cmp_w1, nsa_cmp_w2=nsa_cmp_w2,
        gdn_conv_w=gdn_conv_w, gdn_a_log=gdn_a_log, gdn_dt_bias=gdn_dt_bias, gdn_norm_w=gdn_norm_w,
        s5_lam_re=s5_lam_re, s5_lam_im=s5_lam_im, s5_log_dt=s5_log_dt, s5_b_re=s5_b_re, s5_b_im=s5_b_im,
        s5_c_re=s5_c_re, s5_c_im=s5_c_im, s5_d=s5_d, s5_glu_w=s5_glu_w, w_branch=w_branch, w_out=w_out,
        ln_g=ln_g, ln_b=ln_b))
    assert w_in.shape[2] == sum(s[1] for s in IN_SEGMENTS)
    wp = (jnp.asarray(_tile_sources(), jnp.int32), jnp.swapaxes(w_in, 1, 2))
    p = lax.optimization_barrier(p)
    pt_flat = page_table.reshape(-1).astype(jnp.int32)
    sb = db if ts == S5_L else 1
    c0 = C_KV // LANES
    zero_state = (jnp.zeros((bp, CONV_K - 1, W_BR), F32), jnp.zeros((bp, 1, W_BR), F32),
                  jnp.zeros((bp, CONV_K - 1, 3 * W_BR), F32), jnp.zeros((bp, GDN_H, GDN_HD, GDN_HD), F32),
                  jnp.zeros((bp, 1, S5_G * S5_P), F32), jnp.zeros((bp, 1, S5_G * S5_P), F32))

    def layer(carry, l):
        xp, xs = carry
        lidx = l.reshape(1)
        at = lambda a: lax.dynamic_index_in_dim(a, l, 0, keepdims=False)

        def nsa_prompt(z3):
            ckv = _compress(lidx, z3, c0, tp, p)
            return _attention(lidx, z3, ckv, z3, c0 + 4, c0 + 6, z3, c0 + 8, c0 + 10, 0, 0, BF16)

        xp, zp3, st_p = _mixer_layer(lidx, xp, bp, tp, zero_state, nsa_prompt, p, wp, alpha, BF16)

        win_state = {}

        def nsa_sample(z3):
            cmp_rows, sel_rows = _gather_ctx(lidx, pt_flat, cache_nsa_kv, z3)
            ckv = _compress(lidx, cmp_rows, None, past, p)
            win = jnp.concatenate([at(cache_win_kv).reshape(db, wbuf, winw),
                                   z3[:, :, C_KV + kvw:C_KV + kvw + winw]], axis=1)
            win_state["win"] = win
            pad = max(0, WINDOW + Q_BLOCK - (wbuf + ts))
            win_pad = jnp.concatenate([win, jnp.zeros((db, pad, winw), F32)], axis=1)
            return _attention(lidx, z3, ckv, sel_rows, 0, 2, win_pad, 0, 2, past, past - wbuf, F32)

        st_in = (at(state_lru_conv), at(state_lru_h)[:, None, :], at(state_gdn_conv), at(state_gdn_s),
                 at(state_s5_re).reshape(db // sb, sb, S5_G * S5_P), at(state_s5_im).reshape(db // sb, sb, S5_G * S5_P))
        xs, zs3, st_s = _mixer_layer(lidx, xs, db, ts, st_in, nsa_sample, p, wp, alpha, F32)

        def outs(st, z3, s, t, win):
            lru_buf, lru_h, gdn_buf, gdn_s, s5_re, s5_im = st
            return (lru_h.reshape(s, W_BR), lru_buf,
                    _kv_rows(z3).reshape(s, t, 4, NSA_KVH, NSA_HD),
                    win.reshape(s, win.shape[1], 2, NSA_KVH, NSA_HD),
                    gdn_s, gdn_buf, s5_re.reshape(s, S5_G, S5_P), s5_im.reshape(s, S5_G, S5_P))

        win_p = zp3[:, tp - min(WINDOW, tp):, C_KV + kvw:C_KV + kvw + winw]
        win_s = win_state["win"][:, wbuf + ts - min(WINDOW, past + ts):]
        return (xp, xs), (outs(st_p, zp3, bp, tp, win_p), outs(st_s, zs3, db, ts, win_s))

    (xp, xs), (op, os_) = lax.scan(layer, (x_prompt.reshape(bp * tp, D_MODEL), x_sample.reshape(db * ts, D_MODEL)),
                                   jnp.arange(depth, dtype=jnp.int32))
    return (xp.reshape(bp, tp, D_MODEL), xs.reshape(db, ts, D_MODEL)) + tuple(op) + tuple(os_)
```

```python
import functools

import jax
import jax.numpy as jnp
from jax import lax
from jax.experimental import pallas as pl
from jax.experimental.pallas import tpu as pltpu

F32 = jnp.float32
BF16 = jnp.bfloat16
HI = lax.Precision.HIGHEST

D_MODEL = 2048
W_BR = D_MODEL // 2
N_BRANCH = 4
CONV_K = 4
LRU_BLOCKS = 8
LRU_BS = W_BR // LRU_BLOCKS
LRU_C = 8.0
NSA_HD = 128
NSA_H = 8
NSA_KVH = 2
NSA_GQ = NSA_H // NSA_KVH
CMP_STRIDE = 16
CMP_BLOCK = 32
SEL_BLOCK = 64
CMP_PER_SEL = SEL_BLOCK // CMP_STRIDE
TOP_N = 16
WINDOW = 512
Q_BLOCK = 128
PAGE_SIZE = 128
GDN_HD = 128
GDN_H = 8
GDN_CHUNK = 64
GDN_SPLIT_LEVELS = 2
S5_GS = 16
S5_G = W_BR // S5_GS
S5_P = 64
S5_L = 8
NEG = -1e30
FORCE = 1e9
LOG2E = 1.4426950408889634

LANES = 128
SUBLANES = 8
VMEM_LIMIT = 56 * 1024 * 1024

IN_TILE = 512
C_MERGE = 0
C_ULRU = 8192
C_GLRU = 9216
C_QNSA = 10240
C_GNSA = 11264
C_GGDN = 12288
C_US5 = 13312
C_GS5 = 14336
C_QKV = 15360
C_KV = 18432
C_SMALL_A = 19968
C_SMALL_B = 20480
NP = C_SMALL_B + IN_TILE
IN_SEGMENTS = (("u_lru", W_BR, C_ULRU), ("g_lru", W_BR, C_GLRU), ("q_nsa", W_BR, C_QNSA),
               ("kv", 6 * NSA_KVH * NSA_HD, C_KV), ("gl", 3 * NSA_H, None), ("g_nsa", W_BR, C_GNSA),
               ("qkv", 3 * W_BR, C_QKV), ("a", GDN_H, None), ("b", GDN_H, None),
               ("g_gdn", W_BR, C_GGDN), ("u_s5", W_BR, C_US5), ("g_s5", W_BR, C_GS5),
               ("merge", N_BRANCH * D_MODEL, C_MERGE))


def _segment_start(name):
    return sum(w for n, w, _ in IN_SEGMENTS[:[s[0] for s in IN_SEGMENTS].index(name)])


SRC_SMALL_A = _segment_start("gl") // LANES * LANES
SRC_SMALL_B = _segment_start("a") // LANES * LANES
SM_GL = _segment_start("gl") - SRC_SMALL_A
SM_A = _segment_start("a") - SRC_SMALL_B
SM_B = _segment_start("b") - SRC_SMALL_B


def _tile_sources():
    src = [None] * (NP // IN_TILE)
    at = 0
    for _, width, dst in IN_SEGMENTS:
        if dst is not None:
            for c in range(0, width, IN_TILE):
                src[(dst + c) // IN_TILE] = at + c
        at += width
    src[C_SMALL_A // IN_TILE] = SRC_SMALL_A
    src[C_SMALL_B // IN_TILE] = SRC_SMALL_B
    assert all(s is not None and s % SUBLANES == 0 and s + IN_TILE <= at for s in src)
    return src


def _cparams(sem):
    return pltpu.CompilerParams(dimension_semantics=sem, vmem_limit_bytes=VMEM_LIMIT)


def _sigmoid(x):
    return 0.5 + 0.5 * jnp.tanh(0.5 * x)


def _silu(x):
    return x * _sigmoid(x)


def _softplus(x):
    return jnp.maximum(x, 0.0) + jnp.log1p(jnp.exp(-jnp.abs(x)))


def _dot(a, b):
    return jnp.dot(a, b, preferred_element_type=F32)


def _dot_hi(a, b):
    return jnp.dot(a, b, preferred_element_type=F32, precision=HI)


def _dot_nt(a, b):
    return lax.dot_general(a, b, (((1,), (1,)), ((), ())), preferred_element_type=F32)


def _dot_tn(a, b):
    return lax.dot_general(a, b, (((0,), (0,)), ((), ())), preferred_element_type=F32)


def _iota(shape, axis):
    return lax.broadcasted_iota(jnp.int32, shape, axis)


def _log2(n):
    assert n & (n - 1) == 0
    return n.bit_length() - 1


def _inproj_kernel(l_ref, src_ref, x_ref, w_ref, o_ref, xb_ref):
    @pl.when(pl.program_id(1) == 0)
    def _():
        xb_ref[...] = x_ref[...].astype(BF16)

    o_ref[...] = _dot_nt(xb_ref[...], w_ref[...].astype(BF16))


def _inproj(lidx, srcs, x2d, wt):
    n = x2d.shape[0]
    tm = min(n, 2048)
    wspec = pl.BlockSpec((pl.Squeezed(), pl.Element(IN_TILE), pl.Element(D_MODEL)),
                         lambda i, j, l, src: (l[0], pl.multiple_of(src[j], SUBLANES), 0))
    xspec = pl.BlockSpec((tm, D_MODEL), lambda i, j, l, src: (i, 0), pipeline_mode=pl.Buffered(1))
    return pl.pallas_call(
        _inproj_kernel,
        grid_spec=pltpu.PrefetchScalarGridSpec(
            num_scalar_prefetch=2, grid=(n // tm, NP // IN_TILE),
            in_specs=[xspec, wspec],
            out_specs=pl.BlockSpec((tm, IN_TILE), lambda i, j, l, src: (i, j)),
            scratch_shapes=[pltpu.VMEM((tm, D_MODEL), BF16)]),
        out_shape=jax.ShapeDtypeStruct((n, NP), F32),
        compiler_params=_cparams(("parallel", "arbitrary")),
        name="inproj",
    )(lidx, srcs, x2d, wt)


def _lru_kernel(l_ref, u_ref, g_ref, buf_ref, h0_ref, cw_ref, cb_ref, wa_ref, ba_ref, wx_ref, bx_ref, lam_ref,
                y_ref, bufo_ref, ho_ref, xp_scr, a_scr, b_scr, h_scr, *, tt, nt):
    ti = pl.program_id(1)

    @pl.when(ti == 0)
    def _():
        xp_scr[5:8, :] = buf_ref[...]
        h_scr[...] = h0_ref[...]

    u = u_ref[...]
    xp_scr[8:8 + tt, :] = u
    cw = cw_ref[...]
    xc = (cb_ref[...] + cw[3:4] * u + cw[2:3] * xp_scr[7:7 + tt, :]
          + cw[1:2] * xp_scr[6:6 + tt, :] + cw[0:1] * xp_scr[5:5 + tt, :])
    tail = u[tt - 3:tt, :]
    xp_scr[5:8, :] = tail
    sp = _softplus(-lam_ref[...])
    for n in range(LRU_BLOCKS):
        sl = slice(n * LRU_BS, (n + 1) * LRU_BS)
        xn = xc[:, sl]
        xb = xn.astype(BF16)
        r = _sigmoid(_dot(xb, wa_ref[n]) + ba_ref[:, sl])
        i = _sigmoid(_dot(xb, wx_ref[n]) + bx_ref[:, sl])
        a = jnp.exp(-LRU_C * r * sp[:, sl])
        a_scr[:, sl] = a
        b_scr[:, sl] = jnp.sqrt(1.0 - a * a) * (i * xn)

    def body(i, h):
        for k in range(SUBLANES):
            t = i * SUBLANES + k
            h = a_scr[pl.ds(t, 1), :] * h + b_scr[pl.ds(t, 1), :]
            b_scr[pl.ds(t, 1), :] = h
        return h

    h = lax.fori_loop(0, tt // SUBLANES, body, h_scr[...])
    h_scr[...] = h
    y_ref[...] = (b_scr[...] * _silu(g_ref[...])).astype(y_ref.dtype)

    @pl.when(ti == nt - 1)
    def _():
        bufo_ref[...] = tail
        ho_ref[...] = h


def _lru(lidx, z3, buf, h0, p, ydtype):
    s, t, _ = z3.shape
    tt = min(t, 512)
    nt = t // tt
    wspec = lambda shape: pl.BlockSpec((None,) + shape, lambda b, i, l: (l[0],) + (0,) * len(shape))
    return pl.pallas_call(
        functools.partial(_lru_kernel, tt=tt, nt=nt),
        grid_spec=pltpu.PrefetchScalarGridSpec(
            num_scalar_prefetch=1, grid=(s, nt),
            in_specs=[pl.BlockSpec((None, tt, W_BR), lambda b, i, l: (b, i, C_ULRU // W_BR)),
                      pl.BlockSpec((None, tt, W_BR), lambda b, i, l: (b, i, C_GLRU // W_BR)),
                      pl.BlockSpec((None, 3, W_BR), lambda b, i, l: (b, 0, 0)),
                      pl.BlockSpec((None, 1, W_BR), lambda b, i, l: (b, 0, 0)),
                      wspec((CONV_K, W_BR)), wspec((1, W_BR)),
                      wspec((LRU_BLOCKS, LRU_BS, LRU_BS)), wspec((1, W_BR)),
                      wspec((LRU_BLOCKS, LRU_BS, LRU_BS)), wspec((1, W_BR)), wspec((1, W_BR))],
            out_specs=[pl.BlockSpec((None, tt, W_BR), lambda b, i, l: (b, i, 0)),
                       pl.BlockSpec((None, 3, W_BR), lambda b, i, l: (b, 0, 0)),
                       pl.BlockSpec((None, 1, W_BR), lambda b, i, l: (b, 0, 0))],
            scratch_shapes=[pltpu.VMEM((tt + 8, W_BR), F32), pltpu.VMEM((tt, W_BR), F32),
                            pltpu.VMEM((tt, W_BR), F32), pltpu.VMEM((1, W_BR), F32)]),
        out_shape=[jax.ShapeDtypeStruct((s, t, W_BR), ydtype),
                   jax.ShapeDtypeStruct((s, 3, W_BR), F32),
                   jax.ShapeDtypeStruct((s, 1, W_BR), F32)],
        compiler_params=_cparams(("parallel", "arbitrary")),
        name="rglru",
    )(lidx, z3, z3, buf, h0, p["lru_conv_w"], p["lru_conv_b"], p["lru_wa"], p["lru_ba"],
      p["lru_wx"], p["lru_bx"], p["lru_lambda"])


S5_CB = LANES // S5_GS
S5_SW = S5_CB * S5_P


def _gelu_tanh(x):
    return 0.5 * x * (1.0 + jnp.tanh(0.7978845608028654 * (x + 0.044715 * (x * x * x))))


def _s5_kernel(l_ref, u_ref, h0re_ref, h0im_ref, wst_ref, vout_ref, kt_ref, alre_ref, alim_ref, d_ref,
               y_ref, hre_ref, him_ref, hin_scr, s_scr, *, n, sb):
    rows = sb * n
    us = [u_ref[pl.ds(j, rows, stride=S5_L), :] for j in range(S5_L)]
    ub = jnp.concatenate(us, axis=1).astype(BF16)
    s = _dot(ub, wst_ref[...])
    alre = alre_ref[...]
    alim = alim_ref[...]
    h0re = h0re_ref[...]
    h0im = h0im_ref[...]
    if n == 1:
        hin_scr[:, :S5_SW] = h0re
        hin_scr[:, S5_SW:] = h0im
        hre = alre * h0re - alim * h0im + s[:, :S5_SW]
        him = alre * h0im + alim * h0re + s[:, S5_SW:]
    else:
        s_scr[...] = s

        def body(c, carry):
            hre, him = carry
            hin_scr[pl.ds(c, 1), :S5_SW] = hre
            hin_scr[pl.ds(c, 1), S5_SW:] = him
            srow = s_scr[pl.ds(c, 1), :]
            return (alre * hre - alim * him + srow[:, :S5_SW],
                    alre * him + alim * hre + srow[:, S5_SW:])

        hre, him = lax.fori_loop(0, n, body, (h0re, h0im))
    hre_ref[...] = hre
    him_ref[...] = him
    ycat = _dot(hin_scr[...].astype(BF16), vout_ref[...]) + _dot(ub, kt_ref[...])
    d = d_ref[...]
    for j in range(S5_L):
        yj = ycat[:, j * LANES:(j + 1) * LANES] + d * us[j]
        y_ref[pl.ds(j, rows, stride=S5_L), :] = _gelu_tanh(yj)


def _s5_scan(lidx, z3, h0re, h0im, p, sb):
    s, t, _ = z3.shape
    n = t // S5_L
    assert sb == 1 or n == 1
    sg = s // sb
    zr = z3.reshape(sg, sb * t, NP)
    ncb = W_BR // LANES
    wspec = lambda shape: pl.BlockSpec((None, None) + shape, lambda b, c, l: (l[0], c) + (0,) * len(shape))
    hspec = pl.BlockSpec((None, sb, S5_SW), lambda b, c, l: (b, 0, c))
    y, hre, him = pl.pallas_call(
        functools.partial(_s5_kernel, n=n, sb=sb),
        grid_spec=pltpu.PrefetchScalarGridSpec(
            num_scalar_prefetch=1, grid=(sg, ncb),
            in_specs=[pl.BlockSpec((None, sb * t, LANES), lambda b, c, l: (b, 0, C_US5 // LANES + c)),
                      hspec, hspec,
                      wspec((S5_L * LANES, 2 * S5_SW)), wspec((2 * S5_SW, S5_L * LANES)),
                      wspec((S5_L * LANES, S5_L * LANES)), wspec((1, S5_SW)), wspec((1, S5_SW)),
                      wspec((1, LANES))],
            out_specs=[pl.BlockSpec((None, sb * t, LANES), lambda b, c, l: (b, 0, c)), hspec, hspec],
            scratch_shapes=[pltpu.VMEM((sb * n, 2 * S5_SW), F32), pltpu.VMEM((sb * n, 2 * S5_SW), F32)]),
        out_shape=[jax.ShapeDtypeStruct((sg, sb * t, W_BR), F32),
                   jax.ShapeDtypeStruct((sg, sb, S5_G * S5_P), F32),
                   jax.ShapeDtypeStruct((sg, sb, S5_G * S5_P), F32)],
        compiler_params=_cparams(("parallel", "arbitrary")),
        name="s5_scan",
    )(lidx, zr, h0re, h0im, p["s5_wst"], p["s5_vout"], p["s5_kt"], p["s5_alre"], p["s5_alim"], p["s5_d"])
    return y.reshape(s * t, W_BR), hre, him


def _s5_glu_kernel(l_ref, y_ref, g_ref, w_ref, o_ref):
    gl = _dot(y_ref[...].astype(BF16), w_ref[...])
    o_ref[...] = (gl[:, :W_BR] * _sigmoid(gl[:, W_BR:]) * _silu(g_ref[...])).astype(o_ref.dtype)


def _s5_glu(lidx, y2d, z2d, p, ydtype):
    n = y2d.shape[0]
    tm = min(n, 512)
    return pl.pallas_call(
        _s5_glu_kernel,
        grid_spec=pltpu.PrefetchScalarGridSpec(
            num_scalar_prefetch=1, grid=(n // tm,),
            in_specs=[pl.BlockSpec((tm, W_BR), lambda i, l: (i, 0)),
                      pl.BlockSpec((tm, W_BR), lambda i, l: (i, C_GS5 // W_BR)),
                      pl.BlockSpec((None, W_BR, 2 * W_BR), lambda i, l: (l[0], 0, 0))],
            out_specs=pl.BlockSpec((tm, W_BR), lambda i, l: (i, 0))),
        out_shape=jax.ShapeDtypeStruct((n, W_BR), ydtype),
        compiler_params=_cparams(("parallel",)),
        name="s5_glu",
    )(lidx, y2d, z2d, p["s5_glu_w"])


def _s5_expand_kernel(wst_ref, vout_ref, kt_ref, wst_o, vout_o, kt_o):
    n = S5_L * LANES
    lp = S5_P.bit_length() - 1
    lc = S5_GS.bit_length() - 1
    lg = S5_CB.bit_length() - 1
    row = _iota((n, n), 0)
    col = _iota((n, n), 1)
    src = _iota((LANES, n), 0)
    dst = _iota((LANES, n), 1)
    rep_state = jnp.where(src == (((dst >> (lp + lg)) << lp) | (dst & (S5_P - 1))), 1.0, 0.0).astype(BF16)
    rep_out = jnp.where(src == (((dst >> (lc + lg)) << lc) | (dst & (S5_GS - 1))), 1.0, 0.0).astype(BF16)
    gi_in_row = (row >> lc) & (S5_CB - 1)
    gi_state_row = (row >> lp) & (S5_CB - 1)
    gi_state_col = (col >> lp) & (S5_CB - 1)
    gi_out_col = (col >> lc) & (S5_CB - 1)
    wst_o[...] = jnp.where(gi_in_row == gi_state_col, _dot(wst_ref[...].astype(BF16), rep_state), 0.0).astype(BF16)
    vout_o[...] = jnp.where(gi_state_row == gi_out_col, _dot(vout_ref[...].astype(BF16), rep_out), 0.0).astype(BF16)
    kt_o[...] = jnp.where(gi_in_row == gi_out_col, _dot(kt_ref[...].astype(BF16), rep_out), 0.0).astype(BF16)


def _s5_expand(wst, vout, kt):
    dd, ncb, n, _ = wst.shape
    cspec = pl.BlockSpec((None, None, n, LANES), lambda d, c: (d, c, 0, 0))
    ospec = pl.BlockSpec((None, None, n, n), lambda d, c: (d, c, 0, 0))
    oshape = jax.ShapeDtypeStruct((dd, ncb, n, n), BF16)
    return pl.pallas_call(
        _s5_expand_kernel, grid=(dd, ncb), in_specs=[cspec, cspec, cspec], out_specs=[ospec, ospec, ospec],
        out_shape=[oshape, oshape, oshape], compiler_params=_cparams(("parallel", "parallel")),
        name="s5_expand",
    )(wst, vout, kt)


def _s5_weights(w):
    dt = jnp.exp(w["s5_log_dt"])[..., None]
    lr, li = w["s5_lam_re"], w["s5_lam_im"]
    mag = jnp.exp(lr * dt)
    a_re = mag * jnp.cos(li * dt)
    a_im = mag * jnp.sin(li * dt)
    den = lr * lr + li * li
    f_re = ((a_re - 1.0) * lr + a_im * li) / den
    f_im = (a_im * lr - (a_re - 1.0) * li) / den
    bb_re = f_re[..., None] * w["s5_b_re"] - f_im[..., None] * w["s5_b_im"]
    bb_im = f_re[..., None] * w["s5_b_im"] + f_im[..., None] * w["s5_b_re"]
    pw_re = [jnp.ones_like(a_re)]
    pw_im = [jnp.zeros_like(a_im)]
    for _ in range(S5_L):
        pr, pi = pw_re[-1], pw_im[-1]
        pw_re.append(pr * a_re - pi * a_im)
        pw_im.append(pr * a_im + pi * a_re)
    pw_re = jnp.stack(pw_re, axis=1)
    pw_im = jnp.stack(pw_im, axis=1)
    dd = lr.shape[0]
    ncb = S5_G // S5_CB
    grp = lambda a: a.reshape(a.shape[0], a.shape[1], ncb, S5_CB, *a.shape[3:])
    rev_re = jnp.stack([pw_re[:, S5_L - 1 - j] for j in range(S5_L)], axis=1)
    rev_im = jnp.stack([pw_im[:, S5_L - 1 - j] for j in range(S5_L)], axis=1)
    st_re = rev_re[..., None] * bb_re[:, None] - rev_im[..., None] * bb_im[:, None]
    st_im = rev_re[..., None] * bb_im[:, None] + rev_im[..., None] * bb_re[:, None]

    def state_w(a):
        return grp(a).transpose(0, 2, 1, 3, 5, 4).reshape(dd, ncb, S5_L * LANES, S5_P)

    wst = jnp.concatenate([state_w(st_re), state_w(st_im)], axis=-1)
    c_re, c_im = w["s5_c_re"], w["s5_c_im"]
    nx_re, nx_im = pw_re[:, 1:], pw_im[:, 1:]
    ca_re = c_re[:, None] * nx_re[:, :, :, None] - c_im[:, None] * nx_im[:, :, :, None]
    ca_im = c_re[:, None] * nx_im[:, :, :, None] + c_im[:, None] * nx_re[:, :, :, None]

    def out_w(a):
        return grp(a).transpose(0, 2, 3, 5, 1, 4).reshape(dd, ncb, S5_SW, S5_L * S5_GS)

    vout = jnp.concatenate([out_w(ca_re), out_w(-ca_im)], axis=2)
    cat_re = c_re[:, None] * pw_re[:, :S5_L, :, None] - c_im[:, None] * pw_im[:, :S5_L, :, None]
    cat_im = c_re[:, None] * pw_im[:, :S5_L, :, None] + c_im[:, None] * pw_re[:, :S5_L, :, None]
    kk = jnp.einsum("dlgcp,dgpe->dlgce", jnp.concatenate([cat_re, -cat_im], axis=-1),
                    jnp.concatenate([bb_re, bb_im], axis=2), precision=HI)
    zero = jnp.zeros_like(kk[:, 0])
    kt = jnp.stack([jnp.stack([kk[:, j - i] if j >= i else zero for j in range(S5_L)], axis=1)
                    for i in range(S5_L)], axis=1)
    kt = kt.reshape(dd, S5_L, S5_L, ncb, S5_CB, S5_GS, S5_GS)
    kt = kt.transpose(0, 3, 1, 4, 6, 2, 5).reshape(dd, ncb, S5_L * LANES, S5_L * S5_GS)
    wst, vout, kt = _s5_expand(wst, vout, kt)
    sw = lambda a: a.reshape(dd, ncb, 1, S5_SW)
    return dict(s5_wst=wst, s5_vout=vout, s5_kt=kt, s5_alre=sw(pw_re[:, S5_L]), s5_alim=sw(pw_im[:, S5_L]),
                s5_d=w["s5_d"].reshape(dd, W_BR // LANES, 1, LANES), s5_glu_w=w["s5_glu_w"].astype(BF16))


def _unit_lower_solve(ms, rhs, c):
    def split(a):
        hi = a.astype(BF16)
        return hi, (a - hi.astype(F32)).astype(BF16)

    def dot3(a, b):
        return _dot(a[0], b[0]) + _dot(a[1], b[0]) + _dot(a[0], b[1])

    n = range(len(ms))
    ms = [split(m) for m in ms]
    rs = [split(r) for r in rhs]
    xs = [rhs[i] - dot3(ms[i], rs[i]) for i in n]
    k = 2
    while k < c:
        if k <= 2 ** GDN_SPLIT_LEVELS:
            ms = [split(dot3(ms[i], ms[i])) for i in n]
            rs = [split(x) for x in xs]
            xs = [xs[i] + dot3(ms[i], rs[i]) for i in n]
        else:
            ms = [(_dot(ms[i][0], ms[i][0]).astype(BF16), None) for i in n]
            xs = [xs[i] + _dot(ms[i][0], xs[i].astype(BF16)) for i in n]
        k *= 2
    return xs


def _gdn_kernel(l_ref, q_ref, k_ref, v_ref, sm_ref, gg_ref, bq_ref, bk_ref, bv_ref, cwq_ref, cwk_ref, cwv_ref,
                alog_ref, dtb_ref, nw_ref, s0_ref, y_ref, bufo_ref, so_ref,
                xp_scr, qkv_scr, gx_scr, bx_scr, gc_scr, s_scr, *, tt, nt, c):
    ti = pl.program_id(1)
    log2c = c.bit_length() - 1

    @pl.when(ti == 0)
    def _():
        for i, b_ref in enumerate((bq_ref, bk_ref, bv_ref)):
            xp_scr[i, 5:8, :] = b_ref[...]
        s_scr[...] = s0_ref[...]

    for i, (x_ref, cw_ref) in enumerate(((q_ref, cwq_ref), (k_ref, cwk_ref), (v_ref, cwv_ref))):
        x = x_ref[...]
        xp_scr[i, 8:8 + tt, :] = x
        cw = cw_ref[...]
        cv = (cw[3:4] * x + cw[2:3] * xp_scr[i, 7:7 + tt, :]
              + cw[1:2] * xp_scr[i, 6:6 + tt, :] + cw[0:1] * xp_scr[i, 5:5 + tt, :])
        tail = x[tt - 3:tt, :]
        xp_scr[i, 5:8, :] = tail
        bufo_ref[:, i * W_BR:(i + 1) * W_BR] = tail
        cv = _silu(cv)
        if i < 2:
            scale = GDN_HD ** -0.5 if i == 0 else 1.0
            for h in range(GDN_H):
                sl = slice(h * GDN_HD, (h + 1) * GDN_HD)
                xh = cv[:, sl]
                qkv_scr[i, :, sl] = xh * (lax.rsqrt(jnp.sum(xh * xh, axis=-1, keepdims=True) + 1e-6) * scale)
        else:
            qkv_scr[i] = cv

    sm = sm_ref[...]
    gsm = -jnp.exp(alog_ref[...]) * _softplus(sm + dtb_ref[...])
    bsm = _sigmoid(sm)
    src = _iota((LANES, W_BR), 0)
    head = _iota((LANES, W_BR), 1) >> _log2(GDN_HD)
    gx_scr[...] = _dot_hi(gsm, jnp.where(src - SM_A == head, 1.0, 0.0))
    bx_scr[...] = _dot_hi(bsm, jnp.where(src - SM_B == head, 1.0, 0.0))
    headc = _iota((LANES, GDN_H * c), 1) >> log2c
    gc_scr[...] = _dot_hi(gsm, jnp.where(_iota((LANES, GDN_H * c), 0) - SM_A == headc, 1.0, 0.0))

    rowi = _iota((c, c), 0)
    coli = _iota((c, c), 1)
    incl = coli <= rowi
    strict = coli < rowi
    ltri = jnp.where(incl, 1.0, 0.0)
    upper = jnp.where(_iota((c, GDN_H * c), 0) > (_iota((c, GDN_H * c), 1) & (c - 1)), 1.0, 0.0)
    nw = nw_ref[...]

    def chunk(ci, carry):
        r0 = pl.multiple_of(ci * c, c)
        gcb_all = _dot_hi(ltri, gx_scr[pl.ds(r0, c), :])
        diffs = _dot_hi(ltri, gc_scr[pl.ds(r0, c), :] * upper)
        beta_all = bx_scr[pl.ds(r0, c), :]
        hs = range(GDN_H)
        sls = [slice(h * GDN_HD, (h + 1) * GDN_HD) for h in hs]
        q = [qkv_scr[0, pl.ds(r0, c), sl] for sl in sls]
        k = [qkv_scr[1, pl.ds(r0, c), sl] for sl in sls]
        gcb = [gcb_all[:, sl] for sl in sls]
        decay = [jnp.where(incl, jnp.exp(diffs[:, h * c:(h + 1) * c]), 0.0) for h in hs]
        kb = [k[h] * beta_all[:, sls[h]] for h in hs]
        kbf = [x.astype(BF16) for x in k]
        kq = [_dot_nt(jnp.concatenate([kb[h], q[h]], axis=0).astype(BF16), kbf[h]) for h in hs]
        m = [jnp.where(strict, kq[h][:c] * decay[h], 0.0) for h in hs]
        qk = [(kq[h][c:] * decay[h]).astype(BF16) for h in hs]
        rhs = [jnp.concatenate([qkv_scr[2, pl.ds(r0, c), sls[h]] * beta_all[:, sls[h]], kb[h] * jnp.exp(gcb[h])],
                               axis=1) for h in hs]
        sol = _unit_lower_solve(m, rhs, c)
        s = [s_scr[h] for h in hs]
        sb = [x.astype(BF16) for x in s]
        ws = [_dot(jnp.concatenate([sol[h][:, GDN_HD:], q[h] * jnp.exp(gcb[h])], axis=0).astype(BF16), sb[h])
              for h in hs]
        vnb = [(sol[h][:, :GDN_HD] - ws[h][:c]).astype(BF16) for h in hs]
        o = [ws[h][c:] + _dot(qk[h], vnb[h]) for h in hs]
        for h in hs:
            glast = gcb[h][c - 1:c, :]
            s_scr[h] = s[h] * jnp.exp(glast) + _dot_tn((k[h] * jnp.exp(glast - gcb[h])).astype(BF16), vnb[h])
        for h in hs:
            on = o[h] * lax.rsqrt(jnp.mean(o[h] * o[h], axis=-1, keepdims=True) + 1e-6) * nw
            y_ref[pl.ds(r0, c), sls[h]] = (on * _silu(gg_ref[pl.ds(r0, c), sls[h]])).astype(y_ref.dtype)
        return carry

    lax.fori_loop(0, tt // c, chunk, 0)

    @pl.when(ti == nt - 1)
    def _():
        so_ref[...] = s_scr[...]


def _gdn(lidx, z3, buf, s0, p, ydtype, chunk):
    s, t, _ = z3.shape
    tt = min(t, 256)
    nt = t // tt
    c = min(chunk, tt)
    zspec = lambda col: pl.BlockSpec((None, tt, W_BR), lambda b, i, l: (b, i, col // W_BR))
    bspec = lambda j: pl.BlockSpec((None, 3, W_BR), lambda b, i, l: (b, 0, j))
    cspec = lambda j: pl.BlockSpec((None, CONV_K, W_BR), lambda b, i, l: (l[0], 0, j))
    rspec = pl.BlockSpec((None, 1, LANES), lambda b, i, l: (l[0], 0, 0))
    sspec = pl.BlockSpec((None, GDN_H, GDN_HD, GDN_HD), lambda b, i, l: (b, 0, 0, 0))
    return pl.pallas_call(
        functools.partial(_gdn_kernel, tt=tt, nt=nt, c=c),
        grid_spec=pltpu.PrefetchScalarGridSpec(
            num_scalar_prefetch=1, grid=(s, nt),
            in_specs=[zspec(C_QKV), zspec(C_QKV + W_BR), zspec(C_QKV + 2 * W_BR),
                      pl.BlockSpec((None, tt, LANES), lambda b, i, l: (b, i, C_SMALL_B // LANES)),
                      zspec(C_GGDN), bspec(0), bspec(1), bspec(2), cspec(0), cspec(1), cspec(2),
                      rspec, rspec, rspec, sspec],
            out_specs=[pl.BlockSpec((None, tt, W_BR), lambda b, i, l: (b, i, 0)),
                       pl.BlockSpec((None, 3, 3 * W_BR), lambda b, i, l: (b, 0, 0)),
                       sspec],
            scratch_shapes=[pltpu.VMEM((3, tt + 8, W_BR), F32), pltpu.VMEM((3, tt, W_BR), F32),
                            pltpu.VMEM((tt, W_BR), F32), pltpu.VMEM((tt, W_BR), F32),
                            pltpu.VMEM((tt, GDN_H * c), F32), pltpu.VMEM((GDN_H, GDN_HD, GDN_HD), F32)]),
        out_shape=[jax.ShapeDtypeStruct((s, t, W_BR), ydtype),
                   jax.ShapeDtypeStruct((s, 3, 3 * W_BR), F32),
                   jax.ShapeDtypeStruct((s, GDN_H, GDN_HD, GDN_HD), F32)],
        compiler_params=_cparams(("parallel", "arbitrary")),
        name="gdn",
    )(lidx, z3, z3, z3, z3, z3, buf, buf, buf, p["gdn_conv_w"], p["gdn_conv_w"], p["gdn_conv_w"],
      p["gdn_a_log"], p["gdn_dt_bias"], p["gdn_norm_w"], s0)


PAGES_PER_STEP = 16
MAX_TAIL_STEPS = 3


KEY_TILE = 512


def _key_tile(tk):
    return max(k for k in range(KEY_TILE, min(tk, LANES * SEL_BLOCK) + 1, KEY_TILE) if tk % k == 0)


def _gather_kernel(l_ref, pt_ref, *refs, n_past):
    pages = refs[:PAGES_PER_STEP]
    zn_ref, cmp_ref, sel_ref = refs[PAGES_PER_STEP:]
    i = pl.program_id(1)
    half = 2 * NSA_KVH * NSA_HD

    @pl.when(i < n_past)
    def _():
        per_tok = 4 * NSA_KVH
        for k, p_ref in enumerate(pages):
            rows = slice(k * PAGE_SIZE, (k + 1) * PAGE_SIZE)
            for r in (2, 3):
                for g in range(NSA_KVH):
                    x = p_ref[pl.ds(r * NSA_KVH + g, PAGE_SIZE, stride=per_tok), :]
                    cols = slice(((r - 2) * NSA_KVH + g) * NSA_HD, ((r - 2) * NSA_KVH + g + 1) * NSA_HD)
                    sel_ref[rows, cols] = x.astype(BF16)
        halves = PAGE_SIZE // CMP_STRIDE
        for r in (0, 1):
            for g in range(NSA_KVH):
                flat = [jnp.concatenate([p_ref[pl.ds(s * per_tok + r * NSA_KVH + g, halves,
                                                     stride=CMP_STRIDE * per_tok), :] for p_ref in pages], axis=0)
                        for s in range(CMP_STRIDE)]
                cmp_ref[r, g] = jnp.concatenate(flat, axis=1).astype(BF16)

    @pl.when(i >= n_past)
    def _():
        sel_ref[...] = jnp.zeros(sel_ref.shape, BF16)

    @pl.when(i == n_past)
    def _():
        new = zn_ref[:, half:2 * half]
        sel_ref[0:2 * SUBLANES, :] = jnp.concatenate([new, jnp.zeros_like(new)], axis=0).astype(BF16)


def _gather_ctx(lidx, pt_flat, cache, z3):
    b, t_new, _ = z3.shape
    assert t_new == SUBLANES
    n_pages = pt_flat.shape[0] // b
    assert n_pages % PAGES_PER_STEP == 0
    n_past = n_pages // PAGES_PER_STEP
    rows = PAGES_PER_STEP * PAGE_SIZE
    n_tail = min(range(1, MAX_TAIL_STEPS + 1), key=lambda k: ((n_past + k) * rows // _key_tile((n_past + k) * rows), k))
    width = 4 * NSA_KVH * NSA_HD
    half = width // 2
    n_pool = cache.shape[1]
    cache = cache.reshape(cache.shape[0] * n_pool, PAGE_SIZE * 4 * NSA_KVH, NSA_HD)

    def pspec(k):
        def imap(bi, i, l, pt):
            page = jnp.minimum(i * PAGES_PER_STEP + k, n_pages - 1)
            return (l[0] * n_pool + pt[bi * n_pages + page], 0, 0)
        return pl.BlockSpec((None, PAGE_SIZE * 4 * NSA_KVH, NSA_HD), imap)

    return pl.pallas_call(
        functools.partial(_gather_kernel, n_past=n_past),
        grid_spec=pltpu.PrefetchScalarGridSpec(
            num_scalar_prefetch=2, grid=(b, n_past + n_tail),
            in_specs=[pspec(k) for k in range(PAGES_PER_STEP)]
            + [pl.BlockSpec((None, t_new, width), lambda bi, i, l, pt: (bi, 0, C_KV // width))],
            out_specs=[pl.BlockSpec((None, 2, NSA_KVH, rows // CMP_STRIDE, CMP_STRIDE * NSA_HD),
                                    lambda bi, i, l, pt: (bi, 0, 0, jnp.minimum(i, n_past - 1), 0)),
                       pl.BlockSpec((None, rows, half), lambda bi, i, l, pt: (bi, i, 0))]),
        out_shape=[jax.ShapeDtypeStruct((b, 2, NSA_KVH, n_past * rows // CMP_STRIDE, CMP_STRIDE * NSA_HD), BF16),
                   jax.ShapeDtypeStruct((b, (n_past + n_tail) * rows, half), BF16)],
        compiler_params=_cparams(("parallel", "arbitrary")),
        name="nsa_gather",
    )(lidx, pt_flat, *([cache] * PAGES_PER_STEP), z3)


def _kv_rows_kernel(z_ref, o_ref, *, tt):
    per_tok = 4 * NSA_KVH
    for rg in range(per_tok):
        o_ref[pl.ds(rg, tt, stride=per_tok), :] = z_ref[:, rg * NSA_HD:(rg + 1) * NSA_HD]


def _kv_rows(z3):
    s, t, _ = z3.shape
    tt = min(t, 512)
    per_tok = 4 * NSA_KVH
    return pl.pallas_call(
        functools.partial(_kv_rows_kernel, tt=tt), grid=(s, t // tt),
        in_specs=[pl.BlockSpec((None, tt, per_tok * NSA_HD), lambda b, i: (b, i, C_KV // (per_tok * NSA_HD)))],
        out_specs=pl.BlockSpec((None, tt * per_tok, NSA_HD), lambda b, i: (b, i, 0)),
        out_shape=jax.ShapeDtypeStruct((s, t * per_tok, NSA_HD), F32),
        compiler_params=_cparams(("parallel", "parallel")),
        name="nsa_kv_rows",
    )(z3)


def _cmp_bias_kernel(pe_ref, w1_ref, o_ref):
    o_ref[...] = _dot_hi(pe_ref[...], w1_ref[...].astype(F32))


def _cmp_bias(pe8, w1):
    dd, _, flat, wide = w1.shape
    return pl.pallas_call(
        _cmp_bias_kernel, grid=(dd, 2),
        in_specs=[pl.BlockSpec((None, SUBLANES, flat), lambda d, c: (d, 0, 0)),
                  pl.BlockSpec((None, None, flat, wide), lambda d, c: (d, c, 0, 0))],
        out_specs=pl.BlockSpec((None, None, SUBLANES, wide), lambda d, c: (d, c, 0, 0)),
        out_shape=jax.ShapeDtypeStruct((dd, 2, SUBLANES, wide), F32),
        compiler_params=_cparams(("parallel", "parallel")),
        name="nsa_cmp_bias",
    )(pe8, w1)


def _cmp_kernel(l_ref, x0_ref, x1_ref, w1_ref, bias_ref, w2_ref, o_ref, carry_scr, *, nh, flat):
    @pl.when(pl.program_id(2) == 0)
    def _():
        carry_scr[...] = jnp.zeros(carry_scr.shape, F32)

    w1 = w1_ref[...]
    bias = bias_ref[...]
    last = _iota((nh, NSA_HD), 0) == nh - 1
    for g, x_ref in enumerate((x0_ref, x1_ref)):
        if flat:
            ucat = x_ref[...]
        else:
            ucat = jnp.concatenate([x_ref[pl.ds(s, nh, stride=CMP_STRIDE), :] for s in range(CMP_STRIDE)], axis=1)
        hh = _dot(ucat.astype(BF16), w1)
        hf = hh[:, :NSA_HD] + bias[0:1, :NSA_HD]
        hs = hh[:, NSA_HD:] + bias[1:2, NSA_HD:]
        hs_next = jnp.where(last, carry_scr[g], pltpu.roll(hs, nh - 1, axis=0))
        carry_scr[g] = hs[0:1, :]
        hid = _silu(hf + hs_next)
        o_ref[g] = _dot(hid.astype(BF16), w2_ref[...]).astype(BF16)


def _compress(lidx, src, col0, n_rows, p):
    b = src.shape[0]
    tr = min(n_rows, 4096)
    nt = n_rows // tr
    nh = tr // CMP_STRIDE
    flat = col0 is None
    if flat:
        xspec = lambda g: pl.BlockSpec((None, None, None, nh, CMP_STRIDE * NSA_HD),
                                       lambda bi, c, i, l: (bi, c, g, nt - 1 - i, 0))
    else:
        xspec = lambda g: pl.BlockSpec((None, tr, NSA_HD),
                                       lambda bi, c, i, l: (bi, nt - 1 - i, col0 + NSA_KVH * c + g))
    return pl.pallas_call(
        functools.partial(_cmp_kernel, nh=nh, flat=flat),
        grid_spec=pltpu.PrefetchScalarGridSpec(
            num_scalar_prefetch=1, grid=(b, 2, nt),
            in_specs=[xspec(0), xspec(1),
                      pl.BlockSpec((None, None, CMP_STRIDE * NSA_HD, 2 * NSA_HD), lambda bi, c, i, l: (l[0], c, 0, 0)),
                      pl.BlockSpec((None, None, SUBLANES, 2 * NSA_HD), lambda bi, c, i, l: (l[0], c, 0, 0)),
                      pl.BlockSpec((None, None, NSA_HD, NSA_HD), lambda bi, c, i, l: (l[0], c, 0, 0))],
            out_specs=pl.BlockSpec((None, None, NSA_KVH, nh, NSA_HD), lambda bi, c, i, l: (bi, c, 0, nt - 1 - i, 0)),
            scratch_shapes=[pltpu.VMEM((NSA_KVH, 1, NSA_HD), F32)]),
        out_shape=jax.ShapeDtypeStruct((b, 2, NSA_KVH, n_rows // CMP_STRIDE, NSA_HD), BF16),
        compiler_params=_cparams(("parallel", "parallel", "arbitrary")),
        name="nsa_compress",
    )(lidx, src, src, p["nsa_w1"], p["nsa_bias"], p["nsa_w2"])


def _masked_softmax(s, valid):
    s = jnp.where(valid, s, NEG)
    m = jnp.max(s, axis=-1, keepdims=True)
    e = jnp.where(valid, jnp.exp2(s - m), 0.0)
    den = jnp.sum(e, axis=-1, keepdims=True)
    return e / jnp.where(den > 0.0, den, 1.0)


def _attn_kernel(l_ref, q_ref, sm_ref, gn_ref, kc_ref, vc_ref, ks_ref, vs_ref, kw_ref, vw_ref, y_ref, *,
                 qb, pos0, ncp, ns, nsp, kt, wn, wpos0, tw):
    g = pl.program_id(1)
    q0 = pl.program_id(2) * qb
    qpos0 = pos0 + q0
    hd = NSA_HD
    q = q_ref[...] * (hd ** -0.5 * LOG2E)
    qr = jnp.concatenate([q[:, j * hd:(j + 1) * hd] for j in range(NSA_GQ)], axis=0).astype(BF16)
    slope_g = jnp.where(g == 0, LOG2E, LOG2E * 2.0 ** -NSA_GQ)
    slopes = [slope_g * 2.0 ** -(j + 1) for j in range(NSA_GQ)]
    heads = lambda a: [a[j * qb:(j + 1) * qb] for j in range(NSA_GQ)]

    dist = (qpos0 + _iota((qb, ncp), 0)) - (_iota((qb, ncp), 1) * CMP_STRIDE + (CMP_BLOCK - 1))
    valid = dist >= 0
    distf = dist.astype(F32)
    sc = heads(_dot_nt(qr, kc_ref[...]))
    pc = [_masked_softmax(sc[j] - slopes[j] * distf, valid) for j in range(NSA_GQ)]
    o_c = heads(_dot(jnp.concatenate(pc, axis=0).astype(BF16), vc_ref[...]))
    imp = pc[0] + pc[1] + pc[2] + pc[3]
    pool = jnp.where((_iota((ncp, nsp), 0) >> _log2(CMP_PER_SEL)) == _iota((ncp, nsp), 1), 1.0, 0.0)
    imp = _dot_hi(imp, pool)

    blk = _iota((qb, nsp), 1)
    qp = qpos0 + _iota((qb, nsp), 0)
    val = jnp.where(blk * SEL_BLOCK > qp, -FORCE, imp)
    val = jnp.where(blk == (qp >> _log2(SEL_BLOCK)), FORCE, jnp.where(blk == 0, FORCE, val))
    val = jnp.where(blk >= ns, -3.0 * FORCE, val)
    top = float(min(TOP_N, ns))
    if qb == LANES and nsp == LANES:
        nsr = -(-ns // SUBLANES) * SUBLANES
        val_t = val.T[:nsr]
        blk_t = _iota((nsr, qb), 0)
        rank = jnp.zeros((nsr, qb), F32)
        for bidx in range(ns):
            cand = val_t[bidx:bidx + 1, :]
            rank = rank + jnp.where(cand > val_t, 1.0, jnp.where(cand == val_t, jnp.where(blk_t > bidx, 1.0, 0.0), 0.0))
        sel_t = jnp.where(rank < top, 1.0, 0.0)
        selb = jnp.concatenate([sel_t, jnp.zeros((nsp - nsr, qb), F32)], axis=0).T.astype(BF16)
    else:
        rank = jnp.zeros((qb, nsp), F32)
        for bidx in range(ns):
            cand = val[:, bidx:bidx + 1]
            rank = rank + jnp.where(cand > val, 1.0, jnp.where(cand == val, jnp.where(blk > bidx, 1.0, 0.0), 0.0))
        selb = jnp.where(rank < top, 1.0, 0.0).astype(BF16)

    n_tiles = (qpos0 + qb + kt - 1) // kt
    qh = heads(qr)
    spread = jnp.where(_iota((LANES, kt), 0) == (_iota((LANES, kt), 1) >> _log2(SEL_BLOCK)), 1.0, 0.0).astype(BF16)

    def tile_bias(t):
        k0 = t * kt
        d = (qpos0 + _iota((qb, kt), 0)) - (k0 + _iota((qb, kt), 1))
        pick_blk = jnp.where(_iota((nsp, LANES), 0) == (k0 >> _log2(SEL_BLOCK)) + _iota((nsp, LANES), 1),
                             1.0, 0.0).astype(BF16)
        sel_tile = _dot(selb, pick_blk).astype(BF16)
        ok = jnp.where(d >= 0, _dot(sel_tile, spread), 0.0) > 0.5
        return jnp.where(ok, d.astype(F32), -NEG)

    def tile(t, carry):
        k0 = pl.multiple_of(t * kt, kt)
        kk = ks_ref[pl.ds(k0, kt), :].astype(BF16)
        vv = vs_ref[pl.ds(k0, kt), :].astype(BF16)
        per_head = qb == Q_BLOCK
        s = [_dot_nt(qh[j], kk) for j in range(NSA_GQ)] if per_head else heads(_dot_nt(qr, kk))
        bias = carry[NSA_GQ]
        bias_next = tile_bias(t + 1)
        stats, prs = [], []
        for j in range(NSA_GQ):
            m, lsum, _ = carry[j]
            sm = s[j] - slopes[j] * bias
            m_new = jnp.maximum(m, jnp.max(sm, axis=-1, keepdims=True))
            pr = jnp.exp2(sm - m_new)
            alpha = jnp.exp2(m - m_new)
            stats.append((m_new, alpha * lsum + jnp.sum(pr, axis=-1, keepdims=True), alpha))
            prs.append(pr.astype(BF16))
        pv = [_dot(pr, vv) for pr in prs] if per_head else heads(_dot(jnp.concatenate(prs, axis=0), vv))
        new = [(stats[j][0], stats[j][1], stats[j][2] * carry[j][2] + pv[j]) for j in range(NSA_GQ)]
        return tuple(new) + (bias_next,)

    init = (jnp.full((qb, 1), NEG, F32), jnp.zeros((qb, 1), F32), jnp.zeros((qb, hd), F32))
    fin = lax.fori_loop(0, n_tiles, tile, (init,) * NSA_GQ + (tile_bias(0),))
    o_s = [acc / lsum for _, lsum, acc in fin[:NSA_GQ]]

    k0w = pl.multiple_of(jnp.clip(q0 - WINDOW, 0, tw - wn), SUBLANES)
    dw = (qpos0 + _iota((qb, wn), 0)) - (wpos0 + k0w + _iota((qb, wn), 1))
    okw = jnp.abs(2 * dw - (WINDOW - 1)) < WINDOW
    dwf = dw.astype(F32)
    sw = heads(_dot_nt(qr, kw_ref[pl.ds(k0w, wn), :].astype(BF16)))
    pw = [_masked_softmax(sw[j] - slopes[j] * dwf, okw) for j in range(NSA_GQ)]
    o_w = heads(_dot(jnp.concatenate(pw, axis=0).astype(BF16), vw_ref[pl.ds(k0w, wn), :].astype(BF16)))

    gate = _sigmoid(sm_ref[...])
    lane = _iota((qb, LANES), 1)
    pick = lambda idx: jnp.sum(jnp.where(lane == idx, gate, 0.0), axis=-1, keepdims=True)
    for j in range(NSA_GQ):
        base = SM_GL + 3 * (g * NSA_GQ + j)
        o = pick(base) * o_c[j] + pick(base + 1) * o_s[j] + pick(base + 2) * o_w[j]
        y_ref[:, j * hd:(j + 1) * hd] = (o * _silu(gn_ref[:, j * hd:(j + 1) * hd])).astype(y_ref.dtype)


def _attention(lidx, z3, cmp_kv, ks_src, ks_col, vs_col, kw_src, kw_col, vw_col, pos0, wpos0, ydtype):
    b, t, _ = z3.shape
    qb = min(Q_BLOCK, t)
    ncp = cmp_kv.shape[3]
    tk = ks_src.shape[1]
    tw = kw_src.shape[1]
    if qb == Q_BLOCK:
        kt = KEY_TILE
    else:
        kt = _key_tile(tk)
    ns = -(-(pos0 + t) // SEL_BLOCK)
    nsp = -(-ns // LANES) * LANES
    wn = WINDOW + Q_BLOCK
    assert tk % kt == 0 and tk >= pos0 + t and tw >= wn and ncp // CMP_PER_SEL <= nsp
    hw = NSA_GQ * NSA_HD
    kern = functools.partial(_attn_kernel, qb=qb, pos0=pos0, ncp=ncp, ns=ns, nsp=nsp, kt=kt, wn=wn, wpos0=wpos0, tw=tw)
    kvspec = lambda rows, col: pl.BlockSpec((None, rows, NSA_HD), lambda bi, g, i, l: (bi, 0, col + g))
    cspec = lambda c: pl.BlockSpec((None, None, None, ncp, NSA_HD), lambda bi, g, i, l: (bi, c, g, 0, 0))
    return pl.pallas_call(
        kern,
        grid_spec=pltpu.PrefetchScalarGridSpec(
            num_scalar_prefetch=1, grid=(b, NSA_KVH, t // qb),
            in_specs=[pl.BlockSpec((None, qb, hw), lambda bi, g, i, l: (bi, i, C_QNSA // hw + g)),
                      pl.BlockSpec((None, qb, LANES), lambda bi, g, i, l: (bi, i, C_SMALL_A // LANES)),
                      pl.BlockSpec((None, qb, hw), lambda bi, g, i, l: (bi, i, C_GNSA // hw + g)),
                      cspec(0), cspec(1),
                      kvspec(tk, ks_col), kvspec(tk, vs_col), kvspec(tw, kw_col), kvspec(tw, vw_col)],
            out_specs=pl.BlockSpec((None, qb, hw), lambda bi, g, i, l: (bi, i, g))),
        out_shape=jax.ShapeDtypeStruct((b, t, W_BR), ydtype),
        compiler_params=_cparams(("parallel", "parallel", "arbitrary")),
        name="nsa_attention",
    )(lidx, z3, z3, z3, cmp_kv, cmp_kv, ks_src, ks_src, kw_src, kw_src)


def _merge_kernel(l_ref, y0_ref, y1_ref, y2_ref, y3_ref, g0_ref, g1_ref, g2_ref, g3_ref, w_ref, o_ref):
    acc = None
    for m, (y_ref, g_ref) in enumerate(((y0_ref, g0_ref), (y1_ref, g1_ref), (y2_ref, g2_ref), (y3_ref, g3_ref))):
        term = _sigmoid(g_ref[...]) * _dot(y_ref[...].astype(BF16), w_ref[m])
        acc = term if acc is None else acc + term
    o_ref[...] = acc.astype(o_ref.dtype)


def _merge(lidx, ys, z2d, p):
    n = z2d.shape[0]
    tm = min(n, 256)
    yspec = pl.BlockSpec((tm, W_BR), lambda i, l: (i, 0))
    gspec = lambda m: pl.BlockSpec((tm, D_MODEL), lambda i, l: (i, C_MERGE // D_MODEL + m))
    wspec = pl.BlockSpec((None, N_BRANCH, W_BR, D_MODEL), lambda i, l: (l[0], 0, 0, 0), pipeline_mode=pl.Buffered(1))
    return pl.pallas_call(
        _merge_kernel,
        grid_spec=pltpu.PrefetchScalarGridSpec(
            num_scalar_prefetch=1, grid=(n // tm,),
            in_specs=[yspec] * N_BRANCH + [gspec(m) for m in range(N_BRANCH)] + [wspec],
            out_specs=pl.BlockSpec((tm, D_MODEL), lambda i, l: (i, 0))),
        out_shape=jax.ShapeDtypeStruct((n, D_MODEL), BF16),
        compiler_params=_cparams(("parallel",)),
        name="merge",
    )(lidx, *ys, z2d, z2d, z2d, z2d, p["w_branch"])


def _outproj_kernel(l_ref, m_ref, x_ref, w_ref, g_ref, b_ref, o_ref, *, alpha):
    v = alpha * x_ref[...] + _dot(m_ref[...], w_ref[...])
    mu = jnp.mean(v, axis=-1, keepdims=True)
    c = v - mu
    var = jnp.mean(c * c, axis=-1, keepdims=True)
    o_ref[...] = c * lax.rsqrt(var + 1e-5) * g_ref[...] + b_ref[...]


def _outproj(lidx, merged, x2d, p, alpha):
    n = x2d.shape[0]
    tm = min(n, 256)
    return pl.pallas_call(
        functools.partial(_outproj_kernel, alpha=alpha),
        grid_spec=pltpu.PrefetchScalarGridSpec(
            num_scalar_prefetch=1, grid=(n // tm,),
            in_specs=[pl.BlockSpec((tm, D_MODEL), lambda i, l: (i, 0)),
                      pl.BlockSpec((tm, D_MODEL), lambda i, l: (i, 0)),
                      pl.BlockSpec((None, D_MODEL, D_MODEL), lambda i, l: (l[0], 0, 0)),
                      pl.BlockSpec((None, 1, D_MODEL), lambda i, l: (l[0], 0, 0)),
                      pl.BlockSpec((None, 1, D_MODEL), lambda i, l: (l[0], 0, 0))],
            out_specs=pl.BlockSpec((tm, D_MODEL), lambda i, l: (i, 0))),
        out_shape=jax.ShapeDtypeStruct((n, D_MODEL), F32),
        compiler_params=_cparams(("parallel",)),
        input_output_aliases={2: 0},
        name="outproj_ln",
    )(lidx, merged, x2d, p["w_out"], p["ln_g"], p["ln_b"])


def _prep_params(w):
    p = {}
    row = lambda a: a[:, None, :]
    p["lru_conv_w"] = w["lru_conv_w"]
    p["lru_conv_b"] = row(w["lru_conv_b"])
    p["lru_wa"] = w["lru_wa"].astype(BF16)
    p["lru_ba"] = row(w["lru_ba"])
    p["lru_wx"] = w["lru_wx"].astype(BF16)
    p["lru_bx"] = row(w["lru_bx"])
    p["lru_lambda"] = row(w["lru_lambda"])
    p.update(_s5_weights(w))
    p["w_branch"] = w["w_branch"].astype(BF16)
    p["w_out"] = w["w_out"].astype(BF16)
    p["ln_g"] = row(w["ln_g"])
    p["ln_b"] = row(w["ln_b"])
    dd = w["gdn_a_log"].shape[0]
    lane_row = lambda a, off: jnp.zeros((dd, 1, LANES), F32).at[:, 0, off:off + a.shape[-1]].set(a)
    p["gdn_conv_w"] = w["gdn_conv_w"]
    p["gdn_a_log"] = lane_row(w["gdn_a_log"], SM_A)
    p["gdn_dt_bias"] = lane_row(w["gdn_dt_bias"], SM_A)
    p["gdn_norm_w"] = row(w["gdn_norm_w"])
    flat = CMP_STRIDE * NSA_HD
    w1 = w["nsa_cmp_w1"].reshape(dd, 2, 2, flat, NSA_HD).transpose(0, 1, 3, 2, 4)
    p["nsa_w1"] = w1.reshape(dd, 2, flat, 2 * NSA_HD).astype(BF16)
    pe = w["nsa_cmp_pos"].reshape(dd, 2, flat)
    p["nsa_bias"] = _cmp_bias(jnp.concatenate([pe, jnp.zeros((dd, SUBLANES - 2, flat), F32)], axis=1), p["nsa_w1"])
    p["nsa_w2"] = w["nsa_cmp_w2"].astype(BF16)
    return p


def _mixer_layer(lidx, x2d, s, t, state, nsa_branch, p, win, alpha, ydtype):
    lru_buf, lru_h, gdn_buf, gdn_s, s5_re, s5_im = state
    z = _inproj(lidx, win[0], x2d, win[1])
    z3 = z.reshape(s, t, NP)
    y_lru, lru_buf, lru_h = _lru(lidx, z3, lru_buf, lru_h, p, ydtype)
    y_nsa = nsa_branch(z3)
    y_gdn, gdn_buf, gdn_s = _gdn(lidx, z3, gdn_buf, gdn_s, p, ydtype, GDN_CHUNK)
    y_s5, s5_re, s5_im = _s5_scan(lidx, z3, s5_re, s5_im, p, s5_re.shape[1])
    y_s5 = _s5_glu(lidx, y_s5, z, p, ydtype)
    flat = lambda y: y.reshape(s * t, W_BR)
    merged = _merge(lidx, (flat(y_lru), flat(y_nsa), flat(y_gdn), y_s5), z, p)
    x_new = _outproj(lidx, merged, x2d, p, alpha)
    return x_new, z3, (lru_buf, lru_h, gdn_buf, gdn_s, s5_re, s5_im)


def kernel(x_prompt, x_sample, state_lru_h, state_lru_conv, cache_nsa_kv, cache_win_kv, state_gdn_s, state_gdn_conv, state_s5_re, state_s5_im, page_table, w_in, lru_conv_w, lru_conv_b, lru_wa, lru_ba, lru_wx, lru_bx, lru_lambda, nsa_cmp_pos, nsa_cmp_w1, nsa_cmp_w2, gdn_conv_w, gdn_a_log, gdn_dt_bias, gdn_norm_w, s5_lam_re, s5_lam_im, s5_log_dt, s5_b_re, s5_b_im, s5_c_re, s5_c_im, s5_d, s5_glu_w, w_branch, w_out, ln_g, ln_b):
    depth = w_in.shape[0]
    bp, tp, _ = x_prompt.shape
    db, ts, _ = x_sample.shape
    n_pages = page_table.shape[1]
    past = n_pages * PAGE_SIZE
    wbuf = cache_win_kv.shape[2]
    alpha = (2.0 * depth) ** 0.25
    kvw = 4 * NSA_KVH * NSA_HD
    winw = 2 * NSA_KVH * NSA_HD
    p = _prep_params(dict(
        lru_conv_w=lru_conv_w, lru_conv_b=lru_conv_b, lru_wa=lru_wa, lru_ba=lru_ba, lru_wx=lru_wx, lru_bx=lru_bx,
        lru_lambda=lru_lambda, nsa_cmp_pos=nsa_cmp_pos, nsa_cmp_w1=nsa_cmp_w1, nsa_cmp_w2=nsa_cmp_w2,
        gdn_conv_w=gdn_conv_w, gdn_a_log=gdn_a_log, gdn_dt_bias=gdn_dt_bias, gdn_norm_w=gdn_norm_w,
        s5_lam_re=s5_lam_re, s5_lam_im=s5_lam_im, s5_log_dt=s5_log_dt, s5_b_re=s5_b_re, s5_b_im=s5_b_im,
        s5_c_re=s5_c_re, s5_c_im=s5_c_im, s5_d=s5_d, s5_glu_w=s5_glu_w, w_branch=w_branch, w_out=w_out,
        ln_g=ln_g, ln_b=ln_b))
    assert w_in.shape[2] == sum(s[1] for s in IN_SEGMENTS)
    wp = (jnp.asarray(_tile_sources(), jnp.int32), jnp.swapaxes(w_in, 1, 2))
    p = lax.optimization_barrier(p)
    pt_flat = page_table.reshape(-1).astype(jnp.int32)
    sb = db if ts == S5_L else 1
    c0 = C_KV // LANES
    zero_state = (jnp.zeros((bp, CONV_K - 1, W_BR), F32), jnp.zeros((bp, 1, W_BR), F32),
                  jnp.zeros((bp, CONV_K - 1, 3 * W_BR), F32), jnp.zeros((bp, GDN_H, GDN_HD, GDN_HD), F32),
                  jnp.zeros((bp, 1, S5_G * S5_P), F32), jnp.zeros((bp, 1, S5_G * S5_P), F32))

    def layer(carry, l):
        xp, xs = carry
        lidx = l.reshape(1)
        at = lambda a: lax.dynamic_index_in_dim(a, l, 0, keepdims=False)

        def nsa_prompt(z3):
            ckv = _compress(lidx, z3, c0, tp, p)
            return _attention(lidx, z3, ckv, z3, c0 + 4, c0 + 6, z3, c0 + 8, c0 + 10, 0, 0, BF16)

        xp, zp3, st_p = _mixer_layer(lidx, xp, bp, tp, zero_state, nsa_prompt, p, wp, alpha, BF16)

        win_state = {}

        def nsa_sample(z3):
            cmp_rows, sel_rows = _gather_ctx(lidx, pt_flat, cache_nsa_kv, z3)
            ckv = _compress(lidx, cmp_rows, None, past, p)
            win = jnp.concatenate([at(cache_win_kv).reshape(db, wbuf, winw),
                                   z3[:, :, C_KV + kvw:C_KV + kvw + winw]], axis=1)
            win_state["win"] = win
            pad = max(0, WINDOW + Q_BLOCK - (wbuf + ts))
            win_pad = jnp.concatenate([win, jnp.zeros((db, pad, winw), F32)], axis=1)
            return _attention(lidx, z3, ckv, sel_rows, 0, 2, win_pad, 0, 2, past, past - wbuf, F32)

        st_in = (at(state_lru_conv), at(state_lru_h)[:, None, :], at(state_gdn_conv), at(state_gdn_s),
                 at(state_s5_re).reshape(db // sb, sb, S5_G * S5_P), at(state_s5_im).reshape(db // sb, sb, S5_G * S5_P))
        xs, zs3, st_s = _mixer_layer(lidx, xs, db, ts, st_in, nsa_sample, p, wp, alpha, F32)

        def outs(st, z3, s, t, win):
            lru_buf, lru_h, gdn_buf, gdn_s, s5_re, s5_im = st
            return (lru_h.reshape(s, W_BR), lru_buf,
                    _kv_rows(z3).reshape(s, t, 4, NSA_KVH, NSA_HD),
                    win.reshape(s, win.shape[1], 2, NSA_KVH, NSA_HD),
                    gdn_s, gdn_buf, s5_re.reshape(s, S5_G, S5_P), s5_im.reshape(s, S5_G, S5_P))

        win_p = zp3[:, tp - min(WINDOW, tp):, C_KV + kvw:C_KV + kvw + winw]
        win_s = win_state["win"][:, wbuf + ts - min(WINDOW, past + ts):]
        return (xp, xs), (outs(st_p, zp3, bp, tp, win_p), outs(st_s, zs3, db, ts, win_s))

    (xp, xs), (op, os_) = lax.scan(layer, (x_prompt.reshape(bp * tp, D_MODEL), x_sample.reshape(db * ts, D_MODEL)),
                                   jnp.arange(depth, dtype=jnp.int32))
    return (xp.reshape(bp, tp, D_MODEL), xs.reshape(db, ts, D_MODEL)) + tuple(op) + tuple(os_)
```
